```python
import jax, jax.numpy as jnp
from jax import lax
import numpy as np

D_MODEL = 1024
BATCH = 8
SEQ = 2048
DEPTH = 1
DEC_BATCH = 128
DEC_SEQ = 8
PAST_LEN = 16384
PAGE_SIZE = 128

MIX_WIDTH = D_MODEL
CONV_DIM = MIX_WIDTH // 2
CONV_GROUPS = 8
CONV_K = 3
HEADS_B = 8
HEAD_DIM_B = (MIX_WIDTH - CONV_DIM) // HEADS_B
CHUNK_DIM = HEADS_B * HEAD_DIM_B
CHUNK = 128
D_FF = ((8 * D_MODEL // 3 + 127) // 128) * 128
PLE_DIM = 256
IN_COLS = 3 * CONV_DIM + 2 * CHUNK_DIM
EPS = 1e-6

kernel_name = "hymba_conv_gmlp_macaron_step"


def rmsnorm(x, g):
    xf = x.astype(jnp.float32)
    y = xf * lax.rsqrt(jnp.mean(xf * xf, axis=-1, keepdims=True) + EPS)
    return (y * g.astype(jnp.float32)).astype(x.dtype)


def swiglu(x, w_gate, w_up, w_down):
    return (jax.nn.silu(x @ w_gate) * (x @ w_up)) @ w_down


def causal_dwconv(prev, z, w):
    L = z.shape[1]
    full = jnp.concatenate([prev, z], axis=1)
    y = sum(w[k] * full[:, k:k + L] for k in range(CONV_K))
    return y, full[:, full.shape[1] - (CONV_K - 1):]


def chunk_spatial_mix(v, w_s, b_s):
    B, L, H, D = v.shape
    n_chunks = -(-L // CHUNK)
    pad = n_chunks * CHUNK - L
    vp = jnp.pad(v, ((0, 0), (0, pad), (0, 0), (0, 0))).reshape(B, n_chunks, CHUNK, H, D)
    causal = jnp.tril(jnp.ones((CHUNK, CHUNK), dtype=bool))
    wm = jnp.where(causal[None], w_s, jnp.zeros_like(w_s))
    out = jnp.einsum('hts,bcshd->bcthd', wm, vp) + b_s.T[None, None, :, :, None]
    return out.reshape(B, n_chunks * CHUNK, H, D)[:, :L]


def layer(x, p_emb, conv_prev,
          ffn1_pre_g, ffn1_post_g, ffn1_w_gate, ffn1_w_up, ffn1_w_down,
          mix_pre_g, mix_post_g, w_in, conv_w, v_norm_g, w_s, b_s, out_g_a, out_g_b, w_out,
          ffn2_pre_g, ffn2_post_g, ffn2_w_gate, ffn2_w_up, ffn2_w_down,
          ple_w_gate, ple_w_proj, ple_post_g):
    B, L, _ = x.shape
    h = x + 0.5 * rmsnorm(swiglu(rmsnorm(x, ffn1_pre_g), ffn1_w_gate, ffn1_w_up, ffn1_w_down), ffn1_post_g)
    n = rmsnorm(h, mix_pre_g)
    proj = n @ w_in
    o = 0
    b_a = proj[..., o:o + CONV_DIM]; o += CONV_DIM
    c_a = proj[..., o:o + CONV_DIM]; o += CONV_DIM
    h_a = proj[..., o:o + CONV_DIM]; o += CONV_DIM
    u = proj[..., o:o + CHUNK_DIM]; o += CHUNK_DIM
    v = proj[..., o:o + CHUNK_DIM]
    conv_out, new_conv = causal_dwconv(conv_prev, c_a * h_a, conv_w)
    y_a = b_a * conv_out
    v_n = rmsnorm(v.reshape(B, L, HEADS_B, HEAD_DIM_B), v_norm_g)
    mixed = chunk_spatial_mix(v_n, w_s, b_s)
    y_b = (u.reshape(B, L, HEADS_B, HEAD_DIM_B) * mixed).reshape(B, L, CHUNK_DIM)
    n_cur = ((L - 1) % CHUNK) + 1
    v_cur = v_n[:, L - n_cur:]
    y = jnp.concatenate([rmsnorm(y_a, out_g_a), rmsnorm(y_b, out_g_b)], axis=-1) @ w_out
    h = h + rmsnorm(y, mix_post_g)
    h = h + 0.5 * rmsnorm(swiglu(rmsnorm(h, ffn2_pre_g), ffn2_w_gate, ffn2_w_up, ffn2_w_down), ffn2_post_g)
    gate = jax.nn.sigmoid(h @ ple_w_gate)
    h = h + rmsnorm(gate * (p_emb @ ple_w_proj), ple_post_g)
    return h, new_conv, v_cur


def setup_inputs(seed: int = 0) -> dict:
    key = jax.random.key(seed)
    ks = iter(jax.random.split(key, 40))
    f32 = jnp.float32

    def nrm(shape, scale):
        return jax.random.normal(next(ks), shape, f32) * scale

    def gain(shape):
        return 1.0 + 0.05 * jax.random.normal(next(ks), shape, f32)

    d = {}
    d["x_prompt"] = nrm((BATCH, SEQ, D_MODEL), 1.0)
    d["x_sample"] = nrm((DEC_BATCH, DEC_SEQ, D_MODEL), 1.0)
    d["p_prompt"] = nrm((DEPTH, BATCH, SEQ, PLE_DIM), 1.0)
    d["p_sample"] = nrm((DEPTH, DEC_BATCH, DEC_SEQ, PLE_DIM), 1.0)
    d["state_conv"] = nrm((DEPTH, DEC_BATCH, CONV_K - 1, CONV_DIM), 1.0)
    d["ffn1_pre_g"] = gain((DEPTH, D_MODEL))
    d["ffn1_post_g"] = gain((DEPTH, D_MODEL))
    d["ffn1_w_gate"] = nrm((DEPTH, D_MODEL, D_FF), D_MODEL ** -0.5)
    d["ffn1_w_up"] = nrm((DEPTH, D_MODEL, D_FF), D_MODEL ** -0.5)
    d["ffn1_w_down"] = nrm((DEPTH, D_FF, D_MODEL), D_FF ** -0.5)
    d["mix_pre_g"] = gain((DEPTH, D_MODEL))
    d["mix_post_g"] = gain((DEPTH, D_MODEL))
    d["w_in"] = nrm((DEPTH, D_MODEL, IN_COLS), D_MODEL ** -0.5)
    d["conv_w"] = nrm((DEPTH, CONV_K, CONV_DIM), CONV_K ** -0.5)
    d["v_norm_g"] = gain((DEPTH, HEADS_B, HEAD_DIM_B))
    d["w_s"] = nrm((DEPTH, HEADS_B, CHUNK, CHUNK), CHUNK ** -0.5)
    d["b_s"] = 1.0 + nrm((DEPTH, HEADS_B, CHUNK), 0.1)
    d["out_g_a"] = gain((DEPTH, CONV_DIM))
    d["out_g_b"] = gain((DEPTH, CHUNK_DIM))
    d["w_out"] = nrm((DEPTH, MIX_WIDTH, D_MODEL), MIX_WIDTH ** -0.5)
    d["ffn2_pre_g"] = gain((DEPTH, D_MODEL))
    d["ffn2_post_g"] = gain((DEPTH, D_MODEL))
    d["ffn2_w_gate"] = nrm((DEPTH, D_MODEL, D_FF), D_MODEL ** -0.5)
    d["ffn2_w_up"] = nrm((DEPTH, D_MODEL, D_FF), D_MODEL ** -0.5)
    d["ffn2_w_down"] = nrm((DEPTH, D_FF, D_MODEL), D_FF ** -0.5)
    d["ple_w_gate"] = nrm((DEPTH, D_MODEL, D_MODEL), D_MODEL ** -0.5)
    d["ple_w_proj"] = nrm((DEPTH, PLE_DIM, D_MODEL), PLE_DIM ** -0.5)
    d["ple_post_g"] = gain((DEPTH, D_MODEL))
    return d


def reference(x_prompt, x_sample, p_prompt, p_sample, state_conv,
              ffn1_pre_g, ffn1_post_g, ffn1_w_gate, ffn1_w_up, ffn1_w_down,
              mix_pre_g, mix_post_g, w_in, conv_w, v_norm_g, w_s, b_s, out_g_a, out_g_b, w_out,
              ffn2_pre_g, ffn2_post_g, ffn2_w_gate, ffn2_w_up, ffn2_w_down,
              ple_w_gate, ple_w_proj, ple_post_g):
    hp, hs = x_prompt, x_sample
    conv_p_list, conv_s_list, vp_list, vs_list = [], [], [], []
    for i in range(DEPTH):
        w = (ffn1_pre_g[i], ffn1_post_g[i], ffn1_w_gate[i], ffn1_w_up[i], ffn1_w_down[i],
             mix_pre_g[i], mix_post_g[i], w_in[i], conv_w[i], v_norm_g[i], w_s[i], b_s[i],
             out_g_a[i], out_g_b[i], w_out[i],
             ffn2_pre_g[i], ffn2_post_g[i], ffn2_w_gate[i], ffn2_w_up[i], ffn2_w_down[i],
             ple_w_gate[i], ple_w_proj[i], ple_post_g[i])
        zeros_prev = jnp.zeros((hp.shape[0], CONV_K - 1, CONV_DIM), hp.dtype)
        hp, conv_p, v_p = layer(hp, p_prompt[i], zeros_prev, *w)
        hs, conv_s, v_s = layer(hs, p_sample[i], state_conv[i].astype(hs.dtype), *w)
        conv_p_list.append(conv_p)
        conv_s_list.append(conv_s)
        vp_list.append(v_p)
        vs_list.append(v_s)
    new_conv_prompt = jnp.stack(conv_p_list)
    new_conv_sample = jnp.stack(conv_s_list)
    chunk_v_prompt = jnp.stack(vp_list)
    chunk_v_sample = jnp.stack(vs_list)
    return (hp, hs, new_conv_prompt, new_conv_sample, chunk_v_prompt, chunk_v_sample)
```

```python
import functools

import jax
import jax.numpy as jnp
from jax import lax
from jax.experimental import pallas as pl
from jax.experimental.pallas import tpu as pltpu

D_MODEL = 1024
CONV_DIM = 512
CONV_K = 3
HEADS_B = 8
HEAD_DIM_B = 64
CHUNK_DIM = HEADS_B * HEAD_DIM_B
CHUNK = 128
EPS = 1e-6

V7X_SUBLANES = 8
V7X_MXU_COLS = 256
V7X_VMEM_LIMIT_BYTES = 56 * 1024 * 1024

TOKEN_BLOCK = 512
HEADS_PER_SLAB = V7X_MXU_COLS // HEAD_DIM_B

BF16 = jnp.bfloat16
F32 = jnp.float32


def _rms(x, g):
    return x * lax.rsqrt(jnp.mean(x * x, axis=-1, keepdims=True) + EPS) * g


def _dot(a, b):
    return jnp.dot(a.astype(BF16), b, preferred_element_type=F32)


def _swiglu_half_step(x, pre_g, post_g, wg_ref, wu_ref, wd_ref):
    xn = _rms(x, pre_g).astype(BF16)
    g = jnp.dot(xn, wg_ref[...], preferred_element_type=F32)
    u = jnp.dot(xn, wu_ref[...], preferred_element_type=F32)
    a = (g * jax.nn.sigmoid(g)) * u
    y = _dot(a, wd_ref[...])
    return x + 0.5 * _rms(y, post_g)


def _ffn1_kernel(x_ref, pre_ref, post_ref, wg_ref, wu_ref, wd_ref, o_ref):
    o_ref[...] = _swiglu_half_step(x_ref[...], pre_ref[...], post_ref[...], wg_ref, wu_ref, wd_ref)


def _ffn2_ple_kernel(h_ref, p_ref, pre_ref, post_ref, wg_ref, wu_ref, wd_ref,
                     wpg_ref, wpp_ref, ple_g_ref, o_ref):
    h = _swiglu_half_step(h_ref[...], pre_ref[...], post_ref[...], wg_ref, wu_ref, wd_ref)
    gate = jax.nn.sigmoid(_dot(h, wpg_ref[...]))
    proj = _dot(p_ref[...], wpp_ref[...])
    o_ref[...] = h + _rms(gate * proj, ple_g_ref[...])


def _head_rms(v, ones_bd, g):
    ms = jnp.dot((v * v).astype(BF16), ones_bd, preferred_element_type=F32)
    return v * lax.rsqrt(ms + EPS) * g


def _mix_tail(h, b_a, conv_out, u, mixed, ga_ref, gb_ref, wout_ref, post_ref):
    y_a = _rms(b_a * conv_out, ga_ref[...])
    y_b = _rms(u * mixed, gb_ref[...])
    y = _dot(y_a, wout_ref[0:CONV_DIM, :]) + _dot(y_b, wout_ref[CONV_DIM:, :])
    return h + _rms(y, post_ref[...])


def _in_proj(h, pre_ref, win_ref):
    n = _rms(h, pre_ref[...]).astype(BF16)
    seg = lambda k: jnp.dot(n, win_ref[:, k * CONV_DIM:(k + 1) * CONV_DIM], preferred_element_type=F32)
    return seg(0), seg(1), seg(2), seg(3), seg(4)


def _mix_prompt_kernel(h_ref, pre_ref, post_ref, win_ref, convw_ref, vg_ref, ones_ref,
                       wcat_ref, bias_ref, ga_ref, gb_ref, wout_ref,
                       o_ref, conv_o_ref, v_o_ref, carry_ref, *, blocks_per_seq):
    tm = h_ref.shape[0]
    i = pl.program_id(0)

    @pl.when(i % blocks_per_seq == 0)
    def _():
        carry_ref[...] = jnp.zeros_like(carry_ref)

    h = h_ref[...]
    b_a, c_a, h_a, u, v = _in_proj(h, pre_ref, win_ref)

    z = c_a * h_a
    row = lax.broadcasted_iota(jnp.int32, z.shape, 0)
    prev1 = carry_ref[1:2, :]
    prev2 = carry_ref[0:1, :]
    z1 = jnp.where(row == 0, prev1, pltpu.roll(z, 1, axis=0))
    z2 = jnp.where(row == 0, prev2, jnp.where(row == 1, prev1, pltpu.roll(z, 2, axis=0)))
    conv_out = convw_ref[0:1, :] * z2 + convw_ref[1:2, :] * z1 + convw_ref[2:3, :] * z
    tail = z[tm - (CONV_K - 1):, :]
    carry_ref[...] = tail
    conv_o_ref[0] = tail

    v_n = _head_rms(v, ones_ref[...], vg_ref[...])
    v_o_ref[0] = v_n[tm - CHUNK:, :]
    t_idx = lax.broadcasted_iota(jnp.int32, (CHUNK, HEADS_PER_SLAB * CHUNK), 0)
    s_idx = lax.broadcasted_iota(jnp.int32, (CHUNK, HEADS_PER_SLAB * CHUNK), 1) % CHUNK
    causal = t_idx >= s_idx
    lane_head = lax.broadcasted_iota(jnp.int32, (CHUNK, V7X_MXU_COLS), 1) // HEAD_DIM_B
    n_slabs = CHUNK_DIM // V7X_MXU_COLS
    chunks = []
    for c in range(tm // CHUNK):
        slabs = []
        for k in range(n_slabs):
            w_k = jnp.where(causal, wcat_ref[k], jnp.zeros((), BF16))
            v_ck = v_n[c * CHUNK:(c + 1) * CHUNK, k * V7X_MXU_COLS:(k + 1) * V7X_MXU_COLS].astype(BF16)
            bd = jnp.concatenate(
                [jnp.where(lane_head == j, v_ck, jnp.zeros((), BF16)) for j in range(HEADS_PER_SLAB)], axis=0)
            slabs.append(jnp.dot(w_k, bd, preferred_element_type=F32))
        chunks.append(jnp.concatenate(slabs, axis=1) + bias_ref[...])
    mixed = jnp.concatenate(chunks, axis=0)

    o_ref[...] = _mix_tail(h, b_a, conv_out, u, mixed, ga_ref, gb_ref, wout_ref, post_ref)


def _mix_sample_kernel(h_ref, state_ref, pre_ref, post_ref, win_ref, convw_ref, vg_ref, ones_ref,
                       wdiag_ref, bias_ref, ga_ref, gb_ref, wout_ref,
                       o_ref, conv_o_ref, v_o_ref, *, seq_len):
    tm = h_ref.shape[0]
    h = h_ref[...]
    b_a, c_a, h_a, u, v = _in_proj(h, pre_ref, win_ref)

    z = c_a * h_a
    t = lax.broadcasted_iota(jnp.int32, z.shape, 0) % seq_len
    st = state_ref[...]
    z1 = jnp.where(t == 0, pltpu.roll(st, tm - 1, axis=0), pltpu.roll(z, 1, axis=0))
    z2 = jnp.where(t < 2, st, pltpu.roll(z, 2, axis=0))
    conv_out = convw_ref[0:1, :] * z2 + convw_ref[1:2, :] * z1 + convw_ref[2:3, :] * z
    conv_o_ref[...] = z.reshape(tm // seq_len, seq_len, CONV_DIM)[:, seq_len - (CONV_K - 1):, :]

    v_n = _head_rms(v, ones_ref[...], vg_ref[...])
    v_o_ref[...] = v_n
    mixed = jnp.zeros_like(v_n)
    for d in range(seq_len):
        shifted = v_n if d == 0 else pltpu.roll(v_n, d, axis=0)
        coef = jnp.tile(wdiag_ref[d], (tm // seq_len, 1))
        mixed = mixed + jnp.where(t >= d, coef, 0.0) * shifted
    mixed = mixed + jnp.tile(bias_ref[...], (tm // seq_len, 1))

    o_ref[...] = _mix_tail(h, b_a, conv_out, u, mixed, ga_ref, gb_ref, wout_ref, post_ref)


def _resident(shape):
    return pl.BlockSpec(shape, lambda i: (0,) * len(shape), pipeline_mode=pl.Buffered(1))


def _tok(tm, d):
    return pl.BlockSpec((tm, d), lambda i: (i, 0))


def _params():
    return pltpu.CompilerParams(dimension_semantics=("arbitrary",), vmem_limit_bytes=V7X_VMEM_LIMIT_BYTES)


def _ffn1(x, pre_g, post_g, wg, wu, wd, tm):
    t, d = x.shape
    return pl.pallas_call(
        _ffn1_kernel,
        grid=(t // tm,),
        in_specs=[_tok(tm, d), _resident(pre_g.shape), _resident(post_g.shape),
                  _resident(wg.shape), _resident(wu.shape), _resident(wd.shape)],
        out_specs=_tok(tm, d),
        out_shape=jax.ShapeDtypeStruct((t, d), F32),
        compiler_params=_params(),
        name="ffn1",
    )(x, pre_g, post_g, wg, wu, wd)


def _ffn2_ple(h, p, pre_g, post_g, wg, wu, wd, wpg, wpp, ple_g, tm):
    t, d = h.shape
    return pl.pallas_call(
        _ffn2_ple_kernel,
        grid=(t // tm,),
        in_specs=[_tok(tm, d), _tok(tm, p.shape[1]), _resident(pre_g.shape), _resident(post_g.shape),
                  _resident(wg.shape), _resident(wu.shape), _resident(wd.shape),
                  _resident(wpg.shape), _resident(wpp.shape), _resident(ple_g.shape)],
        out_specs=_tok(tm, d),
        out_shape=jax.ShapeDtypeStruct((t, d), F32),
        compiler_params=_params(),
        name="ffn2_ple",
    )(h, p, pre_g, post_g, wg, wu, wd, wpg, wpp, ple_g)


def _mix_prompt(h, batch, seq, mw, tm):
    t, d = h.shape
    blocks_per_seq = seq // tm
    small = [mw["pre"], mw["post"], mw["w_in"], mw["conv_w"], mw["v_g"], mw["ones_bd"],
             mw["wcat"], mw["bias_chunk"], mw["g_a"], mw["g_b"], mw["w_out"]]
    per_seq = lambda i: (i // blocks_per_seq, 0, 0)
    return pl.pallas_call(
        functools.partial(_mix_prompt_kernel, blocks_per_seq=blocks_per_seq),
        grid=(t // tm,),
        in_specs=[_tok(tm, d)] + [_resident(a.shape) for a in small],
        out_specs=[_tok(tm, d),
                   pl.BlockSpec((1, CONV_K - 1, CONV_DIM), per_seq),
                   pl.BlockSpec((1, CHUNK, CHUNK_DIM), per_seq)],
        out_shape=[jax.ShapeDtypeStruct((t, d), F32),
                   jax.ShapeDtypeStruct((batch, CONV_K - 1, CONV_DIM), F32),
                   jax.ShapeDtypeStruct((batch, CHUNK, CHUNK_DIM), F32)],
        scratch_shapes=[pltpu.VMEM((CONV_K - 1, CONV_DIM), F32)],
        compiler_params=_params(),
        name="mix_prompt",
    )(h, *small)


def _mix_sample(h, state_rows, batch, seq, mw, tm):
    t, d = h.shape
    seqs_per_block = tm // seq
    small = [mw["pre"], mw["post"], mw["w_in"], mw["conv_w"], mw["v_g"], mw["ones_bd"],
             mw["wdiag"], mw["bias_head"], mw["g_a"], mw["g_b"], mw["w_out"]]
    return pl.pallas_call(
        functools.partial(_mix_sample_kernel, seq_len=seq),
        grid=(t // tm,),
        in_specs=[_tok(tm, d), _tok(tm, CONV_DIM)] + [_resident(a.shape) for a in small],
        out_specs=[_tok(tm, d),
                   pl.BlockSpec((seqs_per_block, CONV_K - 1, CONV_DIM), lambda i: (i, 0, 0)),
                   _tok(tm, CHUNK_DIM)],
        out_shape=[jax.ShapeDtypeStruct((t, d), F32),
                   jax.ShapeDtypeStruct((batch, CONV_K - 1, CONV_DIM), F32),
                   jax.ShapeDtypeStruct((t, CHUNK_DIM), F32)],
        compiler_params=_params(),
        name="mix_sample",
    )(h, state_rows, *small)


def _row(g):
    return g.reshape(1, -1).astype(F32)


def _layer_weights(i, seq_s, ffn1_pre_g, ffn1_post_g, ffn1_w_gate, ffn1_w_up, ffn1_w_down,
                   mix_pre_g, mix_post_g, w_in, conv_w, v_norm_g, w_s, b_s, out_g_a, out_g_b, w_out,
                   ffn2_pre_g, ffn2_post_g, ffn2_w_gate, ffn2_w_up, ffn2_w_down,
                   ple_w_gate, ple_w_proj, ple_post_g):
    ffn1 = (_row(ffn1_pre_g[i]), _row(ffn1_post_g[i]),
            ffn1_w_gate[i].astype(BF16), ffn1_w_up[i].astype(BF16), ffn1_w_down[i].astype(BF16))
    ffn2 = (_row(ffn2_pre_g[i]), _row(ffn2_post_g[i]),
            ffn2_w_gate[i].astype(BF16), ffn2_w_up[i].astype(BF16), ffn2_w_down[i].astype(BF16),
            ple_w_gate[i].astype(BF16), ple_w_proj[i].astype(BF16), _row(ple_post_g[i]))
    ws = w_s[i]
    n_slabs = HEADS_B // HEADS_PER_SLAB
    wcat = ws.reshape(n_slabs, HEADS_PER_SLAB, CHUNK, CHUNK).transpose(0, 2, 1, 3)
    wcat = wcat.reshape(n_slabs, CHUNK, HEADS_PER_SLAB * CHUNK).astype(BF16)
    head_of_col = jnp.arange(CHUNK_DIM) // HEAD_DIM_B
    ones_bd = (head_of_col[:, None] == head_of_col[None, :]).astype(F32) / HEAD_DIM_B
    tt = jnp.arange(seq_s)
    src = jnp.clip(tt[None, :] - tt[:, None], 0, seq_s - 1)
    wdiag = ws[:, tt[None, :], src]
    wdiag = jnp.repeat(wdiag.transpose(1, 2, 0), HEAD_DIM_B, axis=-1)
    bias_chunk = jnp.repeat(b_s[i].T, HEAD_DIM_B, axis=-1)
    mix = dict(pre=_row(mix_pre_g[i]), post=_row(mix_post_g[i]), w_in=w_in[i].astype(BF16),
               conv_w=conv_w[i].astype(F32), v_g=_row(v_norm_g[i]), ones_bd=ones_bd.astype(BF16),
               wcat=wcat, wdiag=wdiag.astype(F32), bias_chunk=bias_chunk.astype(F32),
               bias_head=bias_chunk[:seq_s].astype(F32),
               g_a=_row(out_g_a[i]), g_b=_row(out_g_b[i]), w_out=w_out[i].astype(BF16))
    return ffn1, mix, ffn2


def kernel(x_prompt, x_sample, p_prompt, p_sample, state_conv, ffn1_pre_g, ffn1_post_g, ffn1_w_gate, ffn1_w_up, ffn1_w_down, mix_pre_g, mix_post_g, w_in, conv_w, v_norm_g, w_s, b_s, out_g_a, out_g_b, w_out, ffn2_pre_g, ffn2_post_g, ffn2_w_gate, ffn2_w_up, ffn2_w_down, ple_w_gate, ple_w_proj, ple_post_g):
    depth = ffn1_pre_g.shape[0]
    bp, lp, d = x_prompt.shape
    bs, ls, _ = x_sample.shape
    assert lp % TOKEN_BLOCK == 0 and (bs * ls) % TOKEN_BLOCK == 0 and TOKEN_BLOCK % ls == 0
    assert ls <= V7X_SUBLANES and ls >= CONV_K - 1 and (lp - 1) % CHUNK + 1 == CHUNK
    tm = TOKEN_BLOCK
    hp = x_prompt.reshape(bp * lp, d)
    hs = x_sample.reshape(bs * ls, d)
    conv_p, conv_s, v_p, v_s = [], [], [], []
    for i in range(depth):
        ffn1, mix, ffn2 = _layer_weights(
            i, ls, ffn1_pre_g, ffn1_post_g, ffn1_w_gate, ffn1_w_up, ffn1_w_down,
            mix_pre_g, mix_post_g, w_in, conv_w, v_norm_g, w_s, b_s, out_g_a, out_g_b, w_out,
            ffn2_pre_g, ffn2_post_g, ffn2_w_gate, ffn2_w_up, ffn2_w_down,
            ple_w_gate, ple_w_proj, ple_post_g)
        state_rows = jnp.pad(state_conv[i].astype(F32), ((0, 0), (0, ls - (CONV_K - 1)), (0, 0)))
        state_rows = state_rows.reshape(bs * ls, CONV_DIM)

        hp = _ffn1(hp, *ffn1, tm)
        hp, cp, vp = _mix_prompt(hp, bp, lp, mix, tm)
        hp = _ffn2_ple(hp, p_prompt[i].reshape(bp * lp, -1), *ffn2, tm)

        hs = _ffn1(hs, *ffn1, tm)
        hs, cs, vs = _mix_sample(hs, state_rows, bs, ls, mix, tm)
        hs = _ffn2_ple(hs, p_sample[i].reshape(bs * ls, -1), *ffn2, tm)

        conv_p.append(cp)
        conv_s.append(cs)
        v_p.append(vp.reshape(bp, CHUNK, HEADS_B, HEAD_DIM_B))
        v_s.append(vs.reshape(bs, ls, HEADS_B, HEAD_DIM_B))
    return (hp.reshape(bp, lp, d), hs.reshape(bs, ls, d),
            jnp.stack(conv_p), jnp.stack(conv_s), jnp.stack(v_p), jnp.stack(v_s))
```

```python
import functools

import jax
import jax.numpy as jnp
from jax import lax
from jax.experimental import pallas as pl
from jax.experimental.pallas import tpu as pltpu

CONV_K = 3
HEADS_B = 8
HEAD_DIM_B = 64
CHUNK_DIM = HEADS_B * HEAD_DIM_B
CONV_DIM = CHUNK_DIM
CHUNK = 128
EPS = 1e-6

V7X_SUBLANES = 8
V7X_MXU_COLS = 256
V7X_VMEM_LIMIT_BYTES = 58 * 1024 * 1024

TOKEN_BLOCK = 512
WEIGHT_TILE = V7X_MXU_COLS
HEADS_PER_SLAB = V7X_MXU_COLS // HEAD_DIM_B
MIX_SEGMENTS = 5

BF16 = jnp.bfloat16
F32 = jnp.float32


def _rms(x, g):
    return x * lax.rsqrt(jnp.mean(x * x, axis=-1, keepdims=True) + EPS) * g


def _dot(a, b):
    return jnp.dot(a.astype(BF16), b, preferred_element_type=F32)


def _store_piece(dst_ref, src_ref, step, n_pieces):
    rows = src_ref.shape[0]

    @pl.when(step < n_pieces)
    def _():
        dst_ref[pl.ds(pl.multiple_of(step * rows, rows), rows), :] = src_ref[...].astype(BF16)


def _ffn_kernel(*refs, n_prep, n_prompt, with_ple):
    if with_ple:
        (xp_ref, xs_ref, pp_ref, ps_ref, pre_ref, post_ref, wg_ref, wu_ref, wd_ref, wpg_ref, wpp_ref, pleg_ref,
         op_ref, os_ref, wg_s, wu_s, wd_s, xn_s, a_s, wpg_s, wpp_s) = refs
    else:
        (xp_ref, xs_ref, pre_ref, post_ref, wg_ref, wu_ref, wd_ref,
         op_ref, os_ref, wg_s, wu_s, wd_s, xn_s, a_s) = refs
    i = pl.program_id(0)

    @pl.when(i < n_prep)
    def _():
        wg_s[i] = wg_ref[...].astype(BF16)
        wu_s[i] = wu_ref[...].astype(BF16)

    _store_piece(wd_s, wd_ref, i, n_prep)
    if with_ple:
        _store_piece(wpg_s, wpg_ref, i, wpg_s.shape[0] // wpg_ref.shape[0])
        _store_piece(wpp_s, wpp_ref, i, 1)

    @pl.when(i >= n_prep)
    def _():
        is_sample = i >= n_prep + n_prompt
        x = jnp.where(is_sample, xs_ref[...], xp_ref[...])
        xn_s[...] = _rms(x, pre_ref[...]).astype(BF16)
        for c in range(n_prep):
            g = jnp.dot(xn_s[...], wg_s[c], preferred_element_type=F32)
            u = jnp.dot(xn_s[...], wu_s[c], preferred_element_type=F32)
            a_s[:, c * WEIGHT_TILE:(c + 1) * WEIGHT_TILE] = ((g * jax.nn.sigmoid(g)) * u).astype(BF16)
        y = jnp.dot(a_s[...], wd_s[...], preferred_element_type=F32)
        h = x + 0.5 * _rms(y, post_ref[...])
        if with_ple:
            p = jnp.where(is_sample, ps_ref[...], pp_ref[...])
            gate = jax.nn.sigmoid(_dot(h, wpg_s[...]))
            proj = _dot(p, wpp_s[...])
            h = h + _rms(gate * proj, pleg_ref[...])

        @pl.when(is_sample)
        def _():
            os_ref[...] = h

        @pl.when(jnp.logical_not(is_sample))
        def _():
            op_ref[...] = h


def _head_rms(v, ones_bd, g):
    ms = jnp.dot((v * v).astype(BF16), ones_bd, preferred_element_type=F32)
    return v * lax.rsqrt(ms + EPS) * g


def _mix_kernel(hp_ref, hs_ref, state_ref, pre_ref, post_ref, win_ref, wout_ref, convw_ref, vg_ref, ones_ref,
                wcat_ref, bias_chunk_ref, wdiag_ref, bias_head_ref, ga_ref, gb_ref,
                op_ref, os_ref, convp_ref, vp_ref, convs_ref, vs_ref,
                win_s, wout_s, n_s, cv_s, mx_s, carry_ref,
                *, n_prep, n_prompt, blocks_per_seq, seq_len):
    tm = hp_ref.shape[0]
    i = pl.program_id(0)

    @pl.when(i < n_prep)
    def _():
        win_s[i] = win_ref[...].astype(BF16)

    _store_piece(wout_s, wout_ref, i, wout_s.shape[0] // wout_ref.shape[0])

    def seg(k):
        return jnp.dot(n_s[...], win_s[k], preferred_element_type=F32)

    def conv(z, z1, z2):
        return convw_ref[0:1, :] * z2 + convw_ref[1:2, :] * z1 + convw_ref[2:3, :] * z

    @pl.when(i >= n_prep)
    def _():
        is_sample = i >= n_prep + n_prompt
        h = jnp.where(is_sample, hs_ref[...], hp_ref[...])
        n_s[...] = _rms(h, pre_ref[...]).astype(BF16)

        @pl.when(jnp.logical_not(is_sample))
        def _():
            @pl.when((i - n_prep) % blocks_per_seq == 0)
            def _():
                carry_ref[...] = jnp.zeros_like(carry_ref)

            z = seg(1) * seg(2)
            row = lax.broadcasted_iota(jnp.int32, z.shape, 0)
            prev1 = carry_ref[1:2, :]
            prev2 = carry_ref[0:1, :]
            z1 = jnp.where(row == 0, prev1, pltpu.roll(z, 1, axis=0))
            z2 = jnp.where(row == 0, prev2, jnp.where(row == 1, prev1, pltpu.roll(z, 2, axis=0)))
            cv_s[...] = conv(z, z1, z2)
            tail = z[tm - (CONV_K - 1):, :]
            carry_ref[...] = tail
            convp_ref[0] = tail

            v_n = _head_rms(seg(4), ones_ref[...], vg_ref[...])
            vp_ref[0] = v_n[tm - CHUNK:, :]
            t_idx = lax.broadcasted_iota(jnp.int32, (CHUNK, HEADS_PER_SLAB * CHUNK), 0)
            s_idx = lax.broadcasted_iota(jnp.int32, (CHUNK, HEADS_PER_SLAB * CHUNK), 1) % CHUNK
            causal = t_idx >= s_idx
            lane_head = lax.broadcasted_iota(jnp.int32, (CHUNK, V7X_MXU_COLS), 1) // HEAD_DIM_B
            for c in range(tm // CHUNK):
                for k in range(CHUNK_DIM // V7X_MXU_COLS):
                    cols = slice(k * V7X_MXU_COLS, (k + 1) * V7X_MXU_COLS)
                    w_k = jnp.where(causal, wcat_ref[k], jnp.zeros((), BF16))
                    v_ck = v_n[c * CHUNK:(c + 1) * CHUNK, cols].astype(BF16)
                    bd = jnp.concatenate(
                        [jnp.where(lane_head == j, v_ck, jnp.zeros((), BF16)) for j in range(HEADS_PER_SLAB)], axis=0)
                    mx_s[c * CHUNK:(c + 1) * CHUNK, cols] = (
                        jnp.dot(w_k, bd, preferred_element_type=F32) + bias_chunk_ref[:, cols])

        @pl.when(is_sample)
        def _():
            z = seg(1) * seg(2)
            t = lax.broadcasted_iota(jnp.int32, z.shape, 0) % seq_len
            st = state_ref[...]
            z1 = jnp.where(t == 0, pltpu.roll(st, tm - 1, axis=0), pltpu.roll(z, 1, axis=0))
            z2 = jnp.where(t < 2, st, pltpu.roll(z, 2, axis=0))
            cv_s[...] = conv(z, z1, z2)
            convs_ref[...] = z.reshape(tm // seq_len, seq_len, CONV_DIM)[:, seq_len - (CONV_K - 1):, :]

            v_n = _head_rms(seg(4), ones_ref[...], vg_ref[...])
            vs_ref[...] = v_n
            tile_rows = lambda a: jnp.broadcast_to(a[None], (tm // seq_len,) + a.shape).reshape(tm, a.shape[-1])
            mixed = tile_rows(bias_head_ref[...])
            for d in range(seq_len):
                shifted = v_n if d == 0 else pltpu.roll(v_n, d, axis=0)
                mixed = mixed + jnp.where(t >= d, tile_rows(wdiag_ref[d]), 0.0) * shifted
            mx_s[...] = mixed

        y_a = _rms(seg(0) * cv_s[...], ga_ref[...])
        y_b = _rms(seg(3) * mx_s[...], gb_ref[...])
        y = _dot(y_a, wout_s[0:CONV_DIM, :]) + _dot(y_b, wout_s[CONV_DIM:, :])
        out = h + _rms(y, post_ref[...])

        @pl.when(is_sample)
        def _():
            os_ref[...] = out

        @pl.when(jnp.logical_not(is_sample))
        def _():
            op_ref[...] = out


def _resident(a):
    return pl.BlockSpec(a.shape, lambda i: (0,) * a.ndim, pipeline_mode=pl.Buffered(1))


def _clamped(lo, n):
    return lambda i: jnp.minimum(jnp.maximum(i - lo, 0), n - 1)


def _stream(tm, d, lo, n):
    blk = _clamped(lo, n)
    return pl.BlockSpec((tm, d), lambda i: (blk(i), 0))


def _row_pieces(w, rows):
    blk = _clamped(0, w.shape[0] // rows)
    return pl.BlockSpec((rows, w.shape[1]), lambda i: (blk(i), 0))


def _col_pieces(w, cols):
    blk = _clamped(0, w.shape[1] // cols)
    return pl.BlockSpec((w.shape[0], cols), lambda i: (0, blk(i)))


def _params():
    return pltpu.CompilerParams(dimension_semantics=("arbitrary",), vmem_limit_bytes=V7X_VMEM_LIMIT_BYTES)


def _ffn(xp, xs, pre_g, post_g, wg, wu, wd, ple=None, *, name):
    tm = TOKEN_BLOCK
    d, d_ff = wg.shape
    n_prep = d_ff // WEIGHT_TILE
    n_prompt, n_sample = xp.shape[0] // tm, xs.shape[0] // tm
    tok = lambda width: [_stream(tm, width, n_prep, n_prompt), _stream(tm, width, n_prep + n_prompt, n_sample)]
    in_specs = tok(d)
    args = [xp, xs]
    if ple is not None:
        pp, ps, wpg, wpp, ple_g = ple
        in_specs += tok(pp.shape[1])
        args += [pp, ps]
    in_specs += [_resident(pre_g), _resident(post_g),
                 _col_pieces(wg, WEIGHT_TILE), _col_pieces(wu, WEIGHT_TILE), _row_pieces(wd, WEIGHT_TILE)]
    args += [pre_g, post_g, wg, wu, wd]
    scratch = [pltpu.VMEM((n_prep, d, WEIGHT_TILE), BF16), pltpu.VMEM((n_prep, d, WEIGHT_TILE), BF16),
               pltpu.VMEM((d_ff, d), BF16), pltpu.VMEM((tm, d), BF16), pltpu.VMEM((tm, d_ff), BF16)]
    if ple is not None:
        assert wpg.shape[0] // (WEIGHT_TILE // 2) <= n_prep
        in_specs += [_row_pieces(wpg, WEIGHT_TILE // 2), _row_pieces(wpp, wpp.shape[0]), _resident(ple_g)]
        args += [wpg, wpp, ple_g]
        scratch += [pltpu.VMEM(wpg.shape, BF16), pltpu.VMEM(wpp.shape, BF16)]
    return pl.pallas_call(
        functools.partial(_ffn_kernel, n_prep=n_prep, n_prompt=n_prompt, with_ple=ple is not None),
        grid=(n_prep + n_prompt + n_sample,),
        in_specs=in_specs,
        out_specs=tok(d),
        out_shape=[jax.ShapeDtypeStruct(xp.shape, F32), jax.ShapeDtypeStruct(xs.shape, F32)],
        scratch_shapes=scratch,
        compiler_params=_params(),
        name=name,
    )(*args)


def _mix(hp, hs, state_rows, batch_p, seq_p, batch_s, seq_s, mw):
    tm = TOKEN_BLOCK
    d = hp.shape[1]
    n_prep = MIX_SEGMENTS
    n_prompt, n_sample = hp.shape[0] // tm, hs.shape[0] // tm
    blocks_per_seq = seq_p // tm
    seqs_per_block = tm // seq_s
    assert d // WEIGHT_TILE <= n_prep
    small = [mw["conv_w"], mw["v_g"], mw["ones_bd"], mw["wcat"], mw["bias_chunk"], mw["wdiag"], mw["bias_head"],
             mw["g_a"], mw["g_b"]]
    lo_s = n_prep + n_prompt
    prompt_seq = _clamped(0, batch_p)
    per_prompt_seq = lambda i: (prompt_seq((i - n_prep) // blocks_per_seq), 0, 0)
    sample_blk = _clamped(lo_s, n_sample)
    return pl.pallas_call(
        functools.partial(_mix_kernel, n_prep=n_prep, n_prompt=n_prompt, blocks_per_seq=blocks_per_seq,
                          seq_len=seq_s),
        grid=(n_prep + n_prompt + n_sample,),
        in_specs=[_stream(tm, d, n_prep, n_prompt), _stream(tm, d, lo_s, n_sample),
                  _stream(tm, CONV_DIM, lo_s, n_sample), _resident(mw["pre"]), _resident(mw["post"]),
                  _col_pieces(mw["w_in"], CONV_DIM), _row_pieces(mw["w_out"], WEIGHT_TILE)]
                 + [_resident(a) for a in small],
        out_specs=[_stream(tm, d, n_prep, n_prompt), _stream(tm, d, lo_s, n_sample),
                   pl.BlockSpec((1, CONV_K - 1, CONV_DIM), per_prompt_seq),
                   pl.BlockSpec((1, CHUNK, CHUNK_DIM), per_prompt_seq),
                   pl.BlockSpec((seqs_per_block, CONV_K - 1, CONV_DIM), lambda i: (sample_blk(i), 0, 0)),
                   _stream(tm, CHUNK_DIM, lo_s, n_sample)],
        out_shape=[jax.ShapeDtypeStruct(hp.shape, F32), jax.ShapeDtypeStruct(hs.shape, F32),
                   jax.ShapeDtypeStruct((batch_p, CONV_K - 1, CONV_DIM), F32),
                   jax.ShapeDtypeStruct((batch_p, CHUNK, CHUNK_DIM), F32),
                   jax.ShapeDtypeStruct((batch_s, CONV_K - 1, CONV_DIM), F32),
                   jax.ShapeDtypeStruct((hs.shape[0], CHUNK_DIM), F32)],
        scratch_shapes=[pltpu.VMEM((MIX_SEGMENTS, d, CONV_DIM), BF16), pltpu.VMEM(mw["w_out"].shape, BF16),
                        pltpu.VMEM((tm, d), BF16), pltpu.VMEM((tm, CONV_DIM), F32),
                        pltpu.VMEM((tm, CHUNK_DIM), F32), pltpu.VMEM((CONV_K - 1, CONV_DIM), F32)],
        compiler_params=_params(),
        name="mix",
    )(hp, hs, state_rows, mw["pre"], mw["post"], mw["w_in"], mw["w_out"], *small)


def _row(g):
    return g.reshape(1, -1).astype(F32)


def _mix_weights(seq_s, mix_pre_g, mix_post_g, w_in, conv_w, v_norm_g, w_s, b_s, out_g_a, out_g_b, w_out):
    n_slabs = HEADS_B // HEADS_PER_SLAB
    wcat = w_s.reshape(n_slabs, HEADS_PER_SLAB, CHUNK, CHUNK).transpose(0, 2, 1, 3)
    wcat = wcat.reshape(n_slabs, CHUNK, HEADS_PER_SLAB * CHUNK).astype(BF16)
    head_of_col = jnp.arange(CHUNK_DIM) // HEAD_DIM_B
    ones_bd = (head_of_col[:, None] == head_of_col[None, :]).astype(F32) / HEAD_DIM_B
    tt = jnp.arange(seq_s)
    src = jnp.clip(tt[None, :] - tt[:, None], 0, seq_s - 1)
    wdiag = w_s[:, tt[None, :], src]
    wdiag = jnp.repeat(wdiag.transpose(1, 2, 0), HEAD_DIM_B, axis=-1)
    bias_chunk = jnp.repeat(b_s.T, HEAD_DIM_B, axis=-1)
    return dict(pre=_row(mix_pre_g), post=_row(mix_post_g), w_in=w_in, w_out=w_out,
                conv_w=conv_w.astype(F32), v_g=_row(v_norm_g), ones_bd=ones_bd.astype(BF16),
                wcat=wcat, wdiag=wdiag.astype(F32), bias_chunk=bias_chunk.astype(F32),
                bias_head=bias_chunk[:seq_s].astype(F32), g_a=_row(out_g_a), g_b=_row(out_g_b))


def kernel(x_prompt, x_sample, p_prompt, p_sample, state_conv, ffn1_pre_g, ffn1_post_g, ffn1_w_gate, ffn1_w_up, ffn1_w_down, mix_pre_g, mix_post_g, w_in, conv_w, v_norm_g, w_s, b_s, out_g_a, out_g_b, w_out, ffn2_pre_g, ffn2_post_g, ffn2_w_gate, ffn2_w_up, ffn2_w_down, ple_w_gate, ple_w_proj, ple_post_g):
    depth = ffn1_pre_g.shape[0]
    bp, lp, d = x_prompt.shape
    bs, ls, _ = x_sample.shape
    assert lp % TOKEN_BLOCK == 0 and (bs * ls) % TOKEN_BLOCK == 0 and TOKEN_BLOCK % ls == 0
    assert CONV_K - 1 <= ls <= V7X_SUBLANES and (lp - 1) % CHUNK + 1 == CHUNK
    hp = x_prompt.reshape(bp * lp, d)
    hs = x_sample.reshape(bs * ls, d)
    conv_p, conv_s, v_p, v_s = [], [], [], []
    for i in range(depth):
        mw = _mix_weights(ls, mix_pre_g[i], mix_post_g[i], w_in[i], conv_w[i], v_norm_g[i], w_s[i], b_s[i],
                          out_g_a[i], out_g_b[i], w_out[i])
        state_rows = jnp.pad(state_conv[i].astype(F32), ((0, 0), (0, ls - (CONV_K - 1)), (0, 0)))
        state_rows = state_rows.reshape(bs * ls, CONV_DIM)
        ple = (p_prompt[i].reshape(bp * lp, -1), p_sample[i].reshape(bs * ls, -1),
               ple_w_gate[i], ple_w_proj[i], _row(ple_post_g[i]))

        hp, hs = _ffn(hp, hs, _row(ffn1_pre_g[i]), _row(ffn1_post_g[i]),
                      ffn1_w_gate[i], ffn1_w_up[i], ffn1_w_down[i], name="ffn1")
        hp, hs, cp, vp, cs, vs = _mix(hp, hs, state_rows, bp, lp, bs, ls, mw)
        hp, hs = _ffn(hp, hs, _row(ffn2_pre_g[i]), _row(ffn2_post_g[i]),
                      ffn2_w_gate[i], ffn2_w_up[i], ffn2_w_down[i], ple, name="ffn2_ple")

        conv_p.append(cp)
        conv_s.append(cs)
        v_p.append(vp.reshape(bp, CHUNK, HEADS_B, HEAD_DIM_B))
        v_s.append(vs.reshape(bs, ls, HEADS_B, HEAD_DIM_B))
    return (hp.reshape(bp, lp, d), hs.reshape(bs, ls, d),
            jnp.stack(conv_p), jnp.stack(conv_s), jnp.stack(v_p), jnp.stack(v_s))
```

```python
import functools

import jax
import jax.numpy as jnp
from jax import lax
from jax.experimental import pallas as pl
from jax.experimental.pallas import tpu as pltpu

CONV_K = 3
HEADS_B = 8
HEAD_DIM_B = 64
CHUNK_DIM = HEADS_B * HEAD_DIM_B
CONV_DIM = CHUNK_DIM
CHUNK = 128
EPS = 1e-6

V7X_SUBLANES = 8
V7X_MXU_COLS = 256
V7X_VMEM_LIMIT_BYTES = 60 * 1024 * 1024

TOKEN_BLOCK = 512
MIX_TOKEN_BLOCK = 512
WEIGHT_TILE = V7X_MXU_COLS
HEADS_PER_SLAB = V7X_MXU_COLS // HEAD_DIM_B
MIX_SEGMENTS = 5
FINISH_SLICES = 8
FINISH_FIRST_CHUNK = 1
MIX_FINISH_SLICES = 4

BF16 = jnp.bfloat16
F32 = jnp.float32


def _rms(x, g):
    return x * lax.rsqrt(jnp.mean(x * x, axis=-1, keepdims=True) + EPS) * g


def _dot(a, b):
    return jnp.dot(a.astype(BF16), b, preferred_element_type=F32)


def _sigmoid(x):
    return 0.5 + 0.5 * jnp.tanh(0.5 * x)


def _silu(x):
    half = 0.5 * x
    return half + half * jnp.tanh(half)


def _after(value, anchor):
    bits = lax.bitcast_convert_type(anchor, jnp.int32)
    zero = lax.shift_right_logical(lax.shift_right_logical(bits, 16), 16)
    return value + zero.astype(value.dtype)


def _store_piece(dst_ref, src_ref, step, n_pieces):
    rows = src_ref.shape[0]

    @pl.when(step < n_pieces)
    def _():
        dst_ref[pl.ds(pl.multiple_of(step * rows, rows), rows), :] = src_ref[...].astype(BF16)


def _ffn_kernel(*refs, n_prep, n_prompt, n_blocks, with_ple):
    if with_ple:
        (xc_ref, xv_ref, pp_ref, ps_ref, pre_ref, post_ref, wg_ref, wu_ref, wd_ref, wpg_ref, wpp_ref, pleg_ref,
         o_ref, os_ref, wgu_s, wd_s, xn_s, a_s, wpg_s, wpp_s, hb_s) = refs
    else:
        (xpc_ref, xsc_ref, xpv_ref, xsv_ref, pre_ref, post_ref, wg_ref, wu_ref, wd_ref,
         o_ref, wgu_s, wd_s, xn_s, a_s) = refs
    i = pl.program_id(0)
    j = i - n_prep
    n_sample = n_blocks - n_prompt
    cur_is_sample = j + 1 >= n_prompt
    prev_is_sample = jnp.logical_and(j >= 0, j < n_sample) if with_ple else j >= n_prompt

    tm = xn_s.shape[0]
    n_slices = FINISH_SLICES // 2 if with_ple else FINISH_SLICES
    slice_rows = tm // n_slices
    stages = [("residual", s) for s in range(n_slices)]
    if with_ple:
        stages += [("embed_dots", 0)] + [("embed_add", s) for s in range(n_slices)]
    assert FINISH_FIRST_CHUNK + len(stages) <= n_prep

    def ordered(r, rows, anchor):
        return r[rows] if anchor is None else _after(r[rows], anchor[rows, 0:1])

    def x_rows(prompt_ref, sample_ref, is_sample, rows, prompt_only):
        if prompt_only:
            return prompt_ref[rows, :]
        return jnp.where(is_sample, sample_ref[rows, :], prompt_ref[rows, :])

    def finish_stage(stage, s, st, anchor=None, prompt_only=False):
        rows = slice(s * slice_rows, (s + 1) * slice_rows)
        if stage == "residual":
            x = xv_ref[rows, :] if with_ple else x_rows(xpv_ref, xsv_ref, prev_is_sample, rows, prompt_only)
            h = x + 0.5 * (st["y"][rows] * ordered(st["r"], rows, anchor) * post_ref[...])
            o_ref[rows, :] = h
            if with_ple:
                hb_s[rows, :] = h.astype(BF16)
        elif stage == "embed_dots":
            p = jnp.where(prev_is_sample, ps_ref[...], pp_ref[...])
            gate = _sigmoid(jnp.dot(hb_s[...], wpg_s[...], preferred_element_type=F32))
            st["e"] = gate * _dot(p, wpp_s[...])
            st["r2"] = lax.rsqrt(jnp.mean(st["e"] * st["e"], axis=-1, keepdims=True) + EPS)
        else:
            o_ref[rows, :] = o_ref[rows, :] + st["e"][rows] * ordered(st["r2"], rows, anchor) * pleg_ref[...]

    def down_prev():
        y = jnp.dot(a_s[...], wd_s[...], preferred_element_type=F32)
        return dict(y=y, r=lax.rsqrt(jnp.mean(y * y, axis=-1, keepdims=True) + EPS))

    def finish_prev():
        st = down_prev()
        for stage, s in stages:
            finish_stage(stage, s, st)

    def gate_up_chunk(c):
        gu = jnp.dot(xn_s[...], wgu_s[c], preferred_element_type=F32)
        g, u = gu[:, 0:WEIGHT_TILE], gu[:, WEIGHT_TILE:]
        return g, (_silu(g) * u).astype(BF16)

    def finish_prev_and_gate_up_cur(prompt_only=False):
        st = down_prev()
        x = xc_ref[...] if with_ple else x_rows(xpc_ref, xsc_ref, cur_is_sample, slice(None), prompt_only)
        x = _after(x, st["y"][:, 0:1])
        xn_s[...] = _rms(x, pre_ref[...]).astype(BF16)
        for c in range(n_prep):
            g, a_s[:, c * WEIGHT_TILE:(c + 1) * WEIGHT_TILE] = gate_up_chunk(c)
            k = c - FINISH_FIRST_CHUNK
            if 0 <= k < len(stages):
                finish_stage(*stages[k], st, anchor=g, prompt_only=prompt_only)

    @pl.when(i == 0)
    def _():
        xn_s[...] = _rms(xc_ref[...] if with_ple else xpc_ref[...], pre_ref[...]).astype(BF16)

    @pl.when(i < n_prep)
    def _():
        wgu_s[i, :, 0:WEIGHT_TILE] = wg_ref[...].astype(BF16)
        wgu_s[i, :, WEIGHT_TILE:] = wu_ref[...].astype(BF16)
        _, a = gate_up_chunk(i)
        for c in range(n_prep):
            @pl.when(i == c)
            def _():
                a_s[:, c * WEIGHT_TILE:(c + 1) * WEIGHT_TILE] = a

    _store_piece(wd_s, wd_ref, i, n_prep)
    if with_ple:
        _store_piece(wpg_s, wpg_ref, i, wpg_s.shape[0] // wpg_ref.shape[0])
        _store_piece(wpp_s, wpp_ref, i, 1)

    if with_ple:
        @pl.when(jnp.logical_and(j >= 0, j < n_blocks - 1))
        def _():
            finish_prev_and_gate_up_cur()
    else:
        @pl.when(jnp.logical_and(j >= 0, j < n_prompt - 1))
        def _():
            finish_prev_and_gate_up_cur(prompt_only=True)

        @pl.when(jnp.logical_and(j >= n_prompt - 1, j < n_blocks - 1))
        def _():
            finish_prev_and_gate_up_cur()

    @pl.when(j == n_blocks - 1)
    def _():
        finish_prev()

    if with_ple:
        @pl.when(prev_is_sample)
        def _():
            os_ref[...] = o_ref[...]


def _head_rms(v, ones_slab, g):
    vv = (v * v).astype(BF16)
    ms = jnp.concatenate(
        [jnp.dot(vv[:, k * V7X_MXU_COLS:(k + 1) * V7X_MXU_COLS], ones_slab, preferred_element_type=F32)
         for k in range(CHUNK_DIM // V7X_MXU_COLS)], axis=1)
    return v * lax.rsqrt(ms + EPS) * g


def _mix_kernel(hc_ref, hv_ref, state_ref, pre_ref, post_ref, win_ref, wout_ref, convw_ref, vg_ref, ones_ref,
                ws_ref, bias_chunk_ref, wsrc_ref, ga_ref, gb_ref,
                o_ref, convp_ref, vp_ref, convs_ref, vs_ref,
                win_s, wout_s, n_s, yab_s, carry_ref,
                *, n_prep, n_prompt, n_blocks, blocks_per_seq, seq_len):
    tm = hc_ref.shape[0]
    i = pl.program_id(0)
    j = i - n_prep

    @pl.when(i < n_prep)
    def _():
        win_s[i] = win_ref[...].astype(BF16)

    _store_piece(wout_s, wout_ref, i, wout_s.shape[0] // wout_ref.shape[0])

    @pl.when(i == 0)
    def _():
        yab_s[...] = jnp.zeros_like(yab_s)
        carry_ref[...] = jnp.zeros_like(carry_ref)

    slice_rows = tm // MIX_FINISH_SLICES

    def out_proj_prev():
        y = jnp.dot(yab_s[...], wout_s[...], preferred_element_type=F32)
        return dict(y=y, r=lax.rsqrt(jnp.mean(y * y, axis=-1, keepdims=True) + EPS), done=0)

    def finish_slice(prev, anchor=None):
        rows = slice(prev["done"] * slice_rows, (prev["done"] + 1) * slice_rows)
        prev["done"] += 1
        r = prev["r"][rows] if anchor is None else _after(prev["r"][rows], anchor[rows, 0:1])
        o_ref[rows, :] = hv_ref[rows, :] + prev["y"][rows] * r * post_ref[...]

    def seg(k, prev):
        out = jnp.dot(n_s[...], win_s[k], preferred_element_type=F32)
        if prev["done"] < MIX_FINISH_SLICES:
            finish_slice(prev, anchor=out)
        return out

    def conv(z, z1, z2):
        return convw_ref[0:1, :] * z2 + convw_ref[1:2, :] * z1 + convw_ref[2:3, :] * z

    def group_norms(conv_out, mixed, prev):
        yab_s[:, 0:CONV_DIM] = _rms(seg(0, prev) * conv_out, ga_ref[...]).astype(BF16)
        yab_s[:, CONV_DIM:] = _rms(seg(3, prev) * mixed, gb_ref[...]).astype(BF16)

    def prompt_block():
        prev = out_proj_prev()
        n_s[...] = _rms(hc_ref[...], pre_ref[...]).astype(BF16)

        v_n = _head_rms(seg(4, prev), ones_ref[...], vg_ref[...])
        vp_ref[0] = v_n[tm - CHUNK:, :]
        t_idx = lax.broadcasted_iota(jnp.int32, (CHUNK, HEADS_PER_SLAB * CHUNK), 0)
        s_idx = lax.broadcasted_iota(jnp.int32, (CHUNK, HEADS_PER_SLAB * CHUNK), 1) % CHUNK
        causal = t_idx >= s_idx
        lane_head = lax.broadcasted_iota(jnp.int32, (CHUNK, V7X_MXU_COLS), 1) // HEAD_DIM_B
        n_slabs = CHUNK_DIM // V7X_MXU_COLS
        w_slab = [jnp.where(causal, jnp.concatenate([ws_ref[k * HEADS_PER_SLAB + hd] for hd in range(HEADS_PER_SLAB)],
                                                     axis=1), 0.0).astype(BF16) for k in range(n_slabs)]
        chunks = []
        for c in range(tm // CHUNK):
            slabs = []
            for k in range(n_slabs):
                cols = slice(k * V7X_MXU_COLS, (k + 1) * V7X_MXU_COLS)
                v_ck = v_n[c * CHUNK:(c + 1) * CHUNK, cols].astype(BF16)
                bd = jnp.concatenate(
                    [jnp.where(lane_head == hd, v_ck, jnp.zeros((), BF16)) for hd in range(HEADS_PER_SLAB)], axis=0)
                slabs.append(jnp.dot(w_slab[k], bd, preferred_element_type=F32))
            chunks.append(jnp.concatenate(slabs, axis=1) + bias_chunk_ref[...])
        mixed = jnp.concatenate(chunks, axis=0)

        z = seg(1, prev) * seg(2, prev)
        row = lax.broadcasted_iota(jnp.int32, z.shape, 0)
        carry = jnp.where(j % blocks_per_seq == 0, 0.0, carry_ref[...])
        prev1 = carry[1:2, :]
        prev2 = carry[0:1, :]
        z1 = jnp.where(row == 0, prev1, pltpu.roll(z, 1, axis=0))
        z2 = jnp.where(row == 0, prev2, jnp.where(row == 1, prev1, pltpu.roll(z, 2, axis=0)))
        tail = z[tm - (CONV_K - 1):, :]
        carry_ref[...] = tail
        convp_ref[0] = tail
        group_norms(conv(z, z1, z2), mixed, prev)

    def sample_block():
        prev = out_proj_prev()
        n_s[...] = _rms(hc_ref[...], pre_ref[...]).astype(BF16)

        n_seq = tm // seq_len
        per_seq = lambda a: a.reshape(n_seq, seq_len, a.shape[-1])
        every_row = lambda a: jnp.broadcast_to(a, (n_seq, seq_len, a.shape[-1]))
        v_n = _head_rms(seg(4, prev), ones_ref[...], vg_ref[...])
        vs_ref[...] = v_n
        v_seq = per_seq(v_n)
        t_row = lax.broadcasted_iota(jnp.int32, (seq_len, CHUNK_DIM), 0)
        mixed = every_row(bias_chunk_ref[0:seq_len, :][None])
        for s in range(seq_len):
            mixed = mixed + jnp.where(t_row >= s, wsrc_ref[s], 0.0)[None] * every_row(v_seq[:, s:s + 1, :])
        mixed = mixed.reshape(tm, CHUNK_DIM)

        z = seg(1, prev) * seg(2, prev)
        t = lax.broadcasted_iota(jnp.int32, z.shape, 0) % seq_len
        st = state_ref[...]
        prev1 = every_row(st[:, 1:2, :]).reshape(tm, CONV_DIM)
        prev2 = every_row(st[:, 0:1, :]).reshape(tm, CONV_DIM)
        z1 = jnp.where(t == 0, prev1, pltpu.roll(z, 1, axis=0))
        z2 = jnp.where(t == 0, prev2, jnp.where(t == 1, prev1, pltpu.roll(z, 2, axis=0)))
        convs_ref[...] = per_seq(z)[:, seq_len - (CONV_K - 1):, :]
        group_norms(conv(z, z1, z2), mixed, prev)

    @pl.when(jnp.logical_and(j >= 0, j < n_prompt))
    def _():
        prompt_block()

    @pl.when(jnp.logical_and(j >= n_prompt, j < n_blocks))
    def _():
        sample_block()

    @pl.when(j == n_blocks)
    def _():
        prev = out_proj_prev()
        for _ in range(MIX_FINISH_SLICES):
            finish_slice(prev)


def _resident(a):
    return pl.BlockSpec(a.shape, lambda i: (0,) * a.ndim, pipeline_mode=pl.Buffered(1))


def _clamped(lo, n):
    return lambda i: jnp.minimum(jnp.maximum(i - lo, 0), n - 1)


def _stream(tm, d, lo, n):
    blk = _clamped(lo, n)
    return pl.BlockSpec((tm, d), lambda i: (blk(i), 0))


def _row_pieces(w, rows):
    blk = _clamped(0, w.shape[0] // rows)
    return pl.BlockSpec((rows, w.shape[1]), lambda i: (blk(i), 0))


def _col_pieces(w, cols):
    blk = _clamped(0, w.shape[1] // cols)
    return pl.BlockSpec((w.shape[0], cols), lambda i: (0, blk(i)))


def _params():
    return pltpu.CompilerParams(dimension_semantics=("arbitrary",), vmem_limit_bytes=V7X_VMEM_LIMIT_BYTES)


def _ffn(x, n_prompt, n_sample, pre_g, post_g, wg, wu, wd, ple=None, *, name):
    tm = TOKEN_BLOCK
    d, d_ff = wg.shape
    n_prep = d_ff // WEIGHT_TILE
    n_blocks = n_prompt + n_sample
    nxt, prv = n_prep - 1, n_prep
    weights = [_resident(pre_g), _resident(post_g),
               _col_pieces(wg, WEIGHT_TILE), _col_pieces(wu, WEIGHT_TILE), _row_pieces(wd, WEIGHT_TILE)]
    scratch = [pltpu.VMEM((n_prep, d, 2 * WEIGHT_TILE), BF16),
               pltpu.VMEM((d_ff, d), BF16), pltpu.VMEM((tm, d), BF16), pltpu.VMEM((tm, d_ff), BF16)]
    if ple is None:
        xp, xs = x
        in_specs = [_stream(tm, d, nxt, n_prompt), _stream(tm, d, nxt + n_prompt, n_sample),
                    _stream(tm, d, prv, n_prompt), _stream(tm, d, prv + n_prompt, n_sample)] + weights
        args = [xp, xs, xp, xs, pre_g, post_g, wg, wu, wd]
        out_specs = _stream(tm, d, prv, n_blocks)
        out_shape = jax.ShapeDtypeStruct((n_blocks * tm, d), F32)
    else:
        pp, ps, wpg, wpp, ple_g = ple
        assert wpg.shape[0] // (WEIGHT_TILE // 2) <= n_prep
        d_p = pp.shape[1]

        def sample_first(lo):
            k = _clamped(lo, n_blocks)
            return pl.BlockSpec((tm, d), lambda i: (jnp.where(k(i) < n_sample, k(i) + n_prompt, k(i) - n_sample), 0))

        in_specs = [sample_first(nxt), sample_first(prv),
                    _stream(tm, d_p, prv + n_sample, n_prompt), _stream(tm, d_p, prv, n_sample)] + weights
        in_specs += [_row_pieces(wpg, WEIGHT_TILE // 2), _row_pieces(wpp, wpp.shape[0]), _resident(ple_g)]
        args = [x, x, pp, ps, pre_g, post_g, wg, wu, wd, wpg, wpp, ple_g]
        out_specs = [_stream(tm, d, prv + n_sample, n_prompt), _stream(tm, d, prv, n_sample)]
        out_shape = [jax.ShapeDtypeStruct((n_prompt * tm, d), F32), jax.ShapeDtypeStruct((n_sample * tm, d), F32)]
        scratch += [pltpu.VMEM(wpg.shape, BF16), pltpu.VMEM(wpp.shape, BF16), pltpu.VMEM((tm, d), BF16)]
    return pl.pallas_call(
        functools.partial(_ffn_kernel, n_prep=n_prep, n_prompt=n_prompt, n_blocks=n_blocks,
                          with_ple=ple is not None),
        grid=(n_prep + n_blocks,),
        in_specs=in_specs,
        out_specs=out_specs,
        out_shape=out_shape,
        scratch_shapes=scratch,
        compiler_params=_params(),
        name=name,
    )(*args)


def _mix(h, state, batch_p, seq_p, batch_s, seq_s, mw):
    tm = MIX_TOKEN_BLOCK
    d = h.shape[1]
    n_prep = MIX_SEGMENTS
    assert seq_p % tm == 0 and (batch_s * seq_s) % tm == 0 and tm % seq_s == 0
    n_prompt, n_sample = batch_p * seq_p // tm, batch_s * seq_s // tm
    n_blocks = n_prompt + n_sample
    blocks_per_seq = seq_p // tm
    seqs_per_block = tm // seq_s
    assert d // WEIGHT_TILE <= n_prep
    small = [mw["conv_w"], mw["v_g"], mw["ones_bd"], mw["w_s"], mw["bias_chunk"], mw["w_src"], mw["g_a"], mw["g_b"]]
    lo_s = n_prep + n_prompt
    prompt_seq = _clamped(0, batch_p)
    per_prompt_seq = lambda i: (prompt_seq((i - n_prep) // blocks_per_seq), 0, 0)
    sample_blk = _clamped(lo_s, n_sample)
    per_sample_blk = pl.BlockSpec((seqs_per_block, CONV_K - 1, CONV_DIM), lambda i: (sample_blk(i), 0, 0))
    return pl.pallas_call(
        functools.partial(_mix_kernel, n_prep=n_prep, n_prompt=n_prompt, n_blocks=n_blocks,
                          blocks_per_seq=blocks_per_seq, seq_len=seq_s),
        grid=(n_prep + n_blocks + 1,),
        in_specs=[_stream(tm, d, n_prep, n_blocks), _stream(tm, d, n_prep + 1, n_blocks),
                  per_sample_blk, _resident(mw["pre"]), _resident(mw["post"]),
                  _col_pieces(mw["w_in"], CONV_DIM), _row_pieces(mw["w_out"], WEIGHT_TILE)]
                 + [_resident(a) for a in small],
        out_specs=[_stream(tm, d, n_prep + 1, n_blocks),
                   pl.BlockSpec((1, CONV_K - 1, CONV_DIM), per_prompt_seq),
                   pl.BlockSpec((1, CHUNK, CHUNK_DIM), per_prompt_seq),
                   per_sample_blk,
                   _stream(tm, CHUNK_DIM, lo_s, n_sample)],
        out_shape=[jax.ShapeDtypeStruct(h.shape, F32),
                   jax.ShapeDtypeStruct((batch_p, CONV_K - 1, CONV_DIM), F32),
                   jax.ShapeDtypeStruct((batch_p, CHUNK, CHUNK_DIM), F32),
                   jax.ShapeDtypeStruct((batch_s, CONV_K - 1, CONV_DIM), F32),
                   jax.ShapeDtypeStruct((n_sample * tm, CHUNK_DIM), F32)],
        scratch_shapes=[pltpu.VMEM((MIX_SEGMENTS, d, CONV_DIM), BF16), pltpu.VMEM(mw["w_out"].shape, BF16),
                        pltpu.VMEM((tm, d), BF16), pltpu.VMEM((tm, CONV_DIM + CHUNK_DIM), BF16),
                        pltpu.VMEM((CONV_K - 1, CONV_DIM), F32)],
        compiler_params=_params(),
        name="mix",
    )(h, h, state, mw["pre"], mw["post"], mw["w_in"], mw["w_out"], *small)


def _row(g):
    return g.reshape(1, -1).astype(F32)


def _mix_weights(seq_s, mix_pre_g, mix_post_g, w_in, conv_w, v_norm_g, w_s, b_s, out_g_a, out_g_b, w_out):
    head_of_col = jnp.arange(V7X_MXU_COLS) // HEAD_DIM_B
    ones_bd = (head_of_col[:, None] == head_of_col[None, :]).astype(BF16) / HEAD_DIM_B
    w_src = jnp.repeat(w_s[:, :seq_s, :seq_s].transpose(2, 1, 0), HEAD_DIM_B, axis=-1)
    bias_chunk = jnp.repeat(b_s.T, HEAD_DIM_B, axis=-1)
    return dict(pre=_row(mix_pre_g), post=_row(mix_post_g), w_in=w_in, w_out=w_out,
                conv_w=conv_w.astype(F32), v_g=_row(v_norm_g), ones_bd=ones_bd, w_s=w_s.astype(F32),
                w_src=w_src.astype(F32), bias_chunk=bias_chunk.astype(F32), g_a=_row(out_g_a), g_b=_row(out_g_b))


def kernel(x_prompt, x_sample, p_prompt, p_sample, state_conv, ffn1_pre_g, ffn1_post_g, ffn1_w_gate, ffn1_w_up, ffn1_w_down, mix_pre_g, mix_post_g, w_in, conv_w, v_norm_g, w_s, b_s, out_g_a, out_g_b, w_out, ffn2_pre_g, ffn2_post_g, ffn2_w_gate, ffn2_w_up, ffn2_w_down, ple_w_gate, ple_w_proj, ple_post_g):
    depth = ffn1_pre_g.shape[0]
    bp, lp, d = x_prompt.shape
    bs, ls, _ = x_sample.shape
    tm = TOKEN_BLOCK
    assert lp % tm == 0 and (bs * ls) % tm == 0 and tm % ls == 0
    assert CONV_K - 1 <= ls <= V7X_SUBLANES and (lp - 1) % CHUNK + 1 == CHUNK
    n_prompt, n_sample = bp * lp // tm, bs * ls // tm
    x = (x_prompt.reshape(bp * lp, d), x_sample.reshape(bs * ls, d))
    conv_p, conv_s, v_p, v_s = [], [], [], []
    for i in range(depth):
        mw = _mix_weights(ls, mix_pre_g[i], mix_post_g[i], w_in[i], conv_w[i], v_norm_g[i], w_s[i], b_s[i],
                          out_g_a[i], out_g_b[i], w_out[i])
        ple = (p_prompt[i].reshape(bp * lp, -1), p_sample[i].reshape(bs * ls, -1),
               ple_w_gate[i], ple_w_proj[i], _row(ple_post_g[i]))

        h = _ffn(x, n_prompt, n_sample, _row(ffn1_pre_g[i]), _row(ffn1_post_g[i]),
                 ffn1_w_gate[i], ffn1_w_up[i], ffn1_w_down[i], name="ffn1")
        h, cp, vp, cs, vs = _mix(h, state_conv[i].astype(F32), bp, lp, bs, ls, mw)
        x = _ffn(h, n_prompt, n_sample, _row(ffn2_pre_g[i]), _row(ffn2_post_g[i]),
                 ffn2_w_gate[i], ffn2_w_up[i], ffn2_w_down[i], ple, name="ffn2_ple")

        conv_p.append(cp)
        conv_s.append(cs)
        v_p.append(vp.reshape(bp, CHUNK, HEADS_B, HEAD_DIM_B))
        v_s.append(vs.reshape(bs, ls, HEADS_B, HEAD_DIM_B))
    layers = lambda parts: parts[0][None] if depth == 1 else jnp.stack(parts)
    return (x[0].reshape(bp, lp, d), x[1].reshape(bs, ls, d),
            layers(conv_p), layers(conv_s), layers(v_p), layers(v_s))
```

```python
import functools

import jax
import jax.numpy as jnp
from jax import lax
from jax.experimental import pallas as pl
from jax.experimental.pallas import tpu as pltpu

CONV_K = 3
HEADS_B = 8
HEAD_DIM_B = 64
CHUNK_DIM = HEADS_B * HEAD_DIM_B
CONV_DIM = CHUNK_DIM
CHUNK = 128
EPS = 1e-6

V7X_SUBLANES = 8
V7X_LANES = 128
V7X_MXU_COLS = 256
V7X_VMEM_LIMIT_BYTES = 60 * 1024 * 1024

TOKEN_BLOCK = 512
MIX_TOKEN_BLOCK = 512
WEIGHT_TILE = V7X_MXU_COLS
HEADS_PER_SLAB = V7X_MXU_COLS // HEAD_DIM_B
MIX_SEGMENTS = 5
FINISH_SLICES = 8
FINISH_FIRST_CHUNK = 1
MIX_FINISH_SLICES = 4

BF16 = jnp.bfloat16
F32 = jnp.float32


def _rms(x, g):
    return x * lax.rsqrt(jnp.mean(x * x, axis=-1, keepdims=True) + EPS) * g


def _dot(a, b):
    return jnp.dot(a.astype(BF16), b, preferred_element_type=F32)


def _sigmoid(x):
    return 0.5 + 0.5 * jnp.tanh(0.5 * x)


def _silu(x):
    half = 0.5 * x
    return half + half * jnp.tanh(half)


def _after(value, anchor):
    bits = lax.bitcast_convert_type(anchor, jnp.int32)
    zero = lax.shift_right_logical(lax.shift_right_logical(bits, 16), 16)
    return value + zero.astype(value.dtype)


def _store_piece(dst_ref, src_ref, step, n_pieces):
    rows = src_ref.shape[0]

    @pl.when(step < n_pieces)
    def _():
        dst_ref[pl.ds(pl.multiple_of(step * rows, rows), rows), :] = src_ref[...].astype(BF16)


def _ffn_kernel(*refs, n_prep, n_prompt, n_blocks, with_ple):
    if with_ple:
        (xc_ref, xv_ref, pp_ref, ps_ref, pre_ref, post_ref, wg_ref, wu_ref, wd_ref, wpg_ref, wpp_ref, pleg_ref,
         o_ref, os_ref, wgu_s, wd_s, xn_s, a_s, wpg_s, wpp_s, hb_s) = refs
    else:
        (xpc_ref, xsc_ref, xpv_ref, xsv_ref, pre_ref, post_ref, wg_ref, wu_ref, wd_ref,
         o_ref, wgu_s, wd_s, xn_s, a_s) = refs
    i = pl.program_id(0)
    j = i - n_prep
    n_sample = n_blocks - n_prompt
    cur_is_sample = j + 1 >= n_prompt
    prev_is_sample = jnp.logical_and(j >= 0, j < n_sample) if with_ple else j >= n_prompt

    tm = xn_s.shape[0]
    n_slices = FINISH_SLICES // 2 if with_ple else FINISH_SLICES
    slice_rows = tm // n_slices
    stages = [("residual", s) for s in range(n_slices)]
    if with_ple:
        stages += [("embed_dots", 0)] + [("embed_add", s) for s in range(n_slices)]
    assert FINISH_FIRST_CHUNK + len(stages) <= n_prep

    def ordered(r, rows, anchor):
        return r[rows] if anchor is None else _after(r[rows], anchor[rows, 0:1])

    def x_rows(prompt_ref, sample_ref, is_sample, rows, prompt_only):
        if prompt_only:
            return prompt_ref[rows, :]
        return jnp.where(is_sample, sample_ref[rows, :], prompt_ref[rows, :])

    def finish_stage(stage, s, st, anchor=None, prompt_only=False):
        rows = slice(s * slice_rows, (s + 1) * slice_rows)
        if stage == "residual":
            x = xv_ref[rows, :] if with_ple else x_rows(xpv_ref, xsv_ref, prev_is_sample, rows, prompt_only)
            h = x + 0.5 * (st["y"][rows] * ordered(st["r"], rows, anchor) * post_ref[...])
            o_ref[rows, :] = h
            if with_ple:
                hb_s[rows, :] = h.astype(BF16)
        elif stage == "embed_dots":
            p = jnp.where(prev_is_sample, ps_ref[...], pp_ref[...])
            gate = _sigmoid(jnp.dot(hb_s[...], wpg_s[...], preferred_element_type=F32))
            st["e"] = gate * _dot(p, wpp_s[...])
            st["r2"] = lax.rsqrt(jnp.mean(st["e"] * st["e"], axis=-1, keepdims=True) + EPS)
        else:
            o_ref[rows, :] = o_ref[rows, :] + st["e"][rows] * ordered(st["r2"], rows, anchor) * pleg_ref[...]

    def down_prev():
        y = jnp.dot(a_s[...], wd_s[...], preferred_element_type=F32)
        return dict(y=y, r=lax.rsqrt(jnp.mean(y * y, axis=-1, keepdims=True) + EPS))

    def finish_prev():
        st = down_prev()
        for stage, s in stages:
            finish_stage(stage, s, st)

    def gate_up_chunk(c):
        gu = jnp.dot(xn_s[...], wgu_s[c], preferred_element_type=F32)
        g, u = gu[:, 0:WEIGHT_TILE], gu[:, WEIGHT_TILE:]
        return g, (_silu(g) * u).astype(BF16)

    def finish_prev_and_gate_up_cur(prompt_only=False):
        st = down_prev()
        x = xc_ref[...] if with_ple else x_rows(xpc_ref, xsc_ref, cur_is_sample, slice(None), prompt_only)
        x = _after(x, st["y"][:, 0:1])
        xn_s[...] = _rms(x, pre_ref[...]).astype(BF16)
        for c in range(n_prep):
            g, a_s[:, c * WEIGHT_TILE:(c + 1) * WEIGHT_TILE] = gate_up_chunk(c)
            k = c - FINISH_FIRST_CHUNK
            if 0 <= k < len(stages):
                finish_stage(*stages[k], st, anchor=g, prompt_only=prompt_only)

    @pl.when(i == 0)
    def _():
        xn_s[...] = _rms(xc_ref[...] if with_ple else xpc_ref[...], pre_ref[...]).astype(BF16)

    @pl.when(i < n_prep)
    def _():
        wgu_s[i, :, 0:WEIGHT_TILE] = wg_ref[...].astype(BF16)
        wgu_s[i, :, WEIGHT_TILE:] = wu_ref[...].astype(BF16)
        _, a = gate_up_chunk(i)
        for c in range(n_prep):
            @pl.when(i == c)
            def _():
                a_s[:, c * WEIGHT_TILE:(c + 1) * WEIGHT_TILE] = a

    _store_piece(wd_s, wd_ref, i, n_prep)
    if with_ple:
        _store_piece(wpg_s, wpg_ref, i, wpg_s.shape[0] // wpg_ref.shape[0])
        _store_piece(wpp_s, wpp_ref, i, 1)

    if with_ple:
        @pl.when(jnp.logical_and(j >= 0, j < n_blocks - 1))
        def _():
            finish_prev_and_gate_up_cur()
    else:
        @pl.when(jnp.logical_and(j >= 0, j < n_prompt - 1))
        def _():
            finish_prev_and_gate_up_cur(prompt_only=True)

        @pl.when(jnp.logical_and(j >= n_prompt - 1, j < n_blocks - 1))
        def _():
            finish_prev_and_gate_up_cur()

    @pl.when(j == n_blocks - 1)
    def _():
        finish_prev()

    if with_ple:
        @pl.when(prev_is_sample)
        def _():
            os_ref[...] = o_ref[...]


def _head_rms(v, ones_slab, g):
    vv = (v * v).astype(BF16)
    ms = jnp.concatenate(
        [jnp.dot(vv[:, k * V7X_MXU_COLS:(k + 1) * V7X_MXU_COLS], ones_slab, preferred_element_type=F32)
         for k in range(CHUNK_DIM // V7X_MXU_COLS)], axis=1)
    return v * lax.rsqrt(ms + EPS) * g


def _mix_kernel(hc_ref, hv_ref, state_ref, pre_ref, post_ref, win_ref, wout_ref, convw_ref, vg_ref, ones_ref,
                ws_ref, bias_chunk_ref, wsrc_ref, ga_ref, gb_ref,
                o_ref, convp_ref, vp_ref, convs_ref, vs_ref,
                win_s, wout_s, n_s, yab_s, carry_ref, vn_s, vt_s,
                *, n_prep, n_prompt, n_blocks, blocks_per_seq, seq_len):
    tm = hc_ref.shape[0]
    i = pl.program_id(0)
    j = i - n_prep

    @pl.when(i < n_prep)
    def _():
        win_s[i] = win_ref[...].astype(BF16)

    _store_piece(wout_s, wout_ref, i, wout_s.shape[0] // wout_ref.shape[0])

    @pl.when(i == 0)
    def _():
        yab_s[...] = jnp.zeros_like(yab_s)
        carry_ref[...] = jnp.zeros_like(carry_ref)

    slice_rows = tm // MIX_FINISH_SLICES

    def out_proj_prev():
        y = jnp.dot(yab_s[...], wout_s[...], preferred_element_type=F32)
        return dict(y=y, r=lax.rsqrt(jnp.mean(y * y, axis=-1, keepdims=True) + EPS), done=0)

    def finish_slice(prev, anchor=None):
        rows = slice(prev["done"] * slice_rows, (prev["done"] + 1) * slice_rows)
        prev["done"] += 1
        r = prev["r"][rows] if anchor is None else _after(prev["r"][rows], anchor[rows, 0:1])
        o_ref[rows, :] = hv_ref[rows, :] + prev["y"][rows] * r * post_ref[...]

    def seg(k, prev):
        out = jnp.dot(n_s[...], win_s[k], preferred_element_type=F32)
        if prev["done"] < MIX_FINISH_SLICES:
            finish_slice(prev, anchor=out)
        return out

    def conv(z, z1, z2):
        return convw_ref[0:1, :] * z2 + convw_ref[1:2, :] * z1 + convw_ref[2:3, :] * z

    def group_norms(conv_out, mixed, prev):
        yab_s[:, 0:CONV_DIM] = _rms(seg(0, prev) * conv_out, ga_ref[...]).astype(BF16)
        yab_s[:, CONV_DIM:] = _rms(seg(3, prev) * mixed, gb_ref[...]).astype(BF16)

    def prompt_block():
        prev = out_proj_prev()
        n_s[...] = _rms(hc_ref[...], pre_ref[...]).astype(BF16)

        v_n = _head_rms(seg(4, prev), ones_ref[...], vg_ref[...])
        vp_ref[0] = v_n[tm - CHUNK:, :].T
        t_idx = lax.broadcasted_iota(jnp.int32, (CHUNK, HEADS_PER_SLAB * CHUNK), 0)
        s_idx = lax.broadcasted_iota(jnp.int32, (CHUNK, HEADS_PER_SLAB * CHUNK), 1) % CHUNK
        causal = t_idx >= s_idx
        lane_head = lax.broadcasted_iota(jnp.int32, (CHUNK, V7X_MXU_COLS), 1) // HEAD_DIM_B
        n_slabs = CHUNK_DIM // V7X_MXU_COLS
        w_slab = [jnp.where(causal, jnp.concatenate([ws_ref[k * HEADS_PER_SLAB + hd] for hd in range(HEADS_PER_SLAB)],
                                                     axis=1), 0.0).astype(BF16) for k in range(n_slabs)]
        chunks = []
        for c in range(tm // CHUNK):
            slabs = []
            for k in range(n_slabs):
                cols = slice(k * V7X_MXU_COLS, (k + 1) * V7X_MXU_COLS)
                v_ck = v_n[c * CHUNK:(c + 1) * CHUNK, cols].astype(BF16)
                bd = jnp.concatenate(
                    [jnp.where(lane_head == hd, v_ck, jnp.zeros((), BF16)) for hd in range(HEADS_PER_SLAB)], axis=0)
                slabs.append(jnp.dot(w_slab[k], bd, preferred_element_type=F32))
            chunks.append(jnp.concatenate(slabs, axis=1) + bias_chunk_ref[...])
        mixed = jnp.concatenate(chunks, axis=0)

        z = seg(1, prev) * seg(2, prev)
        row = lax.broadcasted_iota(jnp.int32, z.shape, 0)
        carry = jnp.where(j % blocks_per_seq == 0, 0.0, carry_ref[...])
        prev1 = carry[1:2, :]
        prev2 = carry[0:1, :]
        z1 = jnp.where(row == 0, prev1, pltpu.roll(z, 1, axis=0))
        z2 = jnp.where(row == 0, prev2, jnp.where(row == 1, prev1, pltpu.roll(z, 2, axis=0)))
        tail = z[tm - (CONV_K - 1):, :]
        carry_ref[...] = tail
        convp_ref[0] = tail
        group_norms(conv(z, z1, z2), mixed, prev)

    def sample_block():
        prev = out_proj_prev()
        n_s[...] = _rms(hc_ref[...], pre_ref[...]).astype(BF16)

        n_seq = tm // seq_len
        per_seq = lambda a: a.reshape(n_seq, seq_len, a.shape[-1])
        every_row = lambda a: jnp.broadcast_to(a, (n_seq, seq_len, a.shape[-1]))
        v_n = _head_rms(seg(4, prev), ones_ref[...], vg_ref[...])
        first_seq = pl.multiple_of((j - n_prompt) * n_seq, n_seq)
        for c in range(CHUNK_DIM // V7X_LANES):
            lanes = slice(c * V7X_LANES, (c + 1) * V7X_LANES)
            vn_s[c] = v_n[:, lanes]
            for t in range(seq_len):
                vt_s[t, pl.ds(first_seq, n_seq), lanes] = vn_s[c, pl.ds(t, n_seq, stride=seq_len), :]

        @pl.when(j == n_blocks - 1)
        def _():
            for t in range(seq_len):
                vs_ref[t] = vt_s[t].T
        v_seq = per_seq(v_n)
        t_row = lax.broadcasted_iota(jnp.int32, (seq_len, CHUNK_DIM), 0)
        mixed = every_row(bias_chunk_ref[0:seq_len, :][None])
        for s in range(seq_len):
            mixed = mixed + jnp.where(t_row >= s, wsrc_ref[s], 0.0)[None] * every_row(v_seq[:, s:s + 1, :])
        mixed = mixed.reshape(tm, CHUNK_DIM)

        z = seg(1, prev) * seg(2, prev)
        t = lax.broadcasted_iota(jnp.int32, z.shape, 0) % seq_len
        st = state_ref[...]
        prev1 = every_row(st[:, 1:2, :]).reshape(tm, CONV_DIM)
        prev2 = every_row(st[:, 0:1, :]).reshape(tm, CONV_DIM)
        z1 = jnp.where(t == 0, prev1, pltpu.roll(z, 1, axis=0))
        z2 = jnp.where(t == 0, prev2, jnp.where(t == 1, prev1, pltpu.roll(z, 2, axis=0)))
        convs_ref[...] = per_seq(z)[:, seq_len - (CONV_K - 1):, :]
        group_norms(conv(z, z1, z2), mixed, prev)

    @pl.when(jnp.logical_and(j >= 0, j < n_prompt))
    def _():
        prompt_block()

    @pl.when(jnp.logical_and(j >= n_prompt, j < n_blocks))
    def _():
        sample_block()

    @pl.when(j == n_blocks)
    def _():
        prev = out_proj_prev()
        for _ in range(MIX_FINISH_SLICES):
            finish_slice(prev)


def _resident(a):
    return pl.BlockSpec(a.shape, lambda i: (0,) * a.ndim, pipeline_mode=pl.Buffered(1))


def _clamped(lo, n):
    return lambda i: jnp.minimum(jnp.maximum(i - lo, 0), n - 1)


def _stream(tm, d, lo, n):
    blk = _clamped(lo, n)
    return pl.BlockSpec((tm, d), lambda i: (blk(i), 0))


def _row_pieces(w, rows):
    blk = _clamped(0, w.shape[0] // rows)
    return pl.BlockSpec((rows, w.shape[1]), lambda i: (blk(i), 0))


def _col_pieces(w, cols):
    blk = _clamped(0, w.shape[1] // cols)
    return pl.BlockSpec((w.shape[0], cols), lambda i: (0, blk(i)))


def _params():
    return pltpu.CompilerParams(dimension_semantics=("arbitrary",), vmem_limit_bytes=V7X_VMEM_LIMIT_BYTES)


def _ffn(x, n_prompt, n_sample, pre_g, post_g, wg, wu, wd, ple=None, *, name):
    tm = TOKEN_BLOCK
    d, d_ff = wg.shape
    n_prep = d_ff // WEIGHT_TILE
    n_blocks = n_prompt + n_sample
    nxt, prv = n_prep - 1, n_prep
    weights = [_resident(pre_g), _resident(post_g),
               _col_pieces(wg, WEIGHT_TILE), _col_pieces(wu, WEIGHT_TILE), _row_pieces(wd, WEIGHT_TILE)]
    scratch = [pltpu.VMEM((n_prep, d, 2 * WEIGHT_TILE), BF16),
               pltpu.VMEM((d_ff, d), BF16), pltpu.VMEM((tm, d), BF16), pltpu.VMEM((tm, d_ff), BF16)]
    if ple is None:
        xp, xs = x
        in_specs = [_stream(tm, d, nxt, n_prompt), _stream(tm, d, nxt + n_prompt, n_sample),
                    _stream(tm, d, prv, n_prompt), _stream(tm, d, prv + n_prompt, n_sample)] + weights
        args = [xp, xs, xp, xs, pre_g, post_g, wg, wu, wd]
        out_specs = _stream(tm, d, prv, n_blocks)
        out_shape = jax.ShapeDtypeStruct((n_blocks * tm, d), F32)
    else:
        pp, ps, wpg, wpp, ple_g = ple
        assert wpg.shape[0] // (WEIGHT_TILE // 2) <= n_prep
        d_p = pp.shape[1]

        def sample_first(lo):
            k = _clamped(lo, n_blocks)
            return pl.BlockSpec((tm, d), lambda i: (jnp.where(k(i) < n_sample, k(i) + n_prompt, k(i) - n_sample), 0))

        in_specs = [sample_first(nxt), sample_first(prv),
                    _stream(tm, d_p, prv + n_sample, n_prompt), _stream(tm, d_p, prv, n_sample)] + weights
        in_specs += [_row_pieces(wpg, WEIGHT_TILE // 2), _row_pieces(wpp, wpp.shape[0]), _resident(ple_g)]
        args = [x, x, pp, ps, pre_g, post_g, wg, wu, wd, wpg, wpp, ple_g]
        out_specs = [_stream(tm, d, prv + n_sample, n_prompt), _stream(tm, d, prv, n_sample)]
        out_shape = [jax.ShapeDtypeStruct((n_prompt * tm, d), F32), jax.ShapeDtypeStruct((n_sample * tm, d), F32)]
        scratch += [pltpu.VMEM(wpg.shape, BF16), pltpu.VMEM(wpp.shape, BF16), pltpu.VMEM((tm, d), BF16)]
    return pl.pallas_call(
        functools.partial(_ffn_kernel, n_prep=n_prep, n_prompt=n_prompt, n_blocks=n_blocks,
                          with_ple=ple is not None),
        grid=(n_prep + n_blocks,),
        in_specs=in_specs,
        out_specs=out_specs,
        out_shape=out_shape,
        scratch_shapes=scratch,
        compiler_params=_params(),
        name=name,
    )(*args)


def _mix(h, state, batch_p, seq_p, batch_s, seq_s, mw):
    tm = MIX_TOKEN_BLOCK
    d = h.shape[1]
    n_prep = MIX_SEGMENTS
    assert seq_p % tm == 0 and (batch_s * seq_s) % tm == 0 and tm % seq_s == 0
    n_prompt, n_sample = batch_p * seq_p // tm, batch_s * seq_s // tm
    n_blocks = n_prompt + n_sample
    blocks_per_seq = seq_p // tm
    seqs_per_block = tm // seq_s
    assert d // WEIGHT_TILE <= n_prep
    small = [mw["conv_w"], mw["v_g"], mw["ones_bd"], mw["w_s"], mw["bias_chunk"], mw["w_src"], mw["g_a"], mw["g_b"]]
    lo_s = n_prep + n_prompt
    prompt_seq = _clamped(0, batch_p)
    per_prompt_seq = lambda i: (prompt_seq((i - n_prep) // blocks_per_seq), 0, 0)
    sample_blk = _clamped(lo_s, n_sample)
    per_sample_blk = pl.BlockSpec((seqs_per_block, CONV_K - 1, CONV_DIM), lambda i: (sample_blk(i), 0, 0))
    return pl.pallas_call(
        functools.partial(_mix_kernel, n_prep=n_prep, n_prompt=n_prompt, n_blocks=n_blocks,
                          blocks_per_seq=blocks_per_seq, seq_len=seq_s),
        grid=(n_prep + n_blocks + 1,),
        in_specs=[_stream(tm, d, n_prep, n_blocks), _stream(tm, d, n_prep + 1, n_blocks),
                  per_sample_blk, _resident(mw["pre"]), _resident(mw["post"]),
                  _col_pieces(mw["w_in"], CONV_DIM), _row_pieces(mw["w_out"], WEIGHT_TILE)]
                 + [_resident(a) for a in small],
        out_specs=[_stream(tm, d, n_prep + 1, n_blocks),
                   pl.BlockSpec((1, CONV_K - 1, CONV_DIM), per_prompt_seq),
                   pl.BlockSpec((1, CHUNK_DIM, CHUNK), per_prompt_seq),
                   per_sample_blk,
                   pl.BlockSpec((seq_s, CHUNK_DIM, batch_s), lambda i: (0, 0, 0))],
        out_shape=[jax.ShapeDtypeStruct(h.shape, F32),
                   jax.ShapeDtypeStruct((batch_p, CONV_K - 1, CONV_DIM), F32),
                   jax.ShapeDtypeStruct((batch_p, CHUNK_DIM, CHUNK), F32),
                   jax.ShapeDtypeStruct((batch_s, CONV_K - 1, CONV_DIM), F32),
                   jax.ShapeDtypeStruct((seq_s, CHUNK_DIM, batch_s), F32)],
        scratch_shapes=[pltpu.VMEM((MIX_SEGMENTS, d, CONV_DIM), BF16), pltpu.VMEM(mw["w_out"].shape, BF16),
                        pltpu.VMEM((tm, d), BF16), pltpu.VMEM((tm, CONV_DIM + CHUNK_DIM), BF16),
                        pltpu.VMEM((CONV_K - 1, CONV_DIM), F32),
                        pltpu.VMEM((CHUNK_DIM // V7X_LANES, tm, V7X_LANES), F32),
                        pltpu.VMEM((seq_s, batch_s, CHUNK_DIM), F32)],
        compiler_params=_params(),
        name="mix",
    )(h, h, state, mw["pre"], mw["post"], mw["w_in"], mw["w_out"], *small)


def _row(g):
    return g.reshape(1, -1).astype(F32)


def _mix_weights(seq_s, mix_pre_g, mix_post_g, w_in, conv_w, v_norm_g, w_s, b_s, out_g_a, out_g_b, w_out):
    head_of_col = jnp.arange(V7X_MXU_COLS) // HEAD_DIM_B
    ones_bd = (head_of_col[:, None] == head_of_col[None, :]).astype(BF16) / HEAD_DIM_B
    w_src = jnp.repeat(w_s[:, :seq_s, :seq_s].transpose(2, 1, 0), HEAD_DIM_B, axis=-1)
    bias_chunk = jnp.repeat(b_s.T, HEAD_DIM_B, axis=-1)
    return dict(pre=_row(mix_pre_g), post=_row(mix_post_g), w_in=w_in, w_out=w_out,
                conv_w=conv_w.astype(F32), v_g=_row(v_norm_g), ones_bd=ones_bd, w_s=w_s.astype(F32),
                w_src=w_src.astype(F32), bias_chunk=bias_chunk.astype(F32), g_a=_row(out_g_a), g_b=_row(out_g_b))


def kernel(x_prompt, x_sample, p_prompt, p_sample, state_conv, ffn1_pre_g, ffn1_post_g, ffn1_w_gate, ffn1_w_up, ffn1_w_down, mix_pre_g, mix_post_g, w_in, conv_w, v_norm_g, w_s, b_s, out_g_a, out_g_b, w_out, ffn2_pre_g, ffn2_post_g, ffn2_w_gate, ffn2_w_up, ffn2_w_down, ple_w_gate, ple_w_proj, ple_post_g):
    depth = ffn1_pre_g.shape[0]
    bp, lp, d = x_prompt.shape
    bs, ls, _ = x_sample.shape
    tm = TOKEN_BLOCK
    assert lp % tm == 0 and (bs * ls) % tm == 0 and tm % ls == 0
    assert CONV_K - 1 <= ls <= V7X_SUBLANES and (lp - 1) % CHUNK + 1 == CHUNK
    n_prompt, n_sample = bp * lp // tm, bs * ls // tm
    x = (x_prompt.reshape(bp * lp, d), x_sample.reshape(bs * ls, d))
    conv_p, conv_s, v_p, v_s = [], [], [], []
    for i in range(depth):
        mw = _mix_weights(ls, mix_pre_g[i], mix_post_g[i], w_in[i], conv_w[i], v_norm_g[i], w_s[i], b_s[i],
                          out_g_a[i], out_g_b[i], w_out[i])
        ple = (p_prompt[i].reshape(bp * lp, -1), p_sample[i].reshape(bs * ls, -1),
               ple_w_gate[i], ple_w_proj[i], _row(ple_post_g[i]))

        h = _ffn(x, n_prompt, n_sample, _row(ffn1_pre_g[i]), _row(ffn1_post_g[i]),
                 ffn1_w_gate[i], ffn1_w_up[i], ffn1_w_down[i], name="ffn1")
        h, cp, vp, cs, vs = _mix(h, state_conv[i].astype(F32), bp, lp, bs, ls, mw)
        x = _ffn(h, n_prompt, n_sample, _row(ffn2_pre_g[i]), _row(ffn2_post_g[i]),
                 ffn2_w_gate[i], ffn2_w_up[i], ffn2_w_down[i], ple, name="ffn2_ple")

        conv_p.append(cp)
        conv_s.append(cs)
        v_p.append(vp.transpose(0, 2, 1).reshape(bp, CHUNK, HEADS_B, HEAD_DIM_B))
        v_s.append(vs.transpose(2, 0, 1).reshape(bs, ls, HEADS_B, HEAD_DIM_B))
    layers = lambda parts: parts[0][None] if depth == 1 else jnp.stack(parts)
    return (x[0].reshape(bp, lp, d), x[1].reshape(bs, ls, d),
            layers(conv_p), layers(conv_s), layers(v_p), layers(v_s))
```

```python
import functools

import jax
import jax.numpy as jnp
from jax import lax
from jax.experimental import pallas as pl
from jax.experimental.pallas import tpu as pltpu

CONV_K = 3
HEADS_B = 8
HEAD_DIM_B = 64
CHUNK_DIM = HEADS_B * HEAD_DIM_B
CONV_DIM = CHUNK_DIM
CHUNK = 128
EPS = 1e-6

V7X_SUBLANES = 8
V7X_LANES = 128
V7X_MXU_COLS = 256
V7X_VMEM_LIMIT_BYTES = 60 * 1024 * 1024

TOKEN_BLOCK = 512
MIX_TOKEN_BLOCK = 512
WEIGHT_TILE = V7X_MXU_COLS
HEADS_PER_SLAB = V7X_MXU_COLS // HEAD_DIM_B
MIX_SEGMENTS = 5
FINISH_SLICES = 8
FINISH_FIRST_CHUNK = 1
MIX_FINISH_SLICES = 4

BF16 = jnp.bfloat16
F32 = jnp.float32


def _rms(x, g):
    return x * lax.rsqrt(jnp.mean(x * x, axis=-1, keepdims=True) + EPS) * g


def _dot(a, b):
    return jnp.dot(a.astype(BF16), b, preferred_element_type=F32)


def _sigmoid(x):
    return 0.5 + 0.5 * jnp.tanh(0.5 * x)


def _silu(x):
    half = 0.5 * x
    return half + half * jnp.tanh(half)


def _after(value, anchor):
    bits = lax.bitcast_convert_type(anchor, jnp.int32)
    zero = lax.shift_right_logical(lax.shift_right_logical(bits, 16), 16)
    return value + zero.astype(value.dtype)


def _store_piece(dst_ref, src_ref, step, n_pieces):
    rows = src_ref.shape[0]

    @pl.when(step < n_pieces)
    def _():
        dst_ref[pl.ds(pl.multiple_of(step * rows, rows), rows), :] = src_ref[...].astype(BF16)


def _ffn_kernel(*refs, n_prep, n_prompt, n_blocks, with_ple):
    if with_ple:
        (xc_ref, xv_ref, pp_ref, ps_ref, pre_ref, post_ref, wg_ref, wu_ref, wd_ref, wpg_ref, wpp_ref, pleg_ref,
         o_ref, os_ref, wgu_s, wd_s, xn_s, a_s, wpg_s, wpp_s, hb_s) = refs
    else:
        (xpc_ref, xsc_ref, xpv_ref, xsv_ref, pre_ref, post_ref, wg_ref, wu_ref, wd_ref,
         o_ref, wgu_s, wd_s, xn_s, a_s) = refs
    i = pl.program_id(0)
    j = i - n_prep
    n_sample = n_blocks - n_prompt
    cur_is_sample = j + 1 >= n_prompt
    prev_is_sample = jnp.logical_and(j >= 0, j < n_sample) if with_ple else j >= n_prompt

    tm = xn_s.shape[0]
    n_slices = FINISH_SLICES // 2 if with_ple else FINISH_SLICES
    slice_rows = tm // n_slices
    stages = [("residual", s) for s in range(n_slices)]
    if with_ple:
        stages += [("embed_dots", 0)] + [("embed_add", s) for s in range(n_slices)]
    assert FINISH_FIRST_CHUNK + len(stages) <= n_prep

    def ordered(r, rows, anchor):
        return r[rows] if anchor is None else _after(r[rows], anchor[rows, 0:1])

    def x_rows(prompt_ref, sample_ref, is_sample, rows, prompt_only):
        if prompt_only:
            return prompt_ref[rows, :]
        return jnp.where(is_sample, sample_ref[rows, :], prompt_ref[rows, :])

    def finish_stage(stage, s, st, anchor=None, prompt_only=False):
        rows = slice(s * slice_rows, (s + 1) * slice_rows)
        if stage == "residual":
            x = xv_ref[rows, :] if with_ple else x_rows(xpv_ref, xsv_ref, prev_is_sample, rows, prompt_only)
            h = x + 0.5 * (st["y"][rows] * ordered(st["r"], rows, anchor) * post_ref[...])
            o_ref[rows, :] = h
            if with_ple:
                hb_s[rows, :] = h.astype(BF16)
        elif stage == "embed_dots":
            p = jnp.where(prev_is_sample, ps_ref[...], pp_ref[...])
            gate = _sigmoid(jnp.dot(hb_s[...], wpg_s[...], preferred_element_type=F32))
            st["e"] = gate * _dot(p, wpp_s[...])
            st["r2"] = lax.rsqrt(jnp.mean(st["e"] * st["e"], axis=-1, keepdims=True) + EPS)
        else:
            o_ref[rows, :] = o_ref[rows, :] + st["e"][rows] * ordered(st["r2"], rows, anchor) * pleg_ref[...]

    def down_prev():
        y = jnp.dot(a_s[...], wd_s[...], preferred_element_type=F32)
        return dict(y=y, r=lax.rsqrt(jnp.mean(y * y, axis=-1, keepdims=True) + EPS))

    def finish_prev():
        st = down_prev()
        for stage, s in stages:
            finish_stage(stage, s, st)

    def gate_up_chunk(c):
        gu = jnp.dot(xn_s[...], wgu_s[c], preferred_element_type=F32)
        g, u = gu[:, 0:WEIGHT_TILE], gu[:, WEIGHT_TILE:]
        return g, (_silu(g) * u).astype(BF16)

    def finish_prev_and_gate_up_cur(prompt_only=False):
        st = down_prev()
        x = xc_ref[...] if with_ple else x_rows(xpc_ref, xsc_ref, cur_is_sample, slice(None), prompt_only)
        x = _after(x, st["y"][:, 0:1])
        xn_s[...] = _rms(x, pre_ref[...]).astype(BF16)
        for c in range(n_prep):
            g, a_s[:, c * WEIGHT_TILE:(c + 1) * WEIGHT_TILE] = gate_up_chunk(c)
            k = c - FINISH_FIRST_CHUNK
            if 0 <= k < len(stages):
                finish_stage(*stages[k], st, anchor=g, prompt_only=prompt_only)

    @pl.when(i == 0)
    def _():
        xn_s[...] = _rms(xc_ref[...] if with_ple else xpc_ref[...], pre_ref[...]).astype(BF16)

    @pl.when(i < n_prep)
    def _():
        wgu_s[i, :, 0:WEIGHT_TILE] = wg_ref[...].astype(BF16)
        wgu_s[i, :, WEIGHT_TILE:] = wu_ref[...].astype(BF16)
        _, a = gate_up_chunk(i)
        for c in range(n_prep):
            @pl.when(i == c)
            def _():
                a_s[:, c * WEIGHT_TILE:(c + 1) * WEIGHT_TILE] = a

    _store_piece(wd_s, wd_ref, i, n_prep)
    if with_ple:
        _store_piece(wpg_s, wpg_ref, i, wpg_s.shape[0] // wpg_ref.shape[0])
        _store_piece(wpp_s, wpp_ref, i, 1)

    if with_ple:
        @pl.when(jnp.logical_and(j >= 0, j < n_blocks - 1))
        def _():
            finish_prev_and_gate_up_cur()
    else:
        @pl.when(jnp.logical_and(j >= 0, j < n_prompt - 1))
        def _():
            finish_prev_and_gate_up_cur(prompt_only=True)

        @pl.when(jnp.logical_and(j >= n_prompt - 1, j < n_blocks - 1))
        def _():
            finish_prev_and_gate_up_cur()

    @pl.when(j == n_blocks - 1)
    def _():
        finish_prev()

    if with_ple:
        @pl.when(prev_is_sample)
        def _():
            os_ref[...] = o_ref[...]


def _head_rms(v, ones_slab, g):
    vv = (v * v).astype(BF16)
    ms = jnp.concatenate(
        [jnp.dot(vv[:, k * V7X_MXU_COLS:(k + 1) * V7X_MXU_COLS], ones_slab, preferred_element_type=F32)
         for k in range(CHUNK_DIM // V7X_MXU_COLS)], axis=1)
    return v * lax.rsqrt(ms + EPS) * g


def _mix_kernel(hc_ref, hv_ref, state_ref, pre_ref, post_ref, win_ref, wout_ref, convw_ref, vg_ref,
                ws_ref, b_ref, ga_ref, gb_ref,
                o_ref, convp_ref, vp_ref, convs_ref, vs_ref,
                win_s, wout_s, n_s, yab_s, carry_ref, vn_s, vt_s, ones_ref,
                *, n_prep, n_prompt, n_blocks, blocks_per_seq, seq_len):
    tm = hc_ref.shape[0]
    i = pl.program_id(0)
    j = i - n_prep

    @pl.when(i < n_prep)
    def _():
        win_s[i] = win_ref[...].astype(BF16)

    _store_piece(wout_s, wout_ref, i, wout_s.shape[0] // wout_ref.shape[0])

    @pl.when(i == 0)
    def _():
        yab_s[...] = jnp.zeros_like(yab_s)
        carry_ref[...] = jnp.zeros_like(carry_ref)
        row_head = lax.broadcasted_iota(jnp.int32, ones_ref.shape, 0) // HEAD_DIM_B
        col_head = lax.broadcasted_iota(jnp.int32, ones_ref.shape, 1) // HEAD_DIM_B
        ones_ref[...] = jnp.where(row_head == col_head, 1.0 / HEAD_DIM_B, 0.0).astype(BF16)

    slice_rows = tm // MIX_FINISH_SLICES

    def out_proj_prev():
        y = jnp.dot(yab_s[...], wout_s[...], preferred_element_type=F32)
        return dict(y=y, r=lax.rsqrt(jnp.mean(y * y, axis=-1, keepdims=True) + EPS), done=0)

    def finish_slice(prev, anchor=None):
        rows = slice(prev["done"] * slice_rows, (prev["done"] + 1) * slice_rows)
        prev["done"] += 1
        r = prev["r"][rows] if anchor is None else _after(prev["r"][rows], anchor[rows, 0:1])
        o_ref[rows, :] = hv_ref[rows, :] + prev["y"][rows] * r * post_ref[...]

    def seg(k, prev):
        out = jnp.dot(n_s[...], win_s[k], preferred_element_type=F32)
        if prev["done"] < MIX_FINISH_SLICES:
            finish_slice(prev, anchor=out)
        return out

    def conv(z, z1, z2):
        return convw_ref[0:1, :] * z2 + convw_ref[1:2, :] * z1 + convw_ref[2:3, :] * z

    def per_head(piece, rows):
        return jnp.concatenate([jnp.broadcast_to(piece(h), (rows, HEAD_DIM_B)) for h in range(HEADS_B)], axis=1)

    def v_gain():
        return jnp.concatenate([vg_ref[h:h + 1, :] for h in range(HEADS_B)], axis=1)

    def chunk_bias(rows):
        b_t = b_ref[...].T
        return per_head(lambda h: b_t[0:rows, h:h + 1], rows)

    def group_norms(conv_out, mixed, prev):
        yab_s[:, 0:CONV_DIM] = _rms(seg(0, prev) * conv_out, ga_ref[...]).astype(BF16)
        yab_s[:, CONV_DIM:] = _rms(seg(3, prev) * mixed, gb_ref[...]).astype(BF16)

    def prompt_block():
        prev = out_proj_prev()
        n_s[...] = _rms(hc_ref[...], pre_ref[...]).astype(BF16)

        v_n = _head_rms(seg(4, prev), ones_ref[...], v_gain())
        vp_ref[0] = v_n[tm - CHUNK:, :].T
        t_idx = lax.broadcasted_iota(jnp.int32, (CHUNK, HEADS_PER_SLAB * CHUNK), 0)
        s_idx = lax.broadcasted_iota(jnp.int32, (CHUNK, HEADS_PER_SLAB * CHUNK), 1) % CHUNK
        causal = t_idx >= s_idx
        lane_head = lax.broadcasted_iota(jnp.int32, (CHUNK, V7X_MXU_COLS), 1) // HEAD_DIM_B
        n_slabs = CHUNK_DIM // V7X_MXU_COLS
        w_slab = [jnp.where(causal, jnp.concatenate([ws_ref[k * HEADS_PER_SLAB + hd] for hd in range(HEADS_PER_SLAB)],
                                                     axis=1), 0.0).astype(BF16) for k in range(n_slabs)]
        bias = chunk_bias(CHUNK)
        chunks = []
        for c in range(tm // CHUNK):
            slabs = []
            for k in range(n_slabs):
                cols = slice(k * V7X_MXU_COLS, (k + 1) * V7X_MXU_COLS)
                v_ck = v_n[c * CHUNK:(c + 1) * CHUNK, cols].astype(BF16)
                bd = jnp.concatenate(
                    [jnp.where(lane_head == hd, v_ck, jnp.zeros((), BF16)) for hd in range(HEADS_PER_SLAB)], axis=0)
                slabs.append(jnp.dot(w_slab[k], bd, preferred_element_type=F32))
            chunks.append(jnp.concatenate(slabs, axis=1) + bias)
        mixed = jnp.concatenate(chunks, axis=0)

        z = seg(1, prev) * seg(2, prev)
        row = lax.broadcasted_iota(jnp.int32, z.shape, 0)
        carry = jnp.where(j % blocks_per_seq == 0, 0.0, carry_ref[...])
        prev1 = carry[1:2, :]
        prev2 = carry[0:1, :]
        z1 = jnp.where(row == 0, prev1, pltpu.roll(z, 1, axis=0))
        z2 = jnp.where(row == 0, prev2, jnp.where(row == 1, prev1, pltpu.roll(z, 2, axis=0)))
        tail = z[tm - (CONV_K - 1):, :]
        carry_ref[...] = tail
        convp_ref[0] = tail
        group_norms(conv(z, z1, z2), mixed, prev)

    def sample_block():
        prev = out_proj_prev()
        n_s[...] = _rms(hc_ref[...], pre_ref[...]).astype(BF16)

        n_seq = tm // seq_len
        per_seq = lambda a: a.reshape(n_seq, seq_len, a.shape[-1])
        every_row = lambda a: jnp.broadcast_to(a, (n_seq, seq_len, a.shape[-1]))
        v_n = _head_rms(seg(4, prev), ones_ref[...], v_gain())
        first_seq = pl.multiple_of((j - n_prompt) * n_seq, n_seq)
        for c in range(CHUNK_DIM // V7X_LANES):
            lanes = slice(c * V7X_LANES, (c + 1) * V7X_LANES)
            vn_s[c] = v_n[:, lanes]
            for t in range(seq_len):
                vt_s[t, pl.ds(first_seq, n_seq), lanes] = vn_s[c, pl.ds(t, n_seq, stride=seq_len), :]

        @pl.when(j == n_blocks - 1)
        def _():
            for t in range(seq_len):
                vs_ref[t] = vt_s[t].T
        v_seq = per_seq(v_n)
        t_row = lax.broadcasted_iota(jnp.int32, (seq_len, CHUNK_DIM), 0)
        mixed = every_row(chunk_bias(seq_len)[None])
        for s in range(seq_len):
            w_from_s = per_head(lambda h: ws_ref[h, 0:seq_len, s:s + 1], seq_len)
            mixed = mixed + jnp.where(t_row >= s, w_from_s, 0.0)[None] * every_row(v_seq[:, s:s + 1, :])
        mixed = mixed.reshape(tm, CHUNK_DIM)

        z = seg(1, prev) * seg(2, prev)
        t = lax.broadcasted_iota(jnp.int32, z.shape, 0) % seq_len
        st = state_ref[...]
        prev1 = every_row(st[:, 1:2, :]).reshape(tm, CONV_DIM)
        prev2 = every_row(st[:, 0:1, :]).reshape(tm, CONV_DIM)
        z1 = jnp.where(t == 0, prev1, pltpu.roll(z, 1, axis=0))
        z2 = jnp.where(t == 0, prev2, jnp.where(t == 1, prev1, pltpu.roll(z, 2, axis=0)))
        convs_ref[...] = per_seq(z)[:, seq_len - (CONV_K - 1):, :]
        group_norms(conv(z, z1, z2), mixed, prev)

    @pl.when(jnp.logical_and(j >= 0, j < n_prompt))
    def _():
        prompt_block()

    @pl.when(jnp.logical_and(j >= n_prompt, j < n_blocks))
    def _():
        sample_block()

    @pl.when(j == n_blocks)
    def _():
        prev = out_proj_prev()
        for _ in range(MIX_FINISH_SLICES):
            finish_slice(prev)


def _resident(a):
    return pl.BlockSpec(a.shape, lambda i: (0,) * a.ndim, pipeline_mode=pl.Buffered(1))


def _clamped(lo, n):
    return lambda i: jnp.minimum(jnp.maximum(i - lo, 0), n - 1)


def _stream(tm, d, lo, n):
    blk = _clamped(lo, n)
    return pl.BlockSpec((tm, d), lambda i: (blk(i), 0))


def _row_pieces(w, rows):
    blk = _clamped(0, w.shape[0] // rows)
    return pl.BlockSpec((rows, w.shape[1]), lambda i: (blk(i), 0))


def _col_pieces(w, cols):
    blk = _clamped(0, w.shape[1] // cols)
    return pl.BlockSpec((w.shape[0], cols), lambda i: (0, blk(i)))


def _params():
    return pltpu.CompilerParams(dimension_semantics=("arbitrary",), vmem_limit_bytes=V7X_VMEM_LIMIT_BYTES)


def _ffn(x, n_prompt, n_sample, pre_g, post_g, wg, wu, wd, ple=None, *, name):
    tm = TOKEN_BLOCK
    d, d_ff = wg.shape
    n_prep = d_ff // WEIGHT_TILE
    n_blocks = n_prompt + n_sample
    nxt, prv = n_prep - 1, n_prep
    weights = [_resident(pre_g), _resident(post_g),
               _col_pieces(wg, WEIGHT_TILE), _col_pieces(wu, WEIGHT_TILE), _row_pieces(wd, WEIGHT_TILE)]
    scratch = [pltpu.VMEM((n_prep, d, 2 * WEIGHT_TILE), BF16),
               pltpu.VMEM((d_ff, d), BF16), pltpu.VMEM((tm, d), BF16), pltpu.VMEM((tm, d_ff), BF16)]
    if ple is None:
        xp, xs = x
        in_specs = [_stream(tm, d, nxt, n_prompt), _stream(tm, d, nxt + n_prompt, n_sample),
                    _stream(tm, d, prv, n_prompt), _stream(tm, d, prv + n_prompt, n_sample)] + weights
        args = [xp, xs, xp, xs, pre_g, post_g, wg, wu, wd]
        out_specs = _stream(tm, d, prv, n_blocks)
        out_shape = jax.ShapeDtypeStruct((n_blocks * tm, d), F32)
    else:
        pp, ps, wpg, wpp, ple_g = ple
        assert wpg.shape[0] // (WEIGHT_TILE // 2) <= n_prep
        d_p = pp.shape[1]

        def sample_first(lo):
            k = _clamped(lo, n_blocks)
            return pl.BlockSpec((tm, d), lambda i: (jnp.where(k(i) < n_sample, k(i) + n_prompt, k(i) - n_sample), 0))

        in_specs = [sample_first(nxt), sample_first(prv),
                    _stream(tm, d_p, prv + n_sample, n_prompt), _stream(tm, d_p, prv, n_sample)] + weights
        in_specs += [_row_pieces(wpg, WEIGHT_TILE // 2), _row_pieces(wpp, wpp.shape[0]), _resident(ple_g)]
        args = [x, x, pp, ps, pre_g, post_g, wg, wu, wd, wpg, wpp, ple_g]
        out_specs = [_stream(tm, d, prv + n_sample, n_prompt), _stream(tm, d, prv, n_sample)]
        out_shape = [jax.ShapeDtypeStruct((n_prompt * tm, d), F32), jax.ShapeDtypeStruct((n_sample * tm, d), F32)]
        scratch += [pltpu.VMEM(wpg.shape, BF16), pltpu.VMEM(wpp.shape, BF16), pltpu.VMEM((tm, d), BF16)]
    return pl.pallas_call(
        functools.partial(_ffn_kernel, n_prep=n_prep, n_prompt=n_prompt, n_blocks=n_blocks,
                          with_ple=ple is not None),
        grid=(n_prep + n_blocks,),
        in_specs=in_specs,
        out_specs=out_specs,
        out_shape=out_shape,
        scratch_shapes=scratch,
        compiler_params=_params(),
        name=name,
    )(*args)


def _mix(h, state, batch_p, seq_p, batch_s, seq_s, mw):
    tm = MIX_TOKEN_BLOCK
    d = h.shape[1]
    n_prep = MIX_SEGMENTS
    assert seq_p % tm == 0 and (batch_s * seq_s) % tm == 0 and tm % seq_s == 0
    n_prompt, n_sample = batch_p * seq_p // tm, batch_s * seq_s // tm
    n_blocks = n_prompt + n_sample
    blocks_per_seq = seq_p // tm
    seqs_per_block = tm // seq_s
    assert d // WEIGHT_TILE <= n_prep
    small = [mw["v_g"], mw["w_s"], mw["b_s"], mw["g_a"], mw["g_b"]]
    conv_w_spec = pl.BlockSpec((None,) + mw["conv_w"].shape[1:], lambda i: (0, 0, 0), pipeline_mode=pl.Buffered(1))
    lo_s = n_prep + n_prompt
    prompt_seq = _clamped(0, batch_p)
    per_prompt_seq = lambda i: (prompt_seq((i - n_prep) // blocks_per_seq), 0, 0)
    sample_blk = _clamped(lo_s, n_sample)
    per_sample_blk = pl.BlockSpec((seqs_per_block, CONV_K - 1, CONV_DIM), lambda i: (sample_blk(i), 0, 0))
    return pl.pallas_call(
        functools.partial(_mix_kernel, n_prep=n_prep, n_prompt=n_prompt, n_blocks=n_blocks,
                          blocks_per_seq=blocks_per_seq, seq_len=seq_s),
        grid=(n_prep + n_blocks + 1,),
        in_specs=[_stream(tm, d, n_prep, n_blocks), _stream(tm, d, n_prep + 1, n_blocks),
                  per_sample_blk, _resident(mw["pre"]), _resident(mw["post"]),
                  _col_pieces(mw["w_in"], CONV_DIM), _row_pieces(mw["w_out"], WEIGHT_TILE), conv_w_spec]
                 + [_resident(a) for a in small],
        out_specs=[_stream(tm, d, n_prep + 1, n_blocks),
                   pl.BlockSpec((1, CONV_K - 1, CONV_DIM), per_prompt_seq),
                   pl.BlockSpec((1, CHUNK_DIM, CHUNK), per_prompt_seq),
                   per_sample_blk,
                   pl.BlockSpec((seq_s, CHUNK_DIM, batch_s), lambda i: (0, 0, 0))],
        out_shape=[jax.ShapeDtypeStruct(h.shape, F32),
                   jax.ShapeDtypeStruct((batch_p, CONV_K - 1, CONV_DIM), F32),
                   jax.ShapeDtypeStruct((batch_p, CHUNK_DIM, CHUNK), F32),
                   jax.ShapeDtypeStruct((batch_s, CONV_K - 1, CONV_DIM), F32),
                   jax.ShapeDtypeStruct((seq_s, CHUNK_DIM, batch_s), F32)],
        scratch_shapes=[pltpu.VMEM((MIX_SEGMENTS, d, CONV_DIM), BF16), pltpu.VMEM(mw["w_out"].shape, BF16),
                        pltpu.VMEM((tm, d), BF16), pltpu.VMEM((tm, CONV_DIM + CHUNK_DIM), BF16),
                        pltpu.VMEM((CONV_K - 1, CONV_DIM), F32),
                        pltpu.VMEM((CHUNK_DIM // V7X_LANES, tm, V7X_LANES), F32),
                        pltpu.VMEM((seq_s, batch_s, CHUNK_DIM), F32),
                        pltpu.VMEM((V7X_MXU_COLS, V7X_MXU_COLS), BF16)],
        compiler_params=_params(),
        name="mix",
    )(h, h, state, mw["pre"], mw["post"], mw["w_in"], mw["w_out"], mw["conv_w"], *small)


def _row(g):
    return g.reshape(1, -1).astype(F32)


def _mix_weights(mix_pre_g, mix_post_g, w_in, conv_w, v_norm_g, w_s, b_s, out_g_a, out_g_b, w_out):
    return dict(pre=_row(mix_pre_g), post=_row(mix_post_g), w_in=w_in, w_out=w_out,
                conv_w=conv_w.astype(F32), v_g=v_norm_g.astype(F32), w_s=w_s.astype(F32),
                b_s=b_s.astype(F32), g_a=_row(out_g_a), g_b=_row(out_g_b))


def kernel(x_prompt, x_sample, p_prompt, p_sample, state_conv, ffn1_pre_g, ffn1_post_g, ffn1_w_gate, ffn1_w_up, ffn1_w_down, mix_pre_g, mix_post_g, w_in, conv_w, v_norm_g, w_s, b_s, out_g_a, out_g_b, w_out, ffn2_pre_g, ffn2_post_g, ffn2_w_gate, ffn2_w_up, ffn2_w_down, ple_w_gate, ple_w_proj, ple_post_g):
    depth = ffn1_pre_g.shape[0]
    bp, lp, d = x_prompt.shape
    bs, ls, _ = x_sample.shape
    tm = TOKEN_BLOCK
    assert lp % tm == 0 and (bs * ls) % tm == 0 and tm % ls == 0
    assert CONV_K - 1 <= ls <= V7X_SUBLANES and (lp - 1) % CHUNK + 1 == CHUNK
    n_prompt, n_sample = bp * lp // tm, bs * ls // tm
    x = (x_prompt.reshape(bp * lp, d), x_sample.reshape(bs * ls, d))
    conv_p, conv_s, v_p, v_s = [], [], [], []
    for i in range(depth):
        mw = _mix_weights(mix_pre_g[i], mix_post_g[i], w_in[i], conv_w[i:i + 1], v_norm_g[i], w_s[i], b_s[i],
                          out_g_a[i], out_g_b[i], w_out[i])
        ple = (p_prompt[i].reshape(bp * lp, -1), p_sample[i].reshape(bs * ls, -1),
               ple_w_gate[i], ple_w_proj[i], _row(ple_post_g[i]))

        h = _ffn(x, n_prompt, n_sample, _row(ffn1_pre_g[i]), _row(ffn1_post_g[i]),
                 ffn1_w_gate[i], ffn1_w_up[i], ffn1_w_down[i], name="ffn1")
        h, cp, vp, cs, vs = _mix(h, state_conv[i].astype(F32), bp, lp, bs, ls, mw)
        x = _ffn(h, n_prompt, n_sample, _row(ffn2_pre_g[i]), _row(ffn2_post_g[i]),
                 ffn2_w_gate[i], ffn2_w_up[i], ffn2_w_down[i], ple, name="ffn2_ple")

        conv_p.append(cp)
        conv_s.append(cs)
        v_p.append(vp.transpose(0, 2, 1).reshape(bp, CHUNK, HEADS_B, HEAD_DIM_B))
        v_s.append(vs.transpose(2, 0, 1).reshape(bs, ls, HEADS_B, HEAD_DIM_B))
    layers = lambda parts: parts[0][None] if depth == 1 else jnp.stack(parts)
    return (x[0].reshape(bp, lp, d), x[1].reshape(bs, ls, d),
            layers(conv_p), layers(conv_s), layers(v_p), layers(v_s))
```

```python
import functools

import jax
import jax.numpy as jnp
from jax import lax
from jax.experimental import pallas as pl
from jax.experimental.pallas import tpu as pltpu

CONV_K = 3
HEADS_B = 8
HEAD_DIM_B = 64
CHUNK_DIM = HEADS_B * HEAD_DIM_B
CONV_DIM = CHUNK_DIM
CHUNK = 128
EPS = 1e-6

V7X_SUBLANES = 8
V7X_LANES = 128
V7X_MXU_COLS = 256
V7X_VMEM_LIMIT_BYTES = 60 * 1024 * 1024

TOKEN_BLOCK = 512
MIX_TOKEN_BLOCK = 512
WEIGHT_TILE = V7X_MXU_COLS
HEADS_PER_SLAB = V7X_MXU_COLS // HEAD_DIM_B
MIX_SEGMENTS = 5
FINISH_SLICES = 8
FINISH_FIRST_CHUNK = 1
MIX_FINISH_SLICES = 4

BF16 = jnp.bfloat16
F32 = jnp.float32


def _rms(x, g):
    return x * lax.rsqrt(jnp.mean(x * x, axis=-1, keepdims=True) + EPS) * g


def _dot(a, b):
    return jnp.dot(a.astype(BF16), b, preferred_element_type=F32)


def _sigmoid(x):
    return 0.5 + 0.5 * jnp.tanh(0.5 * x)


def _silu(x):
    half = 0.5 * x
    return half + half * jnp.tanh(half)


def _after(value, anchor):
    bits = lax.bitcast_convert_type(anchor, jnp.int32)
    zero = lax.shift_right_logical(lax.shift_right_logical(bits, 16), 16)
    return value + zero.astype(value.dtype)


def _store_piece(dst_ref, src_ref, step, n_pieces):
    rows = src_ref.shape[0]

    @pl.when(step < n_pieces)
    def _():
        dst_ref[pl.ds(pl.multiple_of(step * rows, rows), rows), :] = src_ref[...].astype(BF16)


def _ffn_kernel(*refs, n_prep, n_prompt, n_blocks, with_ple):
    if with_ple:
        (xc_ref, xv_ref, pp_ref, ps_ref, pre_ref, post_ref, wg_ref, wu_ref, wd_ref, wpg_ref, wpp_ref, pleg_ref,
         o_ref, os_ref, wgu_s, wd_s, xn_s, a_s, wpg_s, wpp_s, hb_s) = refs
    else:
        (xpc_ref, xsc_ref, xpv_ref, xsv_ref, pre_ref, post_ref, wg_ref, wu_ref, wd_ref,
         o_ref, wgu_s, wd_s, xn_s, a_s) = refs
    i = pl.program_id(0)
    j = i - (n_prep + 1)
    n_sample = n_blocks - n_prompt
    cur_is_sample = j + 1 >= n_prompt
    prev_is_sample = jnp.logical_and(j >= 0, j < n_sample) if with_ple else j >= n_prompt

    tm = xn_s.shape[0]
    n_slices = FINISH_SLICES // 2 if with_ple else FINISH_SLICES
    slice_rows = tm // n_slices
    stages = [("residual", s) for s in range(n_slices)]
    if with_ple:
        stages += [("embed_dots", 0)] + [("embed_add", s) for s in range(n_slices)]
    assert FINISH_FIRST_CHUNK + len(stages) <= n_prep

    def ordered(r, rows, anchor):
        return r[rows] if anchor is None else _after(r[rows], anchor[rows, 0:1])

    def x_rows(prompt_ref, sample_ref, is_sample, rows, prompt_only):
        if prompt_only:
            return prompt_ref[rows, :]
        return jnp.where(is_sample, sample_ref[rows, :], prompt_ref[rows, :])

    def finish_stage(stage, s, st, anchor=None, prompt_only=False):
        rows = slice(s * slice_rows, (s + 1) * slice_rows)
        if stage == "residual":
            x = xv_ref[rows, :] if with_ple else x_rows(xpv_ref, xsv_ref, prev_is_sample, rows, prompt_only)
            h = x + 0.5 * (st["y"][rows] * ordered(st["r"], rows, anchor) * post_ref[...])
            o_ref[rows, :] = h
            if with_ple:
                hb_s[rows, :] = h.astype(BF16)
        elif stage == "embed_dots":
            p = jnp.where(prev_is_sample, ps_ref[...], pp_ref[...])
            gate = _sigmoid(jnp.dot(hb_s[...], wpg_s[...], preferred_element_type=F32))
            st["e"] = gate * _dot(p, wpp_s[...])
            st["r2"] = lax.rsqrt(jnp.mean(st["e"] * st["e"], axis=-1, keepdims=True) + EPS)
        else:
            o_ref[rows, :] = o_ref[rows, :] + st["e"][rows] * ordered(st["r2"], rows, anchor) * pleg_ref[...]

    def down_prev():
        y = jnp.dot(a_s[...], wd_s[...], preferred_element_type=F32)
        return dict(y=y, r=lax.rsqrt(jnp.mean(y * y, axis=-1, keepdims=True) + EPS))

    def finish_prev():
        st = down_prev()
        for stage, s in stages:
            finish_stage(stage, s, st)

    def gate_up_chunk(c):
        gu = jnp.dot(xn_s[...], wgu_s[c], preferred_element_type=F32)
        g, u = gu[:, 0:WEIGHT_TILE], gu[:, WEIGHT_TILE:]
        return g, (_silu(g) * u).astype(BF16)

    def finish_prev_and_gate_up_cur(prompt_only=False):
        st = down_prev()
        x = xc_ref[...] if with_ple else x_rows(xpc_ref, xsc_ref, cur_is_sample, slice(None), prompt_only)
        x = _after(x, st["y"][:, 0:1])
        xn_s[...] = _rms(x, pre_ref[...]).astype(BF16)
        for c in range(n_prep):
            g, a_s[:, c * WEIGHT_TILE:(c + 1) * WEIGHT_TILE] = gate_up_chunk(c)
            k = c - FINISH_FIRST_CHUNK
            if 0 <= k < len(stages):
                finish_stage(*stages[k], st, anchor=g, prompt_only=prompt_only)

    def store_chunk():
        wgu_s[i, :, 0:WEIGHT_TILE] = wg_ref[...].astype(BF16)
        wgu_s[i, :, WEIGHT_TILE:] = wu_ref[...].astype(BF16)
        rows = wd_ref.shape[0]
        wd_s[pl.ds(pl.multiple_of(i * rows, rows), rows), :] = wd_ref[...].astype(BF16)

    def block0_chunk(and_store_chunk):
        _, a = gate_up_chunk(i - 1)
        if and_store_chunk:
            store_chunk()
        for c in range(n_prep):
            @pl.when(i - 1 == c)
            def _():
                a_s[:, c * WEIGHT_TILE:(c + 1) * WEIGHT_TILE] = a

    @pl.when(i == 0)
    def _():
        xn_s[...] = _rms(xc_ref[...] if with_ple else xpc_ref[...], pre_ref[...]).astype(BF16)
        store_chunk()

    @pl.when(jnp.logical_and(i >= 1, i < n_prep))
    def _():
        block0_chunk(and_store_chunk=True)

    @pl.when(i == n_prep)
    def _():
        block0_chunk(and_store_chunk=False)

    if with_ple:
        _store_piece(wpg_s, wpg_ref, i, wpg_s.shape[0] // wpg_ref.shape[0])
        _store_piece(wpp_s, wpp_ref, i, 1)

    if with_ple:
        @pl.when(jnp.logical_and(j >= 0, j < n_blocks - 1))
        def _():
            finish_prev_and_gate_up_cur()
    else:
        @pl.when(jnp.logical_and(j >= 0, j < n_prompt - 1))
        def _():
            finish_prev_and_gate_up_cur(prompt_only=True)

        @pl.when(jnp.logical_and(j >= n_prompt - 1, j < n_blocks - 1))
        def _():
            finish_prev_and_gate_up_cur()

    @pl.when(j == n_blocks - 1)
    def _():
        finish_prev()

    if with_ple:
        @pl.when(prev_is_sample)
        def _():
            os_ref[...] = o_ref[...]


def _head_rms(v, ones_slab, g):
    vv = (v * v).astype(BF16)
    ms = jnp.concatenate(
        [jnp.dot(vv[:, k * V7X_MXU_COLS:(k + 1) * V7X_MXU_COLS], ones_slab, preferred_element_type=F32)
         for k in range(CHUNK_DIM // V7X_MXU_COLS)], axis=1)
    return v * lax.rsqrt(ms + EPS) * g


def _mix_kernel(hc_ref, hv_ref, state_ref, pre_ref, post_ref, win_ref, wout_ref, convw_ref, vg_ref,
                ws_ref, b_ref, ga_ref, gb_ref,
                o_ref, convp_ref, vp_ref, convs_ref, vs_ref,
                win_s, wout_s, n_s, yab_s, carry_ref, vn_s, vt_s, ones_ref,
                *, n_prep, n_prompt, n_blocks, blocks_per_seq, seq_len):
    tm = hc_ref.shape[0]
    i = pl.program_id(0)
    j = i - n_prep

    @pl.when(i < n_prep)
    def _():
        win_s[i] = win_ref[...].astype(BF16)

    _store_piece(wout_s, wout_ref, i, wout_s.shape[0] // wout_ref.shape[0])

    @pl.when(i == 0)
    def _():
        yab_s[...] = jnp.zeros_like(yab_s)
        carry_ref[...] = jnp.zeros_like(carry_ref)
        row_head = lax.broadcasted_iota(jnp.int32, ones_ref.shape, 0) // HEAD_DIM_B
        col_head = lax.broadcasted_iota(jnp.int32, ones_ref.shape, 1) // HEAD_DIM_B
        ones_ref[...] = jnp.where(row_head == col_head, 1.0 / HEAD_DIM_B, 0.0).astype(BF16)

    slice_rows = tm // MIX_FINISH_SLICES

    def out_proj_prev():
        y = jnp.dot(yab_s[...], wout_s[...], preferred_element_type=F32)
        return dict(y=y, r=lax.rsqrt(jnp.mean(y * y, axis=-1, keepdims=True) + EPS), done=0)

    def finish_slice(prev, anchor=None):
        rows = slice(prev["done"] * slice_rows, (prev["done"] + 1) * slice_rows)
        prev["done"] += 1
        r = prev["r"][rows] if anchor is None else _after(prev["r"][rows], anchor[rows, 0:1])
        o_ref[rows, :] = hv_ref[rows, :] + prev["y"][rows] * r * post_ref[...]

    def seg(k, prev):
        out = jnp.dot(n_s[...], win_s[k], preferred_element_type=F32)
        if prev["done"] < MIX_FINISH_SLICES:
            finish_slice(prev, anchor=out)
        return out

    def conv(z, z1, z2):
        return convw_ref[0:1, :] * z2 + convw_ref[1:2, :] * z1 + convw_ref[2:3, :] * z

    def per_head(piece, rows):
        return jnp.concatenate([jnp.broadcast_to(piece(h), (rows, HEAD_DIM_B)) for h in range(HEADS_B)], axis=1)

    def v_gain():
        return jnp.concatenate([vg_ref[h:h + 1, :] for h in range(HEADS_B)], axis=1)

    def chunk_bias(rows):
        b_t = b_ref[...].T
        return per_head(lambda h: b_t[0:rows, h:h + 1], rows)

    def group_norms(conv_out, mixed, prev):
        yab_s[:, 0:CONV_DIM] = _rms(seg(0, prev) * conv_out, ga_ref[...]).astype(BF16)
        yab_s[:, CONV_DIM:] = _rms(seg(3, prev) * mixed, gb_ref[...]).astype(BF16)

    def prompt_block():
        prev = out_proj_prev()
        n_s[...] = _rms(hc_ref[...], pre_ref[...]).astype(BF16)

        v_n = _head_rms(seg(4, prev), ones_ref[...], v_gain())
        vp_ref[0] = v_n[tm - CHUNK:, :].T
        t_idx = lax.broadcasted_iota(jnp.int32, (CHUNK, HEADS_PER_SLAB * CHUNK), 0)
        s_idx = lax.broadcasted_iota(jnp.int32, (CHUNK, HEADS_PER_SLAB * CHUNK), 1) % CHUNK
        causal = t_idx >= s_idx
        lane_head = lax.broadcasted_iota(jnp.int32, (CHUNK, V7X_MXU_COLS), 1) // HEAD_DIM_B
        n_slabs = CHUNK_DIM // V7X_MXU_COLS
        w_slab = [jnp.where(causal, jnp.concatenate([ws_ref[k * HEADS_PER_SLAB + hd] for hd in range(HEADS_PER_SLAB)],
                                                     axis=1), 0.0).astype(BF16) for k in range(n_slabs)]
        bias = chunk_bias(CHUNK)
        chunks = []
        for c in range(tm // CHUNK):
            slabs = []
            for k in range(n_slabs):
                cols = slice(k * V7X_MXU_COLS, (k + 1) * V7X_MXU_COLS)
                v_ck = v_n[c * CHUNK:(c + 1) * CHUNK, cols].astype(BF16)
                bd = jnp.concatenate(
                    [jnp.where(lane_head == hd, v_ck, jnp.zeros((), BF16)) for hd in range(HEADS_PER_SLAB)], axis=0)
                slabs.append(jnp.dot(w_slab[k], bd, preferred_element_type=F32))
            chunks.append(jnp.concatenate(slabs, axis=1) + bias)
        mixed = jnp.concatenate(chunks, axis=0)

        z = seg(1, prev) * seg(2, prev)
        row = lax.broadcasted_iota(jnp.int32, z.shape, 0)
        carry = jnp.where(j % blocks_per_seq == 0, 0.0, carry_ref[...])
        prev1 = carry[1:2, :]
        prev2 = carry[0:1, :]
        z1 = jnp.where(row == 0, prev1, pltpu.roll(z, 1, axis=0))
        z2 = jnp.where(row == 0, prev2, jnp.where(row == 1, prev1, pltpu.roll(z, 2, axis=0)))
        tail = z[tm - (CONV_K - 1):, :]
        carry_ref[...] = tail
        convp_ref[0] = tail
        group_norms(conv(z, z1, z2), mixed, prev)

    def sample_block():
        prev = out_proj_prev()
        n_s[...] = _rms(hc_ref[...], pre_ref[...]).astype(BF16)

        n_seq = tm // seq_len
        per_seq = lambda a: a.reshape(n_seq, seq_len, a.shape[-1])
        every_row = lambda a: jnp.broadcast_to(a, (n_seq, seq_len, a.shape[-1]))
        v_n = _head_rms(seg(4, prev), ones_ref[...], v_gain())
        first_seq = pl.multiple_of((j - n_prompt) * n_seq, n_seq)
        for c in range(CHUNK_DIM // V7X_LANES):
            lanes = slice(c * V7X_LANES, (c + 1) * V7X_LANES)
            vn_s[c] = v_n[:, lanes]
            for t in range(seq_len):
                vt_s[t, pl.ds(first_seq, n_seq), lanes] = vn_s[c, pl.ds(t, n_seq, stride=seq_len), :]

        @pl.when(j == n_blocks - 1)
        def _():
            for t in range(seq_len):
                vs_ref[t] = vt_s[t].T
        v_seq = per_seq(v_n)
        t_row = lax.broadcasted_iota(jnp.int32, (seq_len, CHUNK_DIM), 0)
        mixed = every_row(chunk_bias(seq_len)[None])
        for s in range(seq_len):
            w_from_s = per_head(lambda h: ws_ref[h, 0:seq_len, s:s + 1], seq_len)
            mixed = mixed + jnp.where(t_row >= s, w_from_s, 0.0)[None] * every_row(v_seq[:, s:s + 1, :])
        mixed = mixed.reshape(tm, CHUNK_DIM)

        z = seg(1, prev) * seg(2, prev)
        t = lax.broadcasted_iota(jnp.int32, z.shape, 0) % seq_len
        st = state_ref[...]
        prev1 = every_row(st[:, 1:2, :]).reshape(tm, CONV_DIM)
        prev2 = every_row(st[:, 0:1, :]).reshape(tm, CONV_DIM)
        z1 = jnp.where(t == 0, prev1, pltpu.roll(z, 1, axis=0))
        z2 = jnp.where(t == 0, prev2, jnp.where(t == 1, prev1, pltpu.roll(z, 2, axis=0)))
        convs_ref[...] = per_seq(z)[:, seq_len - (CONV_K - 1):, :]
        group_norms(conv(z, z1, z2), mixed, prev)

    @pl.when(jnp.logical_and(j >= 0, j < n_prompt))
    def _():
        prompt_block()

    @pl.when(jnp.logical_and(j >= n_prompt, j < n_blocks))
    def _():
        sample_block()

    @pl.when(j == n_blocks)
    def _():
        prev = out_proj_prev()
        for _ in range(MIX_FINISH_SLICES):
            finish_slice(prev)


def _resident(a):
    return pl.BlockSpec(a.shape, lambda i: (0,) * a.ndim, pipeline_mode=pl.Buffered(1))


def _clamped(lo, n):
    return lambda i: jnp.minimum(jnp.maximum(i - lo, 0), n - 1)


def _stream(tm, d, lo, n):
    blk = _clamped(lo, n)
    return pl.BlockSpec((tm, d), lambda i: (blk(i), 0))


def _row_pieces(w, rows):
    blk = _clamped(0, w.shape[0] // rows)
    return pl.BlockSpec((rows, w.shape[1]), lambda i: (blk(i), 0))


def _col_pieces(w, cols):
    blk = _clamped(0, w.shape[1] // cols)
    return pl.BlockSpec((w.shape[0], cols), lambda i: (0, blk(i)))


def _params():
    return pltpu.CompilerParams(dimension_semantics=("arbitrary",), vmem_limit_bytes=V7X_VMEM_LIMIT_BYTES)


def _ffn(x, n_prompt, n_sample, pre_g, post_g, wg, wu, wd, ple=None, *, name):
    tm = TOKEN_BLOCK
    d, d_ff = wg.shape
    n_prep = d_ff // WEIGHT_TILE
    n_blocks = n_prompt + n_sample
    nxt, prv = n_prep, n_prep + 1
    weights = [_resident(pre_g), _resident(post_g),
               _col_pieces(wg, WEIGHT_TILE), _col_pieces(wu, WEIGHT_TILE), _row_pieces(wd, WEIGHT_TILE)]
    scratch = [pltpu.VMEM((n_prep, d, 2 * WEIGHT_TILE), BF16),
               pltpu.VMEM((d_ff, d), BF16), pltpu.VMEM((tm, d), BF16), pltpu.VMEM((tm, d_ff), BF16)]
    if ple is None:
        xp, xs = x
        in_specs = [_stream(tm, d, nxt, n_prompt), _stream(tm, d, nxt + n_prompt, n_sample),
                    _stream(tm, d, prv, n_prompt), _stream(tm, d, prv + n_prompt, n_sample)] + weights
        args = [xp, xs, xp, xs, pre_g, post_g, wg, wu, wd]
        out_specs = _stream(tm, d, prv, n_blocks)
        out_shape = jax.ShapeDtypeStruct((n_blocks * tm, d), F32)
    else:
        pp, ps, wpg, wpp, ple_g = ple
        assert wpg.shape[0] // (WEIGHT_TILE // 2) <= n_prep
        d_p = pp.shape[1]

        def sample_first(lo):
            k = _clamped(lo, n_blocks)
            return pl.BlockSpec((tm, d), lambda i: (jnp.where(k(i) < n_sample, k(i) + n_prompt, k(i) - n_sample), 0))

        in_specs = [sample_first(nxt), sample_first(prv),
                    _stream(tm, d_p, prv + n_sample, n_prompt), _stream(tm, d_p, prv, n_sample)] + weights
        in_specs += [_row_pieces(wpg, WEIGHT_TILE // 2), _row_pieces(wpp, wpp.shape[0]), _resident(ple_g)]
        args = [x, x, pp, ps, pre_g, post_g, wg, wu, wd, wpg, wpp, ple_g]
        out_specs = [_stream(tm, d, prv + n_sample, n_prompt), _stream(tm, d, prv, n_sample)]
        out_shape = [jax.ShapeDtypeStruct((n_prompt * tm, d), F32), jax.ShapeDtypeStruct((n_sample * tm, d), F32)]
        scratch += [pltpu.VMEM(wpg.shape, BF16), pltpu.VMEM(wpp.shape, BF16), pltpu.VMEM((tm, d), BF16)]
    return pl.pallas_call(
        functools.partial(_ffn_kernel, n_prep=n_prep, n_prompt=n_prompt, n_blocks=n_blocks,
                          with_ple=ple is not None),
        grid=(n_prep + 1 + n_blocks,),
        in_specs=in_specs,
        out_specs=out_specs,
        out_shape=out_shape,
        scratch_shapes=scratch,
        compiler_params=_params(),
        name=name,
    )(*args)


def _mix(h, state, batch_p, seq_p, batch_s, seq_s, mw):
    tm = MIX_TOKEN_BLOCK
    d = h.shape[1]
    n_prep = MIX_SEGMENTS
    assert seq_p % tm == 0 and (batch_s * seq_s) % tm == 0 and tm % seq_s == 0
    n_prompt, n_sample = batch_p * seq_p // tm, batch_s * seq_s // tm
    n_blocks = n_prompt + n_sample
    blocks_per_seq = seq_p // tm
    seqs_per_block = tm // seq_s
    assert d // WEIGHT_TILE <= n_prep
    small = [mw["v_g"], mw["w_s"], mw["b_s"], mw["g_a"], mw["g_b"]]
    conv_w_spec = pl.BlockSpec((None,) + mw["conv_w"].shape[1:], lambda i: (0, 0, 0), pipeline_mode=pl.Buffered(1))
    lo_s = n_prep + n_prompt
    prompt_seq = _clamped(0, batch_p)
    per_prompt_seq = lambda i: (prompt_seq((i - n_prep) // blocks_per_seq), 0, 0)
    sample_blk = _clamped(lo_s, n_sample)
    per_sample_blk = pl.BlockSpec((seqs_per_block, CONV_K - 1, CONV_DIM), lambda i: (sample_blk(i), 0, 0))
    return pl.pallas_call(
        functools.partial(_mix_kernel, n_prep=n_prep, n_prompt=n_prompt, n_blocks=n_blocks,
                          blocks_per_seq=blocks_per_seq, seq_len=seq_s),
        grid=(n_prep + n_blocks + 1,),
        in_specs=[_stream(tm, d, n_prep, n_blocks), _stream(tm, d, n_prep + 1, n_blocks),
                  per_sample_blk, _resident(mw["pre"]), _resident(mw["post"]),
                  _col_pieces(mw["w_in"], CONV_DIM), _row_pieces(mw["w_out"], WEIGHT_TILE), conv_w_spec]
                 + [_resident(a) for a in small],
        out_specs=[_stream(tm, d, n_prep + 1, n_blocks),
                   pl.BlockSpec((1, CONV_K - 1, CONV_DIM), per_prompt_seq),
                   pl.BlockSpec((1, CHUNK_DIM, CHUNK), per_prompt_seq),
                   per_sample_blk,
                   pl.BlockSpec((seq_s, CHUNK_DIM, batch_s), lambda i: (0, 0, 0))],
        out_shape=[jax.ShapeDtypeStruct(h.shape, F32),
                   jax.ShapeDtypeStruct((batch_p, CONV_K - 1, CONV_DIM), F32),
                   jax.ShapeDtypeStruct((batch_p, CHUNK_DIM, CHUNK), F32),
                   jax.ShapeDtypeStruct((batch_s, CONV_K - 1, CONV_DIM), F32),
                   jax.ShapeDtypeStruct((seq_s, CHUNK_DIM, batch_s), F32)],
        scratch_shapes=[pltpu.VMEM((MIX_SEGMENTS, d, CONV_DIM), BF16), pltpu.VMEM(mw["w_out"].shape, BF16),
                        pltpu.VMEM((tm, d), BF16), pltpu.VMEM((tm, CONV_DIM + CHUNK_DIM), BF16),
                        pltpu.VMEM((CONV_K - 1, CONV_DIM), F32),
                        pltpu.VMEM((CHUNK_DIM // V7X_LANES, tm, V7X_LANES), F32),
                        pltpu.VMEM((seq_s, batch_s, CHUNK_DIM), F32),
                        pltpu.VMEM((V7X_MXU_COLS, V7X_MXU_COLS), BF16)],
        compiler_params=_params(),
        name="mix",
    )(h, h, state, mw["pre"], mw["post"], mw["w_in"], mw["w_out"], mw["conv_w"], *small)


def _row(g):
    return g.reshape(1, -1).astype(F32)


def _mix_weights(mix_pre_g, mix_post_g, w_in, conv_w, v_norm_g, w_s, b_s, out_g_a, out_g_b, w_out):
    return dict(pre=_row(mix_pre_g), post=_row(mix_post_g), w_in=w_in, w_out=w_out,
                conv_w=conv_w.astype(F32), v_g=v_norm_g.astype(F32), w_s=w_s.astype(F32),
                b_s=b_s.astype(F32), g_a=_row(out_g_a), g_b=_row(out_g_b))


def kernel(x_prompt, x_sample, p_prompt, p_sample, state_conv, ffn1_pre_g, ffn1_post_g, ffn1_w_gate, ffn1_w_up, ffn1_w_down, mix_pre_g, mix_post_g, w_in, conv_w, v_norm_g, w_s, b_s, out_g_a, out_g_b, w_out, ffn2_pre_g, ffn2_post_g, ffn2_w_gate, ffn2_w_up, ffn2_w_down, ple_w_gate, ple_w_proj, ple_post_g):
    depth = ffn1_pre_g.shape[0]
    bp, lp, d = x_prompt.shape
    bs, ls, _ = x_sample.shape
    tm = TOKEN_BLOCK
    assert lp % tm == 0 and (bs * ls) % tm == 0 and tm % ls == 0
    assert CONV_K - 1 <= ls <= V7X_SUBLANES and (lp - 1) % CHUNK + 1 == CHUNK
    n_prompt, n_sample = bp * lp // tm, bs * ls // tm
    x = (x_prompt.reshape(bp * lp, d), x_sample.reshape(bs * ls, d))
    conv_p, conv_s, v_p, v_s = [], [], [], []
    for i in range(depth):
        mw = _mix_weights(mix_pre_g[i], mix_post_g[i], w_in[i], conv_w[i:i + 1], v_norm_g[i], w_s[i], b_s[i],
                          out_g_a[i], out_g_b[i], w_out[i])
        ple = (p_prompt[i].reshape(bp * lp, -1), p_sample[i].reshape(bs * ls, -1),
               ple_w_gate[i], ple_w_proj[i], _row(ple_post_g[i]))

        h = _ffn(x, n_prompt, n_sample, _row(ffn1_pre_g[i]), _row(ffn1_post_g[i]),
                 ffn1_w_gate[i], ffn1_w_up[i], ffn1_w_down[i], name="ffn1")
        h, cp, vp, cs, vs = _mix(h, state_conv[i].astype(F32), bp, lp, bs, ls, mw)
        x = _ffn(h, n_prompt, n_sample, _row(ffn2_pre_g[i]), _row(ffn2_post_g[i]),
                 ffn2_w_gate[i], ffn2_w_up[i], ffn2_w_down[i], ple, name="ffn2_ple")

        conv_p.append(cp)
        conv_s.append(cs)
        v_p.append(vp.transpose(0, 2, 1).reshape(bp, CHUNK, HEADS_B, HEAD_DIM_B))
        v_s.append(vs.transpose(2, 0, 1).reshape(bs, ls, HEADS_B, HEAD_DIM_B))
    layers = lambda parts: parts[0][None] if depth == 1 else jnp.stack(parts)
    return (x[0].reshape(bp, lp, d), x[1].reshape(bs, ls, d),
            layers(conv_p), layers(conv_s), layers(v_p), layers(v_s))
```

```python
import functools

import jax
import jax.numpy as jnp
from jax import lax
from jax.experimental import pallas as pl
from jax.experimental.pallas import tpu as pltpu

CONV_K = 3
HEADS_B = 8
HEAD_DIM_B = 64
CHUNK_DIM = HEADS_B * HEAD_DIM_B
CONV_DIM = CHUNK_DIM
CHUNK = 128
EPS = 1e-6

V7X_SUBLANES = 8
V7X_LANES = 128
V7X_MXU_COLS = 256
V7X_VMEM_LIMIT_BYTES = 60 * 1024 * 1024

TOKEN_BLOCK = 512
MIX_TOKEN_BLOCK = 512
WEIGHT_TILE = V7X_MXU_COLS
HEADS_PER_SLAB = V7X_MXU_COLS // HEAD_DIM_B
MIX_SEGMENTS = 5
FINISH_SLICES = 8
FINISH_FIRST_CHUNK = 1
MIX_FINISH_SLICES = 4

BF16 = jnp.bfloat16
F32 = jnp.float32


def _rms(x, g):
    return x * lax.rsqrt(jnp.mean(x * x, axis=-1, keepdims=True) + EPS) * g


def _dot(a, b):
    return jnp.dot(a.astype(BF16), b, preferred_element_type=F32)


def _sigmoid(x):
    return 0.5 + 0.5 * jnp.tanh(0.5 * x)


def _silu(x):
    half = 0.5 * x
    return half + half * jnp.tanh(half)


def _after(value, anchor):
    bits = lax.bitcast_convert_type(anchor, jnp.int32)
    zero = lax.shift_right_logical(lax.shift_right_logical(bits, 16), 16)
    return value + zero.astype(value.dtype)


def _store_piece(dst_ref, src_ref, step, n_pieces):
    rows = src_ref.shape[0]

    @pl.when(step < n_pieces)
    def _():
        dst_ref[pl.ds(pl.multiple_of(step * rows, rows), rows), :] = src_ref[...].astype(BF16)


def _ffn_kernel(*refs, n_prep, n_prompt, n_blocks, with_ple):
    if with_ple:
        (xc_ref, xv_ref, pp_ref, ps_ref, pre_ref, post_ref, wg_ref, wu_ref, wd_ref, wpg_ref, wpp_ref, pleg_ref,
         o_ref, os_ref, wgu_s, wd_s, xn_s, a_s, wpg_s, wpp_s, hb_s) = refs
    else:
        (xpc_ref, xsc_ref, xpv_ref, xsv_ref, pre_ref, post_ref, wg_ref, wu_ref, wd_ref,
         o_ref, wgu_s, wd_s, xn_s, a_s) = refs
    i = pl.program_id(0)
    j = i - n_prep
    n_sample = n_blocks - n_prompt
    cur_is_sample = j + 1 >= n_prompt
    prev_is_sample = jnp.logical_and(j >= 0, j < n_sample) if with_ple else j >= n_prompt

    tm = xn_s.shape[0]
    n_slices = FINISH_SLICES // 2 if with_ple else FINISH_SLICES
    slice_rows = tm // n_slices
    stages = [("residual", s) for s in range(n_slices)]
    if with_ple:
        stages += [("embed_dots", 0)] + [("embed_add", s) for s in range(n_slices)]
    assert FINISH_FIRST_CHUNK + len(stages) <= n_prep

    def ordered(r, rows, anchor):
        return r[rows] if anchor is None else _after(r[rows], anchor[rows, 0:1])

    def x_rows(prompt_ref, sample_ref, is_sample, rows, prompt_only):
        if prompt_only:
            return prompt_ref[rows, :]
        return jnp.where(is_sample, sample_ref[rows, :], prompt_ref[rows, :])

    def finish_stage(stage, s, st, anchor=None, prompt_only=False):
        rows = slice(s * slice_rows, (s + 1) * slice_rows)
        if stage == "residual":
            x = xv_ref[rows, :] if with_ple else x_rows(xpv_ref, xsv_ref, prev_is_sample, rows, prompt_only)
            h = x + 0.5 * (st["y"][rows] * ordered(st["r"], rows, anchor) * post_ref[...])
            o_ref[rows, :] = h
            if with_ple:
                hb_s[rows, :] = h.astype(BF16)
        elif stage == "embed_dots":
            p = jnp.where(prev_is_sample, ps_ref[...], pp_ref[...])
            gate = _sigmoid(jnp.dot(hb_s[...], wpg_s[...], preferred_element_type=F32))
            st["e"] = gate * _dot(p, wpp_s[...])
            st["r2"] = lax.rsqrt(jnp.mean(st["e"] * st["e"], axis=-1, keepdims=True) + EPS)
        else:
            o_ref[rows, :] = o_ref[rows, :] + st["e"][rows] * ordered(st["r2"], rows, anchor) * pleg_ref[...]

    def down_prev():
        y = jnp.dot(a_s[...], wd_s[...], preferred_element_type=F32)
        return dict(y=y, r=lax.rsqrt(jnp.mean(y * y, axis=-1, keepdims=True) + EPS))

    def finish_prev():
        st = down_prev()
        for stage, s in stages:
            finish_stage(stage, s, st)

    def gate_up_chunk(c):
        gu = jnp.dot(xn_s[...], wgu_s[c], preferred_element_type=F32)
        g, u = gu[:, 0:WEIGHT_TILE], gu[:, WEIGHT_TILE:]
        return g, (_silu(g) * u).astype(BF16)

    def finish_prev_and_gate_up_cur(prompt_only=False):
        st = down_prev()
        x = xc_ref[...] if with_ple else x_rows(xpc_ref, xsc_ref, cur_is_sample, slice(None), prompt_only)
        x = _after(x, st["y"][:, 0:1])
        xn_s[...] = _rms(x, pre_ref[...]).astype(BF16)
        for c in range(n_prep):
            g, a_s[:, c * WEIGHT_TILE:(c + 1) * WEIGHT_TILE] = gate_up_chunk(c)
            k = c - FINISH_FIRST_CHUNK
            if 0 <= k < len(stages):
                finish_stage(*stages[k], st, anchor=g, prompt_only=prompt_only)

    @pl.when(i == 0)
    def _():
        xn_s[...] = _rms(xc_ref[...] if with_ple else xpc_ref[...], pre_ref[...]).astype(BF16)

    @pl.when(i < n_prep)
    def _():
        wgu_s[i, :, 0:WEIGHT_TILE] = wg_ref[...].astype(BF16)
        wgu_s[i, :, WEIGHT_TILE:] = wu_ref[...].astype(BF16)
        _, a = gate_up_chunk(i)
        for c in range(n_prep):
            @pl.when(i == c)
            def _():
                a_s[:, c * WEIGHT_TILE:(c + 1) * WEIGHT_TILE] = a

    _store_piece(wd_s, wd_ref, i, n_prep)
    if with_ple:
        _store_piece(wpg_s, wpg_ref, i, wpg_s.shape[0] // wpg_ref.shape[0])
        _store_piece(wpp_s, wpp_ref, i, 1)

    if with_ple:
        @pl.when(jnp.logical_and(j >= 0, j < n_blocks - 1))
        def _():
            finish_prev_and_gate_up_cur()
    else:
        @pl.when(jnp.logical_and(j >= 0, j < n_prompt - 1))
        def _():
            finish_prev_and_gate_up_cur(prompt_only=True)

        @pl.when(jnp.logical_and(j >= n_prompt - 1, j < n_blocks - 1))
        def _():
            finish_prev_and_gate_up_cur()

    @pl.when(j == n_blocks - 1)
    def _():
        finish_prev()

    if with_ple:
        @pl.when(prev_is_sample)
        def _():
            os_ref[...] = o_ref[...]


def _head_rms(v, ones_slab, g):
    vv = (v * v).astype(BF16)
    ms = jnp.concatenate(
        [jnp.dot(vv[:, k * V7X_MXU_COLS:(k + 1) * V7X_MXU_COLS], ones_slab, preferred_element_type=F32)
         for k in range(CHUNK_DIM // V7X_MXU_COLS)], axis=1)
    return v * lax.rsqrt(ms + EPS) * g


def _mix_kernel(hc_ref, hv_ref, state_ref, pre_ref, post_ref, win_ref, wout_ref, convw_ref, vg_ref,
                ws_ref, b_ref, ga_ref, gb_ref,
                o_ref, convp_ref, vp_ref, convs_ref, vs_ref,
                win_s, wout_s, n_s, yab_s, carry_ref, vn_s, vt_s, ones_ref,
                *, n_prep, n_prompt, n_blocks, blocks_per_seq, seq_len):
    tm = hc_ref.shape[0]
    i = pl.program_id(0)
    j = i - n_prep

    @pl.when(i < n_prep)
    def _():
        win_s[i] = win_ref[...].astype(BF16)

    _store_piece(wout_s, wout_ref, i, wout_s.shape[0] // wout_ref.shape[0])

    @pl.when(i == 0)
    def _():
        yab_s[...] = jnp.zeros_like(yab_s)
        carry_ref[...] = jnp.zeros_like(carry_ref)
        row_head = lax.broadcasted_iota(jnp.int32, ones_ref.shape, 0) // HEAD_DIM_B
        col_head = lax.broadcasted_iota(jnp.int32, ones_ref.shape, 1) // HEAD_DIM_B
        ones_ref[...] = jnp.where(row_head == col_head, 1.0 / HEAD_DIM_B, 0.0).astype(BF16)

    assert MIX_FINISH_SLICES <= MIX_SEGMENTS
    slice_rows = tm // MIX_FINISH_SLICES

    def out_proj_prev():
        y = jnp.dot(yab_s[...], wout_s[...], preferred_element_type=F32)
        return dict(y=y, r=lax.rsqrt(jnp.mean(y * y, axis=-1, keepdims=True) + EPS), done=0)

    def finish_slice(prev, anchor=None):
        rows = slice(prev["done"] * slice_rows, (prev["done"] + 1) * slice_rows)
        prev["done"] += 1
        r = prev["r"][rows] if anchor is None else _after(prev["r"][rows], anchor[rows, 0:1])
        o_ref[rows, :] = hv_ref[rows, :] + prev["y"][rows] * r * post_ref[...]

    def seg(k, prev):
        out = jnp.dot(n_s[...], win_s[k], preferred_element_type=F32)
        if prev["done"] < MIX_FINISH_SLICES:
            finish_slice(prev, anchor=out)
        return out

    def conv(z, z1, z2):
        return convw_ref[0:1, :] * z2 + convw_ref[1:2, :] * z1 + convw_ref[2:3, :] * z

    def per_head(piece, rows):
        return jnp.concatenate([jnp.broadcast_to(piece(h), (rows, HEAD_DIM_B)) for h in range(HEADS_B)], axis=1)

    def v_gain():
        return jnp.concatenate([vg_ref[h:h + 1, :] for h in range(HEADS_B)], axis=1)

    def chunk_bias(rows):
        b_t = b_ref[...].T
        return per_head(lambda h: b_t[0:rows, h:h + 1], rows)

    def group_norms(conv_out, mixed, prev):
        yab_s[:, 0:CONV_DIM] = _rms(seg(0, prev) * conv_out, ga_ref[...]).astype(BF16)
        yab_s[:, CONV_DIM:] = _rms(seg(3, prev) * mixed, gb_ref[...]).astype(BF16)

    def prompt_block():
        prev = out_proj_prev()
        n_s[...] = _rms(hc_ref[...], pre_ref[...]).astype(BF16)

        v_n = _head_rms(seg(4, prev), ones_ref[...], v_gain())
        vp_ref[0] = v_n[tm - CHUNK:, :].T
        t_idx = lax.broadcasted_iota(jnp.int32, (CHUNK, HEADS_PER_SLAB * CHUNK), 0)
        s_idx = lax.broadcasted_iota(jnp.int32, (CHUNK, HEADS_PER_SLAB * CHUNK), 1) % CHUNK
        causal = t_idx >= s_idx
        lane_head = lax.broadcasted_iota(jnp.int32, (CHUNK, V7X_MXU_COLS), 1) // HEAD_DIM_B
        n_slabs = CHUNK_DIM // V7X_MXU_COLS
        w_slab = [jnp.where(causal, jnp.concatenate([ws_ref[k * HEADS_PER_SLAB + hd] for hd in range(HEADS_PER_SLAB)],
                                                     axis=1), 0.0).astype(BF16) for k in range(n_slabs)]
        bias = chunk_bias(CHUNK)
        chunks = []
        for c in range(tm // CHUNK):
            slabs = []
            for k in range(n_slabs):
                cols = slice(k * V7X_MXU_COLS, (k + 1) * V7X_MXU_COLS)
                v_ck = v_n[c * CHUNK:(c + 1) * CHUNK, cols].astype(BF16)
                bd = jnp.concatenate(
                    [jnp.where(lane_head == hd, v_ck, jnp.zeros((), BF16)) for hd in range(HEADS_PER_SLAB)], axis=0)
                slabs.append(jnp.dot(w_slab[k], bd, preferred_element_type=F32))
            chunks.append(jnp.concatenate(slabs, axis=1) + bias)
        mixed = jnp.concatenate(chunks, axis=0)

        z = seg(1, prev) * seg(2, prev)
        row = lax.broadcasted_iota(jnp.int32, z.shape, 0)
        carry = jnp.where(j % blocks_per_seq == 0, 0.0, carry_ref[...])
        prev1 = carry[1:2, :]
        prev2 = carry[0:1, :]
        z1 = jnp.where(row == 0, prev1, pltpu.roll(z, 1, axis=0))
        z2 = jnp.where(row == 0, prev2, jnp.where(row == 1, prev1, pltpu.roll(z, 2, axis=0)))
        tail = z[tm - (CONV_K - 1):, :]
        carry_ref[...] = tail
        convp_ref[0] = tail
        group_norms(conv(z, z1, z2), mixed, prev)

    def sample_block():
        prev = out_proj_prev()
        n_s[...] = _rms(hc_ref[...], pre_ref[...]).astype(BF16)

        n_seq = tm // seq_len
        per_seq = lambda a: a.reshape(n_seq, seq_len, a.shape[-1])
        every_row = lambda a: jnp.broadcast_to(a, (n_seq, seq_len, a.shape[-1]))
        v_n = _head_rms(seg(4, prev), ones_ref[...], v_gain())
        first_seq = pl.multiple_of((j - n_prompt) * n_seq, n_seq)
        for c in range(CHUNK_DIM // V7X_LANES):
            lanes = slice(c * V7X_LANES, (c + 1) * V7X_LANES)
            vn_s[c] = v_n[:, lanes]
            for t in range(seq_len):
                vt_s[t, pl.ds(first_seq, n_seq), lanes] = vn_s[c, pl.ds(t, n_seq, stride=seq_len), :]

        @pl.when(j == n_blocks - 1)
        def _():
            for t in range(seq_len):
                vs_ref[t] = vt_s[t].T
        v_seq = per_seq(v_n)
        t_row = lax.broadcasted_iota(jnp.int32, (seq_len, CHUNK_DIM), 0)
        mixed = every_row(chunk_bias(seq_len)[None])
        for s in range(seq_len):
            w_from_s = per_head(lambda h: ws_ref[h, 0:seq_len, s:s + 1], seq_len)
            mixed = mixed + jnp.where(t_row >= s, w_from_s, 0.0)[None] * every_row(v_seq[:, s:s + 1, :])
        mixed = mixed.reshape(tm, CHUNK_DIM)

        z = seg(1, prev) * seg(2, prev)
        t = lax.broadcasted_iota(jnp.int32, z.shape, 0) % seq_len
        st = state_ref[...]
        prev1 = every_row(st[:, 1:2, :]).reshape(tm, CONV_DIM)
        prev2 = every_row(st[:, 0:1, :]).reshape(tm, CONV_DIM)
        z1 = jnp.where(t == 0, prev1, pltpu.roll(z, 1, axis=0))
        z2 = jnp.where(t == 0, prev2, jnp.where(t == 1, prev1, pltpu.roll(z, 2, axis=0)))
        convs_ref[...] = per_seq(z)[:, seq_len - (CONV_K - 1):, :]
        group_norms(conv(z, z1, z2), mixed, prev)

    @pl.when(jnp.logical_and(j >= 0, j < n_prompt))
    def _():
        prompt_block()

    @pl.when(jnp.logical_and(j >= n_prompt, j < n_blocks))
    def _():
        sample_block()

    @pl.when(j == n_blocks)
    def _():
        prev = out_proj_prev()
        for _ in range(MIX_FINISH_SLICES):
            finish_slice(prev)


def _resident(a):
    return pl.BlockSpec(a.shape, lambda i: (0,) * a.ndim, pipeline_mode=pl.Buffered(1))


def _clamped(lo, n):
    return lambda i: jnp.minimum(jnp.maximum(i - lo, 0), n - 1)


def _stream(tm, d, lo, n):
    blk = _clamped(lo, n)
    return pl.BlockSpec((tm, d), lambda i: (blk(i), 0))


def _row_pieces(w, rows):
    blk = _clamped(0, w.shape[0] // rows)
    return pl.BlockSpec((rows, w.shape[1]), lambda i: (blk(i), 0))


def _col_pieces(w, cols):
    blk = _clamped(0, w.shape[1] // cols)
    return pl.BlockSpec((w.shape[0], cols), lambda i: (0, blk(i)))


def _params():
    return pltpu.CompilerParams(dimension_semantics=("arbitrary",), vmem_limit_bytes=V7X_VMEM_LIMIT_BYTES)


def _ffn(x, n_prompt, n_sample, pre_g, post_g, wg, wu, wd, ple=None, *, name):
    tm = TOKEN_BLOCK
    d, d_ff = wg.shape
    n_prep = d_ff // WEIGHT_TILE
    n_blocks = n_prompt + n_sample
    nxt, prv = n_prep - 1, n_prep
    weights = [_resident(pre_g), _resident(post_g),
               _col_pieces(wg, WEIGHT_TILE), _col_pieces(wu, WEIGHT_TILE), _row_pieces(wd, WEIGHT_TILE)]
    scratch = [pltpu.VMEM((n_prep, d, 2 * WEIGHT_TILE), BF16),
               pltpu.VMEM((d_ff, d), BF16), pltpu.VMEM((tm, d), BF16), pltpu.VMEM((tm, d_ff), BF16)]
    if ple is None:
        xp, xs = x
        in_specs = [_stream(tm, d, nxt, n_prompt), _stream(tm, d, nxt + n_prompt, n_sample),
                    _stream(tm, d, prv, n_prompt), _stream(tm, d, prv + n_prompt, n_sample)] + weights
        args = [xp, xs, xp, xs, pre_g, post_g, wg, wu, wd]
        out_specs = _stream(tm, d, prv, n_blocks)
        out_shape = jax.ShapeDtypeStruct((n_blocks * tm, d), F32)
    else:
        pp, ps, wpg, wpp, ple_g = ple
        assert wpg.shape[0] // (WEIGHT_TILE // 2) <= n_prep
        d_p = pp.shape[1]

        def sample_first(lo):
            k = _clamped(lo, n_blocks)
            return pl.BlockSpec((tm, d), lambda i: (jnp.where(k(i) < n_sample, k(i) + n_prompt, k(i) - n_sample), 0))

        in_specs = [sample_first(nxt), sample_first(prv),
                    _stream(tm, d_p, prv + n_sample, n_prompt), _stream(tm, d_p, prv, n_sample)] + weights
        in_specs += [_row_pieces(wpg, WEIGHT_TILE // 2), _row_pieces(wpp, wpp.shape[0]), _resident(ple_g)]
        args = [x, x, pp, ps, pre_g, post_g, wg, wu, wd, wpg, wpp, ple_g]
        out_specs = [_stream(tm, d, prv + n_sample, n_prompt), _stream(tm, d, prv, n_sample)]
        out_shape = [jax.ShapeDtypeStruct((n_prompt * tm, d), F32), jax.ShapeDtypeStruct((n_sample * tm, d), F32)]
        scratch += [pltpu.VMEM(wpg.shape, BF16), pltpu.VMEM(wpp.shape, BF16), pltpu.VMEM((tm, d), BF16)]
    return pl.pallas_call(
        functools.partial(_ffn_kernel, n_prep=n_prep, n_prompt=n_prompt, n_blocks=n_blocks,
                          with_ple=ple is not None),
        grid=(n_prep + n_blocks,),
        in_specs=in_specs,
        out_specs=out_specs,
        out_shape=out_shape,
        scratch_shapes=scratch,
        compiler_params=_params(),
        name=name,
    )(*args)


def _mix(h, state, batch_p, seq_p, batch_s, seq_s, mw):
    tm = MIX_TOKEN_BLOCK
    d = h.shape[1]
    n_prep = MIX_SEGMENTS
    assert seq_p % tm == 0 and (batch_s * seq_s) % tm == 0 and tm % seq_s == 0
    n_prompt, n_sample = batch_p * seq_p // tm, batch_s * seq_s // tm
    n_blocks = n_prompt + n_sample
    blocks_per_seq = seq_p // tm
    seqs_per_block = tm // seq_s
    assert d // WEIGHT_TILE <= n_prep
    small = [mw["v_g"], mw["w_s"], mw["b_s"], mw["g_a"], mw["g_b"]]
    conv_w_spec = pl.BlockSpec((None,) + mw["conv_w"].shape[1:], lambda i: (0, 0, 0), pipeline_mode=pl.Buffered(1))
    lo_s = n_prep + n_prompt
    prompt_seq = _clamped(0, batch_p)
    per_prompt_seq = lambda i: (prompt_seq((i - n_prep) // blocks_per_seq), 0, 0)
    sample_blk = _clamped(lo_s, n_sample)
    per_sample_blk = pl.BlockSpec((seqs_per_block, CONV_K - 1, CONV_DIM), lambda i: (sample_blk(i), 0, 0))
    return pl.pallas_call(
        functools.partial(_mix_kernel, n_prep=n_prep, n_prompt=n_prompt, n_blocks=n_blocks,
                          blocks_per_seq=blocks_per_seq, seq_len=seq_s),
        grid=(n_prep + n_blocks + 1,),
        in_specs=[_stream(tm, d, n_prep, n_blocks), _stream(tm, d, n_prep + 1, n_blocks),
                  per_sample_blk, _resident(mw["pre"]), _resident(mw["post"]),
                  _col_pieces(mw["w_in"], CONV_DIM), _row_pieces(mw["w_out"], WEIGHT_TILE), conv_w_spec]
                 + [_resident(a) for a in small],
        out_specs=[_stream(tm, d, n_prep + 1, n_blocks),
                   pl.BlockSpec((1, CONV_K - 1, CONV_DIM), per_prompt_seq),
                   pl.BlockSpec((1, CHUNK_DIM, CHUNK), per_prompt_seq),
                   per_sample_blk,
                   pl.BlockSpec((seq_s, CHUNK_DIM, batch_s), lambda i: (0, 0, 0))],
        out_shape=[jax.ShapeDtypeStruct(h.shape, F32),
                   jax.ShapeDtypeStruct((batch_p, CONV_K - 1, CONV_DIM), F32),
                   jax.ShapeDtypeStruct((batch_p, CHUNK_DIM, CHUNK), F32),
                   jax.ShapeDtypeStruct((batch_s, CONV_K - 1, CONV_DIM), F32),
                   jax.ShapeDtypeStruct((seq_s, CHUNK_DIM, batch_s), F32)],
        scratch_shapes=[pltpu.VMEM((MIX_SEGMENTS, d, CONV_DIM), BF16), pltpu.VMEM(mw["w_out"].shape, BF16),
                        pltpu.VMEM((tm, d), BF16), pltpu.VMEM((tm, CONV_DIM + CHUNK_DIM), BF16),
                        pltpu.VMEM((CONV_K - 1, CONV_DIM), F32),
                        pltpu.VMEM((CHUNK_DIM // V7X_LANES, tm, V7X_LANES), F32),
                        pltpu.VMEM((seq_s, batch_s, CHUNK_DIM), F32),
                        pltpu.VMEM((V7X_MXU_COLS, V7X_MXU_COLS), BF16)],
        compiler_params=_params(),
        name="mix",
    )(h, h, state, mw["pre"], mw["post"], mw["w_in"], mw["w_out"], mw["conv_w"], *small)


def _row(g):
    return g.reshape(1, -1).astype(F32)


def _mix_weights(mix_pre_g, mix_post_g, w_in, conv_w, v_norm_g, w_s, b_s, out_g_a, out_g_b, w_out):
    return dict(pre=_row(mix_pre_g), post=_row(mix_post_g), w_in=w_in, w_out=w_out,
                conv_w=conv_w.astype(F32), v_g=v_norm_g.astype(F32), w_s=w_s.astype(F32),
                b_s=b_s.astype(F32), g_a=_row(out_g_a), g_b=_row(out_g_b))


def kernel(x_prompt, x_sample, p_prompt, p_sample, state_conv, ffn1_pre_g, ffn1_post_g, ffn1_w_gate, ffn1_w_up, ffn1_w_down, mix_pre_g, mix_post_g, w_in, conv_w, v_norm_g, w_s, b_s, out_g_a, out_g_b, w_out, ffn2_pre_g, ffn2_post_g, ffn2_w_gate, ffn2_w_up, ffn2_w_down, ple_w_gate, ple_w_proj, ple_post_g):
    depth = ffn1_pre_g.shape[0]
    bp, lp, d = x_prompt.shape
    bs, ls, _ = x_sample.shape
    tm = TOKEN_BLOCK
    assert lp % tm == 0 and (bs * ls) % tm == 0 and tm % ls == 0
    assert CONV_K - 1 <= ls <= V7X_SUBLANES and (lp - 1) % CHUNK + 1 == CHUNK
    n_prompt, n_sample = bp * lp // tm, bs * ls // tm
    x = (x_prompt.reshape(bp * lp, d), x_sample.reshape(bs * ls, d))
    conv_p, conv_s, v_p, v_s = [], [], [], []
    for i in range(depth):
        mw = _mix_weights(mix_pre_g[i], mix_post_g[i], w_in[i], conv_w[i:i + 1], v_norm_g[i], w_s[i], b_s[i],
                          out_g_a[i], out_g_b[i], w_out[i])
        ple = (p_prompt[i].reshape(bp * lp, -1), p_sample[i].reshape(bs * ls, -1),
               ple_w_gate[i], ple_w_proj[i], _row(ple_post_g[i]))

        h = _ffn(x, n_prompt, n_sample, _row(ffn1_pre_g[i]), _row(ffn1_post_g[i]),
                 ffn1_w_gate[i], ffn1_w_up[i], ffn1_w_down[i], name="ffn1")
        h, cp, vp, cs, vs = _mix(h, state_conv[i].astype(F32), bp, lp, bs, ls, mw)
        x = _ffn(h, n_prompt, n_sample, _row(ffn2_pre_g[i]), _row(ffn2_post_g[i]),
                 ffn2_w_gate[i], ffn2_w_up[i], ffn2_w_down[i], ple, name="ffn2_ple")

        conv_p.append(cp)
        conv_s.append(cs)
        v_p.append(vp.transpose(0, 2, 1).reshape(bp, CHUNK, HEADS_B, HEAD_DIM_B))
        v_s.append(vs.transpose(2, 0, 1).reshape(bs, ls, HEADS_B, HEAD_DIM_B))
    layers = lambda parts: parts[0][None] if depth == 1 else jnp.stack(parts)
    return (x[0].reshape(bp, lp, d), x[1].reshape(bs, ls, d),
            layers(conv_p), layers(conv_s), layers(v_p), layers(v_s))
```

```python
import functools

import jax
import jax.numpy as jnp
from jax import lax
from jax.experimental import pallas as pl
from jax.experimental.pallas import tpu as pltpu

CONV_K = 3
HEADS_B = 8
HEAD_DIM_B = 64
CHUNK_DIM = HEADS_B * HEAD_DIM_B
CONV_DIM = CHUNK_DIM
CHUNK = 128
EPS = 1e-6

V7X_SUBLANES = 8
V7X_LANES = 128
V7X_MXU_COLS = 256
V7X_VMEM_LIMIT_BYTES = 60 * 1024 * 1024

FFN1_TOKEN_BLOCK = 1024
FFN2_TOKEN_BLOCK = 512
MIX_TOKEN_BLOCK = 512
WEIGHT_TILE = V7X_MXU_COLS
HEADS_PER_SLAB = V7X_MXU_COLS // HEAD_DIM_B
MIX_SEGMENTS = 5
FINISH_SLICES = 8
FINISH_FIRST_CHUNK = 1
MIX_FINISH_SLICES = 4

BF16 = jnp.bfloat16
F32 = jnp.float32


def _rms(x, g):
    return x * lax.rsqrt(jnp.mean(x * x, axis=-1, keepdims=True) + EPS) * g


def _dot(a, b):
    return jnp.dot(a.astype(BF16), b, preferred_element_type=F32)


def _sigmoid(x):
    return 0.5 + 0.5 * jnp.tanh(0.5 * x)


def _silu(x):
    half = 0.5 * x
    return half + half * jnp.tanh(half)


def _after(value, anchor):
    bits = lax.bitcast_convert_type(anchor, jnp.int32)
    zero = lax.shift_right_logical(lax.shift_right_logical(bits, 16), 16)
    return value + zero.astype(value.dtype)


def _store_piece(dst_ref, src_ref, step, n_pieces):
    rows = src_ref.shape[0]

    @pl.when(step < n_pieces)
    def _():
        dst_ref[pl.ds(pl.multiple_of(step * rows, rows), rows), :] = src_ref[...].astype(BF16)


def _ffn_kernel(*refs, n_prep, n_prompt, n_blocks, with_ple):
    if with_ple:
        (xc_ref, pp_ref, ps_ref, pre_ref, post_ref, wg_ref, wu_ref, wd_ref, wpg_ref, wpp_ref, pleg_ref,
         o_ref, os_ref, wgu_s, wd_s, xn_s, a_s, xk_s, wpg_s, wpp_s, hb_s) = refs
    else:
        (xpc_ref, xsc_ref, pre_ref, post_ref, wg_ref, wu_ref, wd_ref,
         o_ref, wgu_s, wd_s, xn_s, a_s, xk_s) = refs
    i = pl.program_id(0)
    j = i - n_prep
    n_sample = n_blocks - n_prompt
    cur_is_sample = j + 1 >= n_prompt
    prev_is_sample = jnp.logical_and(j >= 0, j < n_sample) if with_ple else j >= n_prompt

    tm = xn_s.shape[0]
    n_slices = FINISH_SLICES // 2 if with_ple else FINISH_SLICES
    slice_rows = tm // n_slices
    stages = [("residual", s) for s in range(n_slices)]
    if with_ple:
        stages += [("embed_dots", 0)] + [("embed_add", s) for s in range(n_slices)]
    assert FINISH_FIRST_CHUNK + len(stages) <= n_prep

    def ordered(r, rows, anchor):
        return r[rows] if anchor is None else _after(r[rows], anchor[rows, 0:1])

    def cur_x(prompt_only):
        if with_ple:
            return xc_ref[...]
        if prompt_only:
            return xpc_ref[...]
        return jnp.where(cur_is_sample, xsc_ref[...], xpc_ref[...])

    def finish_stage(stage, s, st, anchor=None):
        rows = slice(s * slice_rows, (s + 1) * slice_rows)
        if stage == "residual":
            h = xk_s[rows, :] + 0.5 * (st["y"][rows] * ordered(st["r"], rows, anchor) * post_ref[...])
            o_ref[rows, :] = h
            if with_ple:
                hb_s[rows, :] = h.astype(BF16)
        elif stage == "embed_dots":
            p = jnp.where(prev_is_sample, ps_ref[...], pp_ref[...])
            gate = _sigmoid(jnp.dot(hb_s[...], wpg_s[...], preferred_element_type=F32))
            st["e"] = gate * _dot(p, wpp_s[...])
            st["r2"] = lax.rsqrt(jnp.mean(st["e"] * st["e"], axis=-1, keepdims=True) + EPS)
        else:
            o_ref[rows, :] = o_ref[rows, :] + st["e"][rows] * ordered(st["r2"], rows, anchor) * pleg_ref[...]

    def down_prev():
        y = jnp.dot(a_s[...], wd_s[...], preferred_element_type=F32)
        return dict(y=y, r=lax.rsqrt(jnp.mean(y * y, axis=-1, keepdims=True) + EPS))

    def finish_prev():
        st = down_prev()
        for stage, s in stages:
            finish_stage(stage, s, st)

    def gate_up_chunk(c):
        gu = jnp.dot(xn_s[...], wgu_s[c], preferred_element_type=F32)
        g, u = gu[:, 0:WEIGHT_TILE], gu[:, WEIGHT_TILE:]
        return g, (_silu(g) * u).astype(BF16)

    def finish_prev_and_gate_up_cur(prompt_only=False):
        st = down_prev()
        x = _after(cur_x(prompt_only), st["y"][:, 0:1])
        xn_s[...] = _rms(x, pre_ref[...]).astype(BF16)
        for c in range(n_prep):
            g, a_s[:, c * WEIGHT_TILE:(c + 1) * WEIGHT_TILE] = gate_up_chunk(c)
            k = c - FINISH_FIRST_CHUNK
            if 0 <= k < len(stages):
                finish_stage(*stages[k], st, anchor=g)
        xk_s[...] = cur_x(prompt_only)

    @pl.when(i == 0)
    def _():
        xk_s[...] = cur_x(prompt_only=True)
        xn_s[...] = _rms(xk_s[...], pre_ref[...]).astype(BF16)

    @pl.when(i < n_prep)
    def _():
        wgu_s[i, :, 0:WEIGHT_TILE] = wg_ref[...].astype(BF16)
        wgu_s[i, :, WEIGHT_TILE:] = wu_ref[...].astype(BF16)
        _, a = gate_up_chunk(i)
        for c in range(n_prep):
            @pl.when(i == c)
            def _():
                a_s[:, c * WEIGHT_TILE:(c + 1) * WEIGHT_TILE] = a

    _store_piece(wd_s, wd_ref, i, n_prep)
    if with_ple:
        _store_piece(wpg_s, wpg_ref, i, wpg_s.shape[0] // wpg_ref.shape[0])
        _store_piece(wpp_s, wpp_ref, i, 1)

    if with_ple:
        @pl.when(jnp.logical_and(j >= 0, j < n_blocks - 1))
        def _():
            finish_prev_and_gate_up_cur()
    else:
        @pl.when(jnp.logical_and(j >= 0, j < n_prompt - 1))
        def _():
            finish_prev_and_gate_up_cur(prompt_only=True)

        @pl.when(jnp.logical_and(j >= n_prompt - 1, j < n_blocks - 1))
        def _():
            finish_prev_and_gate_up_cur()

    @pl.when(j == n_blocks - 1)
    def _():
        finish_prev()

    if with_ple:
        @pl.when(prev_is_sample)
        def _():
            os_ref[...] = o_ref[...]


def _head_rms(v, ones_slab, g):
    vv = (v * v).astype(BF16)
    ms = jnp.concatenate(
        [jnp.dot(vv[:, k * V7X_MXU_COLS:(k + 1) * V7X_MXU_COLS], ones_slab, preferred_element_type=F32)
         for k in range(CHUNK_DIM // V7X_MXU_COLS)], axis=1)
    return v * lax.rsqrt(ms + EPS) * g


def _mix_kernel(hc_ref, hv_ref, state_ref, pre_ref, post_ref, win_ref, wout_ref, convw_ref, vg_ref,
                ws_ref, b_ref, ga_ref, gb_ref,
                o_ref, convp_ref, vp_ref, convs_ref, vs_ref,
                win_s, wout_s, n_s, yab_s, carry_ref, vn_s, vt_s, ones_ref,
                *, n_prep, n_prompt, n_blocks, blocks_per_seq, seq_len):
    tm = hc_ref.shape[0]
    i = pl.program_id(0)
    j = i - n_prep

    @pl.when(i < n_prep)
    def _():
        win_s[i] = win_ref[...].astype(BF16)

    _store_piece(wout_s, wout_ref, i, wout_s.shape[0] // wout_ref.shape[0])

    @pl.when(i == 0)
    def _():
        yab_s[...] = jnp.zeros_like(yab_s)
        carry_ref[...] = jnp.zeros_like(carry_ref)
        row_head = lax.broadcasted_iota(jnp.int32, ones_ref.shape, 0) // HEAD_DIM_B
        col_head = lax.broadcasted_iota(jnp.int32, ones_ref.shape, 1) // HEAD_DIM_B
        ones_ref[...] = jnp.where(row_head == col_head, 1.0 / HEAD_DIM_B, 0.0).astype(BF16)

    assert MIX_FINISH_SLICES <= MIX_SEGMENTS
    slice_rows = tm // MIX_FINISH_SLICES

    def out_proj_prev():
        y = jnp.dot(yab_s[...], wout_s[...], preferred_element_type=F32)
        return dict(y=y, r=lax.rsqrt(jnp.mean(y * y, axis=-1, keepdims=True) + EPS), done=0)

    def finish_slice(prev, anchor=None):
        rows = slice(prev["done"] * slice_rows, (prev["done"] + 1) * slice_rows)
        prev["done"] += 1
        r = prev["r"][rows] if anchor is None else _after(prev["r"][rows], anchor[rows, 0:1])
        o_ref[rows, :] = hv_ref[rows, :] + prev["y"][rows] * r * post_ref[...]

    def seg(k, prev):
        out = jnp.dot(n_s[...], win_s[k], preferred_element_type=F32)
        if prev["done"] < MIX_FINISH_SLICES:
            finish_slice(prev, anchor=out)
        return out

    def conv(z, z1, z2):
        return convw_ref[0:1, :] * z2 + convw_ref[1:2, :] * z1 + convw_ref[2:3, :] * z

    def per_head(piece, rows):
        return jnp.concatenate([jnp.broadcast_to(piece(h), (rows, HEAD_DIM_B)) for h in range(HEADS_B)], axis=1)

    def v_gain():
        return jnp.concatenate([vg_ref[h:h + 1, :] for h in range(HEADS_B)], axis=1)

    def chunk_bias(rows):
        b_t = b_ref[...].T
        return per_head(lambda h: b_t[0:rows, h:h + 1], rows)

    def group_norms(conv_out, mixed, prev):
        yab_s[:, 0:CONV_DIM] = _rms(seg(0, prev) * conv_out, ga_ref[...]).astype(BF16)
        yab_s[:, CONV_DIM:] = _rms(seg(3, prev) * mixed, gb_ref[...]).astype(BF16)

    def prompt_block():
        prev = out_proj_prev()
        n_s[...] = _rms(hc_ref[...], pre_ref[...]).astype(BF16)

        v_n = _head_rms(seg(4, prev), ones_ref[...], v_gain())
        vp_ref[0] = v_n[tm - CHUNK:, :].T
        t_idx = lax.broadcasted_iota(jnp.int32, (CHUNK, HEADS_PER_SLAB * CHUNK), 0)
        s_idx = lax.broadcasted_iota(jnp.int32, (CHUNK, HEADS_PER_SLAB * CHUNK), 1) % CHUNK
        causal = t_idx >= s_idx
        lane_head = lax.broadcasted_iota(jnp.int32, (CHUNK, V7X_MXU_COLS), 1) // HEAD_DIM_B
        n_slabs = CHUNK_DIM // V7X_MXU_COLS
        w_slab = [jnp.where(causal, jnp.concatenate([ws_ref[k * HEADS_PER_SLAB + hd] for hd in range(HEADS_PER_SLAB)],
                                                     axis=1), 0.0).astype(BF16) for k in range(n_slabs)]
        bias = chunk_bias(CHUNK)
        chunks = []
        for c in range(tm // CHUNK):
            slabs = []
            for k in range(n_slabs):
                cols = slice(k * V7X_MXU_COLS, (k + 1) * V7X_MXU_COLS)
                v_ck = v_n[c * CHUNK:(c + 1) * CHUNK, cols].astype(BF16)
                bd = jnp.concatenate(
                    [jnp.where(lane_head == hd, v_ck, jnp.zeros((), BF16)) for hd in range(HEADS_PER_SLAB)], axis=0)
                slabs.append(jnp.dot(w_slab[k], bd, preferred_element_type=F32))
            chunks.append(jnp.concatenate(slabs, axis=1) + bias)
        mixed = jnp.concatenate(chunks, axis=0)

        z = seg(1, prev) * seg(2, prev)
        row = lax.broadcasted_iota(jnp.int32, z.shape, 0)
        carry = jnp.where(j % blocks_per_seq == 0, 0.0, carry_ref[...])
        prev1 = carry[1:2, :]
        prev2 = carry[0:1, :]
        z1 = jnp.where(row == 0, prev1, pltpu.roll(z, 1, axis=0))
        z2 = jnp.where(row == 0, prev2, jnp.where(row == 1, prev1, pltpu.roll(z, 2, axis=0)))
        tail = z[tm - (CONV_K - 1):, :]
        carry_ref[...] = tail
        convp_ref[0] = tail
        group_norms(conv(z, z1, z2), mixed, prev)

    def sample_block():
        prev = out_proj_prev()
        n_s[...] = _rms(hc_ref[...], pre_ref[...]).astype(BF16)

        n_seq = tm // seq_len
        per_seq = lambda a: a.reshape(n_seq, seq_len, a.shape[-1])
        every_row = lambda a: jnp.broadcast_to(a, (n_seq, seq_len, a.shape[-1]))
        v_n = _head_rms(seg(4, prev), ones_ref[...], v_gain())
        first_seq = pl.multiple_of((j - n_prompt) * n_seq, n_seq)
        for c in range(CHUNK_DIM // V7X_LANES):
            lanes = slice(c * V7X_LANES, (c + 1) * V7X_LANES)
            vn_s[c] = v_n[:, lanes]
            for t in range(seq_len):
                vt_s[t, pl.ds(first_seq, n_seq), lanes] = vn_s[c, pl.ds(t, n_seq, stride=seq_len), :]

        @pl.when(j == n_blocks - 1)
        def _():
            for t in range(seq_len):
                vs_ref[t] = vt_s[t].T
        v_seq = per_seq(v_n)
        t_row = lax.broadcasted_iota(jnp.int32, (seq_len, CHUNK_DIM), 0)
        mixed = every_row(chunk_bias(seq_len)[None])
        for s in range(seq_len):
            w_from_s = per_head(lambda h: ws_ref[h, 0:seq_len, s:s + 1], seq_len)
            mixed = mixed + jnp.where(t_row >= s, w_from_s, 0.0)[None] * every_row(v_seq[:, s:s + 1, :])
        mixed = mixed.reshape(tm, CHUNK_DIM)

        z = seg(1, prev) * seg(2, prev)
        t = lax.broadcasted_iota(jnp.int32, z.shape, 0) % seq_len
        st = state_ref[...]
        prev1 = every_row(st[:, 1:2, :]).reshape(tm, CONV_DIM)
        prev2 = every_row(st[:, 0:1, :]).reshape(tm, CONV_DIM)
        z1 = jnp.where(t == 0, prev1, pltpu.roll(z, 1, axis=0))
        z2 = jnp.where(t == 0, prev2, jnp.where(t == 1, prev1, pltpu.roll(z, 2, axis=0)))
        convs_ref[...] = per_seq(z)[:, seq_len - (CONV_K - 1):, :]
        group_norms(conv(z, z1, z2), mixed, prev)

    @pl.when(jnp.logical_and(j >= 0, j < n_prompt))
    def _():
        prompt_block()

    @pl.when(jnp.logical_and(j >= n_prompt, j < n_blocks))
    def _():
        sample_block()

    @pl.when(j == n_blocks)
    def _():
        prev = out_proj_prev()
        for _ in range(MIX_FINISH_SLICES):
            finish_slice(prev)


def _resident(a):
    return pl.BlockSpec(a.shape, lambda i: (0,) * a.ndim, pipeline_mode=pl.Buffered(1))


def _clamped(lo, n):
    return lambda i: jnp.minimum(jnp.maximum(i - lo, 0), n - 1)


def _buffering(n_buffers):
    return {} if n_buffers is None else dict(pipeline_mode=pl.Buffered(n_buffers))


def _stream(tm, d, lo, n, n_buffers=None):
    blk = _clamped(lo, n)
    return pl.BlockSpec((tm, d), lambda i: (blk(i), 0), **_buffering(n_buffers))


def _row_pieces(w, rows, n_buffers=None):
    blk = _clamped(0, w.shape[0] // rows)
    return pl.BlockSpec((rows, w.shape[1]), lambda i: (blk(i), 0), **_buffering(n_buffers))


def _col_pieces(w, cols, n_buffers=None):
    blk = _clamped(0, w.shape[1] // cols)
    return pl.BlockSpec((w.shape[0], cols), lambda i: (0, blk(i)), **_buffering(n_buffers))


def _params():
    return pltpu.CompilerParams(dimension_semantics=("arbitrary",), vmem_limit_bytes=V7X_VMEM_LIMIT_BYTES)


def _ffn(x, rows_prompt, rows_sample, pre_g, post_g, wg, wu, wd, ple=None, *, tm, name):
    assert rows_prompt % tm == 0 and rows_sample % tm == 0
    n_prompt, n_sample = rows_prompt // tm, rows_sample // tm
    d, d_ff = wg.shape
    n_prep = d_ff // WEIGHT_TILE
    n_blocks = n_prompt + n_sample
    nxt, prv = n_prep - 1, n_prep
    piece_buffers = 1 if tm > FFN2_TOKEN_BLOCK else None
    weights = [_resident(pre_g), _resident(post_g),
               _col_pieces(wg, WEIGHT_TILE, piece_buffers), _col_pieces(wu, WEIGHT_TILE, piece_buffers),
               _row_pieces(wd, WEIGHT_TILE, piece_buffers)]
    scratch = [pltpu.VMEM((n_prep, d, 2 * WEIGHT_TILE), BF16),
               pltpu.VMEM((d_ff, d), BF16), pltpu.VMEM((tm, d), BF16), pltpu.VMEM((tm, d_ff), BF16),
               pltpu.VMEM((tm, d), F32)]
    if ple is None:
        xp, xs = x
        in_specs = [_stream(tm, d, nxt, n_prompt),
                    _stream(tm, d, nxt + n_prompt, n_sample, 1 if n_sample == 1 else None)] + weights
        args = [xp, xs, pre_g, post_g, wg, wu, wd]
        out_specs = _stream(tm, d, prv, n_blocks)
        out_shape = jax.ShapeDtypeStruct((n_blocks * tm, d), F32)
    else:
        pp, ps, wpg, wpp, ple_g = ple
        assert wpg.shape[0] // (WEIGHT_TILE // 2) <= n_prep
        d_p = pp.shape[1]

        def sample_first(lo):
            k = _clamped(lo, n_blocks)
            return pl.BlockSpec((tm, d), lambda i: (jnp.where(k(i) < n_sample, k(i) + n_prompt, k(i) - n_sample), 0))

        in_specs = [sample_first(nxt),
                    _stream(tm, d_p, prv + n_sample, n_prompt), _stream(tm, d_p, prv, n_sample)] + weights
        in_specs += [_row_pieces(wpg, WEIGHT_TILE // 2), _row_pieces(wpp, wpp.shape[0]), _resident(ple_g)]
        args = [x, pp, ps, pre_g, post_g, wg, wu, wd, wpg, wpp, ple_g]
        out_specs = [_stream(tm, d, prv + n_sample, n_prompt), _stream(tm, d, prv, n_sample)]
        out_shape = [jax.ShapeDtypeStruct((n_prompt * tm, d), F32), jax.ShapeDtypeStruct((n_sample * tm, d), F32)]
        scratch += [pltpu.VMEM(wpg.shape, BF16), pltpu.VMEM(wpp.shape, BF16), pltpu.VMEM((tm, d), BF16)]
    return pl.pallas_call(
        functools.partial(_ffn_kernel, n_prep=n_prep, n_prompt=n_prompt, n_blocks=n_blocks,
                          with_ple=ple is not None),
        grid=(n_prep + n_blocks,),
        in_specs=in_specs,
        out_specs=out_specs,
        out_shape=out_shape,
        scratch_shapes=scratch,
        compiler_params=_params(),
        name=name,
    )(*args)


def _mix(h, state, batch_p, seq_p, batch_s, seq_s, mw):
    tm = MIX_TOKEN_BLOCK
    d = h.shape[1]
    n_prep = MIX_SEGMENTS
    assert seq_p % tm == 0 and (batch_s * seq_s) % tm == 0 and tm % seq_s == 0
    n_prompt, n_sample = batch_p * seq_p // tm, batch_s * seq_s // tm
    n_blocks = n_prompt + n_sample
    blocks_per_seq = seq_p // tm
    seqs_per_block = tm // seq_s
    assert d // WEIGHT_TILE <= n_prep
    small = [mw["v_g"], mw["w_s"], mw["b_s"], mw["g_a"], mw["g_b"]]
    conv_w_spec = pl.BlockSpec((None,) + mw["conv_w"].shape[1:], lambda i: (0, 0, 0), pipeline_mode=pl.Buffered(1))
    lo_s = n_prep + n_prompt
    prompt_seq = _clamped(0, batch_p)
    per_prompt_seq = lambda i: (prompt_seq((i - n_prep) // blocks_per_seq), 0, 0)
    sample_blk = _clamped(lo_s, n_sample)
    per_sample_blk = pl.BlockSpec((seqs_per_block, CONV_K - 1, CONV_DIM), lambda i: (sample_blk(i), 0, 0))
    return pl.pallas_call(
        functools.partial(_mix_kernel, n_prep=n_prep, n_prompt=n_prompt, n_blocks=n_blocks,
                          blocks_per_seq=blocks_per_seq, seq_len=seq_s),
        grid=(n_prep + n_blocks + 1,),
        in_specs=[_stream(tm, d, n_prep, n_blocks), _stream(tm, d, n_prep + 1, n_blocks),
                  per_sample_blk, _resident(mw["pre"]), _resident(mw["post"]),
                  _col_pieces(mw["w_in"], CONV_DIM), _row_pieces(mw["w_out"], WEIGHT_TILE), conv_w_spec]
                 + [_resident(a) for a in small],
        out_specs=[_stream(tm, d, n_prep + 1, n_blocks),
                   pl.BlockSpec((1, CONV_K - 1, CONV_DIM), per_prompt_seq),
                   pl.BlockSpec((1, CHUNK_DIM, CHUNK), per_prompt_seq),
                   per_sample_blk,
                   pl.BlockSpec((seq_s, CHUNK_DIM, batch_s), lambda i: (0, 0, 0))],
        out_shape=[jax.ShapeDtypeStruct(h.shape, F32),
                   jax.ShapeDtypeStruct((batch_p, CONV_K - 1, CONV_DIM), F32),
                   jax.ShapeDtypeStruct((batch_p, CHUNK_DIM, CHUNK), F32),
                   jax.ShapeDtypeStruct((batch_s, CONV_K - 1, CONV_DIM), F32),
                   jax.ShapeDtypeStruct((seq_s, CHUNK_DIM, batch_s), F32)],
        scratch_shapes=[pltpu.VMEM((MIX_SEGMENTS, d, CONV_DIM), BF16), pltpu.VMEM(mw["w_out"].shape, BF16),
                        pltpu.VMEM((tm, d), BF16), pltpu.VMEM((tm, CONV_DIM + CHUNK_DIM), BF16),
                        pltpu.VMEM((CONV_K - 1, CONV_DIM), F32),
                        pltpu.VMEM((CHUNK_DIM // V7X_LANES, tm, V7X_LANES), F32),
                        pltpu.VMEM((seq_s, batch_s, CHUNK_DIM), F32),
                        pltpu.VMEM((V7X_MXU_COLS, V7X_MXU_COLS), BF16)],
        compiler_params=_params(),
        name="mix",
    )(h, h, state, mw["pre"], mw["post"], mw["w_in"], mw["w_out"], mw["conv_w"], *small)


def _row(g):
    return g.reshape(1, -1).astype(F32)


def _mix_weights(mix_pre_g, mix_post_g, w_in, conv_w, v_norm_g, w_s, b_s, out_g_a, out_g_b, w_out):
    return dict(pre=_row(mix_pre_g), post=_row(mix_post_g), w_in=w_in, w_out=w_out,
                conv_w=conv_w.astype(F32), v_g=v_norm_g.astype(F32), w_s=w_s.astype(F32),
                b_s=b_s.astype(F32), g_a=_row(out_g_a), g_b=_row(out_g_b))


def kernel(x_prompt, x_sample, p_prompt, p_sample, state_conv, ffn1_pre_g, ffn1_post_g, ffn1_w_gate, ffn1_w_up, ffn1_w_down, mix_pre_g, mix_post_g, w_in, conv_w, v_norm_g, w_s, b_s, out_g_a, out_g_b, w_out, ffn2_pre_g, ffn2_post_g, ffn2_w_gate, ffn2_w_up, ffn2_w_down, ple_w_gate, ple_w_proj, ple_post_g):
    depth = ffn1_pre_g.shape[0]
    bp, lp, d = x_prompt.shape
    bs, ls, _ = x_sample.shape
    assert CONV_K - 1 <= ls <= V7X_SUBLANES and (lp - 1) % CHUNK + 1 == CHUNK
    rows_p, rows_s = bp * lp, bs * ls
    x = (x_prompt.reshape(rows_p, d), x_sample.reshape(rows_s, d))
    conv_p, conv_s, v_p, v_s = [], [], [], []
    for i in range(depth):
        mw = _mix_weights(mix_pre_g[i], mix_post_g[i], w_in[i], conv_w[i:i + 1], v_norm_g[i], w_s[i], b_s[i],
                          out_g_a[i], out_g_b[i], w_out[i])
        ple = (p_prompt[i].reshape(bp * lp, -1), p_sample[i].reshape(bs * ls, -1),
               ple_w_gate[i], ple_w_proj[i], _row(ple_post_g[i]))

        h = _ffn(x, rows_p, rows_s, _row(ffn1_pre_g[i]), _row(ffn1_post_g[i]),
                 ffn1_w_gate[i], ffn1_w_up[i], ffn1_w_down[i], tm=FFN1_TOKEN_BLOCK, name="ffn1")
        h, cp, vp, cs, vs = _mix(h, state_conv[i].astype(F32), bp, lp, bs, ls, mw)
        x = _ffn(h, rows_p, rows_s, _row(ffn2_pre_g[i]), _row(ffn2_post_g[i]),
                 ffn2_w_gate[i], ffn2_w_up[i], ffn2_w_down[i], ple, tm=FFN2_TOKEN_BLOCK, name="ffn2_ple")

        conv_p.append(cp)
        conv_s.append(cs)
        v_p.append(vp.transpose(0, 2, 1).reshape(bp, CHUNK, HEADS_B, HEAD_DIM_B))
        v_s.append(vs.transpose(2, 0, 1).reshape(bs, ls, HEADS_B, HEAD_DIM_B))
    layers = lambda parts: parts[0][None] if depth == 1 else jnp.stack(parts)
    return (x[0].reshape(bp, lp, d), x[1].reshape(bs, ls, d),
            layers(conv_p), layers(conv_s), layers(v_p), layers(v_s))
```

```python
import functools

import jax
import jax.numpy as jnp
from jax import lax
from jax.experimental import pallas as pl
from jax.experimental.pallas import tpu as pltpu

CONV_K = 3
HEADS_B = 8
HEAD_DIM_B = 64
CHUNK_DIM = HEADS_B * HEAD_DIM_B
CONV_DIM = CHUNK_DIM
CHUNK = 128
EPS = 1e-6

V7X_SUBLANES = 8
V7X_LANES = 128
V7X_MXU_COLS = 256
V7X_VMEM_LIMIT_BYTES = 60 * 1024 * 1024

TOKEN_BLOCK = 512
MIX_TOKEN_BLOCK = 512
WEIGHT_TILE = V7X_MXU_COLS
HEADS_PER_SLAB = V7X_MXU_COLS // HEAD_DIM_B
N_SLABS = CHUNK_DIM // V7X_MXU_COLS
MIX_SEGMENTS = 5
FINISH_SLICES = 8
FINISH_FIRST_CHUNK = 1
MIX_FINISH_SLICES = 4

BF16 = jnp.bfloat16
F32 = jnp.float32


def _rms(x, g):
    return x * lax.rsqrt(jnp.mean(x * x, axis=-1, keepdims=True) + EPS) * g


def _dot(a, b):
    return jnp.dot(a.astype(BF16), b, preferred_element_type=F32)


def _sigmoid(x):
    return 0.5 + 0.5 * jnp.tanh(0.5 * x)


def _silu(x):
    half = 0.5 * x
    return half + half * jnp.tanh(half)


def _after(value, anchor):
    bits = lax.bitcast_convert_type(anchor, jnp.int32)
    zero = lax.shift_right_logical(lax.shift_right_logical(bits, 16), 16)
    return value + zero.astype(value.dtype)


def _store_piece(dst_ref, src_ref, step, n_pieces):
    rows = src_ref.shape[0]

    @pl.when(step < n_pieces)
    def _():
        dst_ref[pl.ds(pl.multiple_of(step * rows, rows), rows), :] = src_ref[...].astype(BF16)


def _ffn_kernel(*refs, n_prep, n_prompt, n_blocks, with_ple):
    if with_ple:
        (xc_ref, xv_ref, pp_ref, ps_ref, pre_ref, post_ref, wg_ref, wu_ref, wd_ref, wpg_ref, wpp_ref, pleg_ref,
         o_ref, os_ref, wgu_s, wd_s, xn_s, a_s, wpg_s, wpp_s, hb_s) = refs
    else:
        (xpc_ref, xsc_ref, xpv_ref, xsv_ref, pre_ref, post_ref, wg_ref, wu_ref, wd_ref,
         o_ref, wgu_s, wd_s, xn_s, a_s) = refs
    i = pl.program_id(0)
    j = i - n_prep
    n_sample = n_blocks - n_prompt
    cur_is_sample = j + 1 >= n_prompt
    prev_is_sample = jnp.logical_and(j >= 0, j < n_sample) if with_ple else j >= n_prompt

    tm = xn_s.shape[0]
    n_slices = FINISH_SLICES // 2 if with_ple else FINISH_SLICES
    slice_rows = tm // n_slices
    stages = [("residual", s) for s in range(n_slices)]
    if with_ple:
        stages += [("embed_dots", 0)] + [("embed_add", s) for s in range(n_slices)]
    assert FINISH_FIRST_CHUNK + len(stages) <= n_prep

    def ordered(r, rows, anchor):
        return r[rows] if anchor is None else _after(r[rows], anchor[rows, 0:1])

    def x_rows(prompt_ref, sample_ref, is_sample, rows, prompt_only):
        if prompt_only:
            return prompt_ref[rows, :]
        return jnp.where(is_sample, sample_ref[rows, :], prompt_ref[rows, :])

    def finish_stage(stage, s, st, anchor=None, prompt_only=False):
        rows = slice(s * slice_rows, (s + 1) * slice_rows)
        if stage == "residual":
            x = xv_ref[rows, :] if with_ple else x_rows(xpv_ref, xsv_ref, prev_is_sample, rows, prompt_only)
            h = x + 0.5 * (st["y"][rows] * ordered(st["r"], rows, anchor) * post_ref[...])
            o_ref[rows, :] = h
            if with_ple:
                hb_s[rows, :] = h.astype(BF16)
        elif stage == "embed_dots":
            p = jnp.where(prev_is_sample, ps_ref[...], pp_ref[...])
            gate = _sigmoid(jnp.dot(hb_s[...], wpg_s[...], preferred_element_type=F32))
            st["e"] = gate * _dot(p, wpp_s[...])
            st["r2"] = lax.rsqrt(jnp.mean(st["e"] * st["e"], axis=-1, keepdims=True) + EPS)
        else:
            o_ref[rows, :] = o_ref[rows, :] + st["e"][rows] * ordered(st["r2"], rows, anchor) * pleg_ref[...]

    def down_prev():
        y = jnp.dot(a_s[...], wd_s[...], preferred_element_type=F32)
        return dict(y=y, r=lax.rsqrt(jnp.mean(y * y, axis=-1, keepdims=True) + EPS))

    def finish_prev():
        st = down_prev()
        for stage, s in stages:
            finish_stage(stage, s, st)

    def gate_up_chunk(c):
        gu = jnp.dot(xn_s[...], wgu_s[c], preferred_element_type=F32)
        g, u = gu[:, 0:WEIGHT_TILE], gu[:, WEIGHT_TILE:]
        return g, (_silu(g) * u).astype(BF16)

    def finish_prev_and_gate_up_cur(prompt_only=False):
        st = down_prev()
        x = xc_ref[...] if with_ple else x_rows(xpc_ref, xsc_ref, cur_is_sample, slice(None), prompt_only)
        x = _after(x, st["y"][:, 0:1])
        xn_s[...] = _rms(x, pre_ref[...]).astype(BF16)
        for c in range(n_prep):
            g, a_s[:, c * WEIGHT_TILE:(c + 1) * WEIGHT_TILE] = gate_up_chunk(c)
            k = c - FINISH_FIRST_CHUNK
            if 0 <= k < len(stages):
                finish_stage(*stages[k], st, anchor=g, prompt_only=prompt_only)

    @pl.when(i == 0)
    def _():
        xn_s[...] = _rms(xc_ref[...] if with_ple else xpc_ref[...], pre_ref[...]).astype(BF16)

    @pl.when(i < n_prep)
    def _():
        wgu_s[i, :, 0:WEIGHT_TILE] = wg_ref[...].astype(BF16)
        wgu_s[i, :, WEIGHT_TILE:] = wu_ref[...].astype(BF16)
        _, a = gate_up_chunk(i)
        for c in range(n_prep):
            @pl.when(i == c)
            def _():
                a_s[:, c * WEIGHT_TILE:(c + 1) * WEIGHT_TILE] = a

    _store_piece(wd_s, wd_ref, i, n_prep)
    if with_ple:
        _store_piece(wpg_s, wpg_ref, i, wpg_s.shape[0] // wpg_ref.shape[0])
        _store_piece(wpp_s, wpp_ref, i, 1)

    if with_ple:
        @pl.when(jnp.logical_and(j >= 0, j < n_blocks - 1))
        def _():
            finish_prev_and_gate_up_cur()
    else:
        @pl.when(jnp.logical_and(j >= 0, j < n_prompt - 1))
        def _():
            finish_prev_and_gate_up_cur(prompt_only=True)

        @pl.when(jnp.logical_and(j >= n_prompt - 1, j < n_blocks - 1))
        def _():
            finish_prev_and_gate_up_cur()

    @pl.when(j == n_blocks - 1)
    def _():
        finish_prev()

    if with_ple:
        @pl.when(prev_is_sample)
        def _():
            os_ref[...] = o_ref[...]


def _head_rms(v, ones_slab, g):
    vv = (v * v).astype(BF16)
    ms = jnp.concatenate(
        [jnp.dot(vv[:, k * V7X_MXU_COLS:(k + 1) * V7X_MXU_COLS], ones_slab, preferred_element_type=F32)
         for k in range(CHUNK_DIM // V7X_MXU_COLS)], axis=1)
    return v * lax.rsqrt(ms + EPS) * g


def _mix_kernel(hc_ref, hv_ref, state_ref, pre_ref, post_ref, win_ref, wout_ref, convw_ref, vg_ref,
                ws_ref, b_ref, ga_ref, gb_ref,
                o_ref, convp_ref, vp_ref, convs_ref, vs_ref,
                win_s, wout_s, n_s, yab_s, carry_ref, vn_s, vt_s, ones_ref, wslab_s, bias_s,
                *, n_prep, n_prompt, n_blocks, blocks_per_seq, seq_len):
    tm = hc_ref.shape[0]
    i = pl.program_id(0)
    j = i - n_prep

    @pl.when(i < n_prep)
    def _():
        win_s[i] = win_ref[...].astype(BF16)

    _store_piece(wout_s, wout_ref, i, wout_s.shape[0] // wout_ref.shape[0])

    @pl.when(i == 0)
    def _():
        yab_s[...] = jnp.zeros_like(yab_s)
        carry_ref[...] = jnp.zeros_like(carry_ref)
        row_head = lax.broadcasted_iota(jnp.int32, ones_ref.shape, 0) // HEAD_DIM_B
        col_head = lax.broadcasted_iota(jnp.int32, ones_ref.shape, 1) // HEAD_DIM_B
        ones_ref[...] = jnp.where(row_head == col_head, 1.0 / HEAD_DIM_B, 0.0).astype(BF16)

    assert MIX_FINISH_SLICES <= MIX_SEGMENTS
    slice_rows = tm // MIX_FINISH_SLICES

    def out_proj_prev():
        y = jnp.dot(yab_s[...], wout_s[...], preferred_element_type=F32)
        return dict(y=y, r=lax.rsqrt(jnp.mean(y * y, axis=-1, keepdims=True) + EPS), done=0)

    def finish_slice(prev, anchor=None):
        rows = slice(prev["done"] * slice_rows, (prev["done"] + 1) * slice_rows)
        prev["done"] += 1
        r = prev["r"][rows] if anchor is None else _after(prev["r"][rows], anchor[rows, 0:1])
        o_ref[rows, :] = hv_ref[rows, :] + prev["y"][rows] * r * post_ref[...]

    def seg(k, prev):
        out = jnp.dot(n_s[...], win_s[k], preferred_element_type=F32)
        if prev["done"] < MIX_FINISH_SLICES:
            finish_slice(prev, anchor=out)
        return out

    def conv(z, z1, z2):
        return convw_ref[0:1, :] * z2 + convw_ref[1:2, :] * z1 + convw_ref[2:3, :] * z

    def per_head(piece, rows):
        return jnp.concatenate([jnp.broadcast_to(piece(h), (rows, HEAD_DIM_B)) for h in range(HEADS_B)], axis=1)

    def v_gain():
        return jnp.concatenate([vg_ref[h:h + 1, :] for h in range(HEADS_B)], axis=1)

    def chunk_bias(rows):
        b_t = b_ref[...].T
        return per_head(lambda h: b_t[0:rows, h:h + 1], rows)

    @pl.when(i == 0)
    def _():
        t_idx = lax.broadcasted_iota(jnp.int32, (CHUNK, HEADS_PER_SLAB * CHUNK), 0)
        s_idx = lax.broadcasted_iota(jnp.int32, (CHUNK, HEADS_PER_SLAB * CHUNK), 1) % CHUNK
        for k in range(N_SLABS):
            w_k = jnp.concatenate([ws_ref[k * HEADS_PER_SLAB + hd] for hd in range(HEADS_PER_SLAB)], axis=1)
            wslab_s[k] = jnp.where(t_idx >= s_idx, w_k, 0.0).astype(BF16)
        bias_s[...] = chunk_bias(CHUNK)

    def group_norms(conv_out, mixed, prev):
        yab_s[:, 0:CONV_DIM] = _rms(seg(0, prev) * conv_out, ga_ref[...]).astype(BF16)
        yab_s[:, CONV_DIM:] = _rms(seg(3, prev) * mixed, gb_ref[...]).astype(BF16)

    def prompt_block():
        prev = out_proj_prev()
        n_s[...] = _rms(hc_ref[...], pre_ref[...]).astype(BF16)

        v_n = _head_rms(seg(4, prev), ones_ref[...], v_gain())
        vp_ref[0] = v_n[tm - CHUNK:, :].T
        lane_head = lax.broadcasted_iota(jnp.int32, (CHUNK, V7X_MXU_COLS), 1) // HEAD_DIM_B
        chunks = []
        for c in range(tm // CHUNK):
            slabs = []
            for k in range(N_SLABS):
                cols = slice(k * V7X_MXU_COLS, (k + 1) * V7X_MXU_COLS)
                v_ck = v_n[c * CHUNK:(c + 1) * CHUNK, cols].astype(BF16)
                bd = jnp.concatenate(
                    [jnp.where(lane_head == hd, v_ck, jnp.zeros((), BF16)) for hd in range(HEADS_PER_SLAB)], axis=0)
                slabs.append(jnp.dot(wslab_s[k], bd, preferred_element_type=F32))
            chunks.append(jnp.concatenate(slabs, axis=1) + bias_s[...])
        mixed = jnp.concatenate(chunks, axis=0)

        z = seg(1, prev) * seg(2, prev)
        row = lax.broadcasted_iota(jnp.int32, z.shape, 0)
        carry = jnp.where(j % blocks_per_seq == 0, 0.0, carry_ref[...])
        prev1 = carry[1:2, :]
        prev2 = carry[0:1, :]
        z1 = jnp.where(row == 0, prev1, pltpu.roll(z, 1, axis=0))
        z2 = jnp.where(row == 0, prev2, jnp.where(row == 1, prev1, pltpu.roll(z, 2, axis=0)))
        tail = z[tm - (CONV_K - 1):, :]
        carry_ref[...] = tail
        convp_ref[0] = tail
        group_norms(conv(z, z1, z2), mixed, prev)

    def sample_block():
        prev = out_proj_prev()
        n_s[...] = _rms(hc_ref[...], pre_ref[...]).astype(BF16)

        n_seq = tm // seq_len
        per_seq = lambda a: a.reshape(n_seq, seq_len, a.shape[-1])
        every_row = lambda a: jnp.broadcast_to(a, (n_seq, seq_len, a.shape[-1]))
        v_n = _head_rms(seg(4, prev), ones_ref[...], v_gain())
        first_seq = pl.multiple_of((j - n_prompt) * n_seq, n_seq)
        for c in range(CHUNK_DIM // V7X_LANES):
            lanes = slice(c * V7X_LANES, (c + 1) * V7X_LANES)
            vn_s[c] = v_n[:, lanes]
            for t in range(seq_len):
                vt_s[t, pl.ds(first_seq, n_seq), lanes] = vn_s[c, pl.ds(t, n_seq, stride=seq_len), :]

        @pl.when(j == n_blocks - 1)
        def _():
            for t in range(seq_len):
                vs_ref[t] = vt_s[t].T
        v_seq = per_seq(v_n)
        t_row = lax.broadcasted_iota(jnp.int32, (seq_len, CHUNK_DIM), 0)
        mixed = every_row(chunk_bias(seq_len)[None])
        for s in range(seq_len):
            w_from_s = per_head(lambda h: ws_ref[h, 0:seq_len, s:s + 1], seq_len)
            mixed = mixed + jnp.where(t_row >= s, w_from_s, 0.0)[None] * every_row(v_seq[:, s:s + 1, :])
        mixed = mixed.reshape(tm, CHUNK_DIM)

        z = seg(1, prev) * seg(2, prev)
        t = lax.broadcasted_iota(jnp.int32, z.shape, 0) % seq_len
        st = state_ref[...]
        prev1 = every_row(st[:, 1:2, :]).reshape(tm, CONV_DIM)
        prev2 = every_row(st[:, 0:1, :]).reshape(tm, CONV_DIM)
        z1 = jnp.where(t == 0, prev1, pltpu.roll(z, 1, axis=0))
        z2 = jnp.where(t == 0, prev2, jnp.where(t == 1, prev1, pltpu.roll(z, 2, axis=0)))
        convs_ref[...] = per_seq(z)[:, seq_len - (CONV_K - 1):, :]
        group_norms(conv(z, z1, z2), mixed, prev)

    @pl.when(jnp.logical_and(j >= 0, j < n_prompt))
    def _():
        prompt_block()

    @pl.when(jnp.logical_and(j >= n_prompt, j < n_blocks))
    def _():
        sample_block()

    @pl.when(j == n_blocks)
    def _():
        prev = out_proj_prev()
        for _ in range(MIX_FINISH_SLICES):
            finish_slice(prev)


def _resident(a):
    return pl.BlockSpec(a.shape, lambda i: (0,) * a.ndim, pipeline_mode=pl.Buffered(1))


def _clamped(lo, n):
    return lambda i: jnp.minimum(jnp.maximum(i - lo, 0), n - 1)


def _stream(tm, d, lo, n):
    blk = _clamped(lo, n)
    return pl.BlockSpec((tm, d), lambda i: (blk(i), 0))


def _row_pieces(w, rows):
    blk = _clamped(0, w.shape[0] // rows)
    return pl.BlockSpec((rows, w.shape[1]), lambda i: (blk(i), 0))


def _col_pieces(w, cols):
    blk = _clamped(0, w.shape[1] // cols)
    return pl.BlockSpec((w.shape[0], cols), lambda i: (0, blk(i)))


def _params():
    return pltpu.CompilerParams(dimension_semantics=("arbitrary",), vmem_limit_bytes=V7X_VMEM_LIMIT_BYTES)


def _ffn(x, n_prompt, n_sample, pre_g, post_g, wg, wu, wd, ple=None, *, name):
    tm = TOKEN_BLOCK
    d, d_ff = wg.shape
    n_prep = d_ff // WEIGHT_TILE
    n_blocks = n_prompt + n_sample
    nxt, prv = n_prep - 1, n_prep
    weights = [_resident(pre_g), _resident(post_g),
               _col_pieces(wg, WEIGHT_TILE), _col_pieces(wu, WEIGHT_TILE), _row_pieces(wd, WEIGHT_TILE)]
    scratch = [pltpu.VMEM((n_prep, d, 2 * WEIGHT_TILE), BF16),
               pltpu.VMEM((d_ff, d), BF16), pltpu.VMEM((tm, d), BF16), pltpu.VMEM((tm, d_ff), BF16)]
    if ple is None:
        xp, xs = x
        in_specs = [_stream(tm, d, nxt, n_prompt), _stream(tm, d, nxt + n_prompt, n_sample),
                    _stream(tm, d, prv, n_prompt), _stream(tm, d, prv + n_prompt, n_sample)] + weights
        args = [xp, xs, xp, xs, pre_g, post_g, wg, wu, wd]
        out_specs = _stream(tm, d, prv, n_blocks)
        out_shape = jax.ShapeDtypeStruct((n_blocks * tm, d), F32)
    else:
        pp, ps, wpg, wpp, ple_g = ple
        assert wpg.shape[0] // (WEIGHT_TILE // 2) <= n_prep
        d_p = pp.shape[1]

        def sample_first(lo):
            k = _clamped(lo, n_blocks)
            return pl.BlockSpec((tm, d), lambda i: (jnp.where(k(i) < n_sample, k(i) + n_prompt, k(i) - n_sample), 0))

        in_specs = [sample_first(nxt), sample_first(prv),
                    _stream(tm, d_p, prv + n_sample, n_prompt), _stream(tm, d_p, prv, n_sample)] + weights
        in_specs += [_row_pieces(wpg, WEIGHT_TILE // 2), _row_pieces(wpp, wpp.shape[0]), _resident(ple_g)]
        args = [x, x, pp, ps, pre_g, post_g, wg, wu, wd, wpg, wpp, ple_g]
        out_specs = [_stream(tm, d, prv + n_sample, n_prompt), _stream(tm, d, prv, n_sample)]
        out_shape = [jax.ShapeDtypeStruct((n_prompt * tm, d), F32), jax.ShapeDtypeStruct((n_sample * tm, d), F32)]
        scratch += [pltpu.VMEM(wpg.shape, BF16), pltpu.VMEM(wpp.shape, BF16), pltpu.VMEM((tm, d), BF16)]
    return pl.pallas_call(
        functools.partial(_ffn_kernel, n_prep=n_prep, n_prompt=n_prompt, n_blocks=n_blocks,
                          with_ple=ple is not None),
        grid=(n_prep + n_blocks,),
        in_specs=in_specs,
        out_specs=out_specs,
        out_shape=out_shape,
        scratch_shapes=scratch,
        compiler_params=_params(),
        name=name,
    )(*args)


def _mix(h, state, batch_p, seq_p, batch_s, seq_s, mw):
    tm = MIX_TOKEN_BLOCK
    d = h.shape[1]
    n_prep = MIX_SEGMENTS
    assert seq_p % tm == 0 and (batch_s * seq_s) % tm == 0 and tm % seq_s == 0
    n_prompt, n_sample = batch_p * seq_p // tm, batch_s * seq_s // tm
    n_blocks = n_prompt + n_sample
    blocks_per_seq = seq_p // tm
    seqs_per_block = tm // seq_s
    assert d // WEIGHT_TILE <= n_prep
    small = [mw["v_g"], mw["w_s"], mw["b_s"], mw["g_a"], mw["g_b"]]
    conv_w_spec = pl.BlockSpec((None,) + mw["conv_w"].shape[1:], lambda i: (0, 0, 0), pipeline_mode=pl.Buffered(1))
    lo_s = n_prep + n_prompt
    prompt_seq = _clamped(0, batch_p)
    per_prompt_seq = lambda i: (prompt_seq((i - n_prep) // blocks_per_seq), 0, 0)
    sample_blk = _clamped(lo_s, n_sample)
    per_sample_blk = pl.BlockSpec((seqs_per_block, CONV_K - 1, CONV_DIM), lambda i: (sample_blk(i), 0, 0))
    return pl.pallas_call(
        functools.partial(_mix_kernel, n_prep=n_prep, n_prompt=n_prompt, n_blocks=n_blocks,
                          blocks_per_seq=blocks_per_seq, seq_len=seq_s),
        grid=(n_prep + n_blocks + 1,),
        in_specs=[_stream(tm, d, n_prep, n_blocks), _stream(tm, d, n_prep + 1, n_blocks),
                  per_sample_blk, _resident(mw["pre"]), _resident(mw["post"]),
                  _col_pieces(mw["w_in"], CONV_DIM), _row_pieces(mw["w_out"], WEIGHT_TILE), conv_w_spec]
                 + [_resident(a) for a in small],
        out_specs=[_stream(tm, d, n_prep + 1, n_blocks),
                   pl.BlockSpec((1, CONV_K - 1, CONV_DIM), per_prompt_seq),
                   pl.BlockSpec((1, CHUNK_DIM, CHUNK), per_prompt_seq),
                   per_sample_blk,
                   pl.BlockSpec((seq_s, CHUNK_DIM, batch_s), lambda i: (0, 0, 0))],
        out_shape=[jax.ShapeDtypeStruct(h.shape, F32),
                   jax.ShapeDtypeStruct((batch_p, CONV_K - 1, CONV_DIM), F32),
                   jax.ShapeDtypeStruct((batch_p, CHUNK_DIM, CHUNK), F32),
                   jax.ShapeDtypeStruct((batch_s, CONV_K - 1, CONV_DIM), F32),
                   jax.ShapeDtypeStruct((seq_s, CHUNK_DIM, batch_s), F32)],
        scratch_shapes=[pltpu.VMEM((MIX_SEGMENTS, d, CONV_DIM), BF16), pltpu.VMEM(mw["w_out"].shape, BF16),
                        pltpu.VMEM((tm, d), BF16), pltpu.VMEM((tm, CONV_DIM + CHUNK_DIM), BF16),
                        pltpu.VMEM((CONV_K - 1, CONV_DIM), F32),
                        pltpu.VMEM((CHUNK_DIM // V7X_LANES, tm, V7X_LANES), F32),
                        pltpu.VMEM((seq_s, batch_s, CHUNK_DIM), F32),
                        pltpu.VMEM((V7X_MXU_COLS, V7X_MXU_COLS), BF16),
                        pltpu.VMEM((N_SLABS, CHUNK, HEADS_PER_SLAB * CHUNK), BF16),
                        pltpu.VMEM((CHUNK, CHUNK_DIM), F32)],
        compiler_params=_params(),
        name="mix",
    )(h, h, state, mw["pre"], mw["post"], mw["w_in"], mw["w_out"], mw["conv_w"], *small)


def _row(g):
    return g.reshape(1, -1).astype(F32)


def _mix_weights(mix_pre_g, mix_post_g, w_in, conv_w, v_norm_g, w_s, b_s, out_g_a, out_g_b, w_out):
    return dict(pre=_row(mix_pre_g), post=_row(mix_post_g), w_in=w_in, w_out=w_out,
                conv_w=conv_w.astype(F32), v_g=v_norm_g.astype(F32), w_s=w_s.astype(F32),
                b_s=b_s.astype(F32), g_a=_row(out_g_a), g_b=_row(out_g_b))


def kernel(x_prompt, x_sample, p_prompt, p_sample, state_conv, ffn1_pre_g, ffn1_post_g, ffn1_w_gate, ffn1_w_up, ffn1_w_down, mix_pre_g, mix_post_g, w_in, conv_w, v_norm_g, w_s, b_s, out_g_a, out_g_b, w_out, ffn2_pre_g, ffn2_post_g, ffn2_w_gate, ffn2_w_up, ffn2_w_down, ple_w_gate, ple_w_proj, ple_post_g):
    depth = ffn1_pre_g.shape[0]
    bp, lp, d = x_prompt.shape
    bs, ls, _ = x_sample.shape
    tm = TOKEN_BLOCK
    assert lp % tm == 0 and (bs * ls) % tm == 0 and tm % ls == 0
    assert CONV_K - 1 <= ls <= V7X_SUBLANES and (lp - 1) % CHUNK + 1 == CHUNK
    n_prompt, n_sample = bp * lp // tm, bs * ls // tm
    x = (x_prompt.reshape(bp * lp, d), x_sample.reshape(bs * ls, d))
    conv_p, conv_s, v_p, v_s = [], [], [], []
    for i in range(depth):
        mw = _mix_weights(mix_pre_g[i], mix_post_g[i], w_in[i], conv_w[i:i + 1], v_norm_g[i], w_s[i], b_s[i],
                          out_g_a[i], out_g_b[i], w_out[i])
        ple = (p_prompt[i].reshape(bp * lp, -1), p_sample[i].reshape(bs * ls, -1),
               ple_w_gate[i], ple_w_proj[i], _row(ple_post_g[i]))

        h = _ffn(x, n_prompt, n_sample, _row(ffn1_pre_g[i]), _row(ffn1_post_g[i]),
                 ffn1_w_gate[i], ffn1_w_up[i], ffn1_w_down[i], name="ffn1")
        h, cp, vp, cs, vs = _mix(h, state_conv[i].astype(F32), bp, lp, bs, ls, mw)
        x = _ffn(h, n_prompt, n_sample, _row(ffn2_pre_g[i]), _row(ffn2_post_g[i]),
                 ffn2_w_gate[i], ffn2_w_up[i], ffn2_w_down[i], ple, name="ffn2_ple")

        conv_p.append(cp)
        conv_s.append(cs)
        v_p.append(vp.transpose(0, 2, 1).reshape(bp, CHUNK, HEADS_B, HEAD_DIM_B))
        v_s.append(vs.transpose(2, 0, 1).reshape(bs, ls, HEADS_B, HEAD_DIM_B))
    layers = lambda parts: parts[0][None] if depth == 1 else jnp.stack(parts)
    return (x[0].reshape(bp, lp, d), x[1].reshape(bs, ls, d),
            layers(conv_p), layers(conv_s), layers(v_p), layers(v_s))
```

```python
import functools

import jax
import jax.numpy as jnp
from jax import lax
from jax.experimental import pallas as pl
from jax.experimental.pallas import tpu as pltpu

CONV_K = 3
HEADS_B = 8
HEAD_DIM_B = 64
CHUNK_DIM = HEADS_B * HEAD_DIM_B
CONV_DIM = CHUNK_DIM
CHUNK = 128
EPS = 1e-6

V7X_SUBLANES = 8
V7X_LANES = 128
V7X_MXU_COLS = 256
V7X_VMEM_LIMIT_BYTES = 60 * 1024 * 1024

TOKEN_BLOCK = 512
MIX_TOKEN_BLOCK = 512
WEIGHT_TILE = V7X_MXU_COLS
HEADS_PER_SLAB = V7X_MXU_COLS // HEAD_DIM_B
N_SLABS = CHUNK_DIM // V7X_MXU_COLS
MIX_SEGMENTS = 5
FINISH_SLICES = 8
FINISH_FIRST_CHUNK = 1
MIX_FINISH_SLICES = 4

BF16 = jnp.bfloat16
F32 = jnp.float32


def _rms(x, g):
    return x * lax.rsqrt(jnp.mean(x * x, axis=-1, keepdims=True) + EPS) * g


def _dot(a, b):
    return jnp.dot(a.astype(BF16), b, preferred_element_type=F32)


def _sigmoid(x):
    return 0.5 + 0.5 * jnp.tanh(0.5 * x)


def _silu(x):
    half = 0.5 * x
    return half + half * jnp.tanh(half)


def _after(value, anchor):
    bits = lax.bitcast_convert_type(anchor, jnp.int32)
    zero = lax.shift_right_logical(lax.shift_right_logical(bits, 16), 16)
    return value + zero.astype(value.dtype)


def _store_piece(dst_ref, src_ref, step, n_pieces):
    rows = src_ref.shape[0]

    @pl.when(step < n_pieces)
    def _():
        dst_ref[pl.ds(pl.multiple_of(step * rows, rows), rows), :] = src_ref[...].astype(BF16)


def _ffn_kernel(*refs, n_prep, n_prompt, n_blocks, with_ple):
    if with_ple:
        (xc_ref, xv_ref, pp_ref, ps_ref, pre_ref, post_ref, wg_ref, wu_ref, wd_ref, wpg_ref, wpp_ref, pleg_ref,
         o_ref, os_ref, wgu_s, wd_s, xn_s, a_s, wpg_s, wpp_s, hb_s) = refs
    else:
        (xpc_ref, xsc_ref, xpv_ref, xsv_ref, pre_ref, post_ref, wg_ref, wu_ref, wd_ref,
         o_ref, wgu_s, wd_s, xn_s, a_s) = refs
    i = pl.program_id(0)
    j = i - n_prep
    n_sample = n_blocks - n_prompt
    cur_is_sample = j + 1 >= n_prompt
    prev_is_sample = jnp.logical_and(j >= 0, j < n_sample) if with_ple else j >= n_prompt

    tm = xn_s.shape[0]
    n_slices = FINISH_SLICES // 2 if with_ple else FINISH_SLICES
    slice_rows = tm // n_slices
    stages = [("residual", s) for s in range(n_slices)]
    if with_ple:
        stages += [("embed_dots", 0)] + [("embed_add", s) for s in range(n_slices)]
    assert FINISH_FIRST_CHUNK + len(stages) <= n_prep

    def ordered(r, rows, anchor):
        return r[rows] if anchor is None else _after(r[rows], anchor[rows, 0:1])

    def x_rows(prompt_ref, sample_ref, is_sample, rows, prompt_only):
        if prompt_only:
            return prompt_ref[rows, :]
        return jnp.where(is_sample, sample_ref[rows, :], prompt_ref[rows, :])

    def finish_stage(stage, s, st, anchor=None, prompt_only=False):
        rows = slice(s * slice_rows, (s + 1) * slice_rows)
        if stage == "residual":
            x = xv_ref[rows, :] if with_ple else x_rows(xpv_ref, xsv_ref, prev_is_sample, rows, prompt_only)
            h = x + 0.5 * (st["y"][rows] * ordered(st["r"], rows, anchor) * post_ref[...])
            o_ref[rows, :] = h
            if with_ple:
                hb_s[rows, :] = h.astype(BF16)
        elif stage == "embed_dots":
            p = jnp.where(prev_is_sample, ps_ref[...], pp_ref[...])
            gate = _sigmoid(jnp.dot(hb_s[...], wpg_s[...], preferred_element_type=F32))
            st["e"] = gate * _dot(p, wpp_s[...])
            st["r2"] = lax.rsqrt(jnp.mean(st["e"] * st["e"], axis=-1, keepdims=True) + EPS)
        else:
            o_ref[rows, :] = o_ref[rows, :] + st["e"][rows] * ordered(st["r2"], rows, anchor) * pleg_ref[...]

    def down_prev():
        y = jnp.dot(a_s[...], wd_s[...], preferred_element_type=F32)
        return dict(y=y, r=lax.rsqrt(jnp.mean(y * y, axis=-1, keepdims=True) + EPS))

    def finish_prev():
        st = down_prev()
        for stage, s in stages:
            finish_stage(stage, s, st)

    def gate_up_chunk(c):
        gu = jnp.dot(xn_s[...], wgu_s[c], preferred_element_type=F32)
        g, u = gu[:, 0:WEIGHT_TILE], gu[:, WEIGHT_TILE:]
        return g, (_silu(g) * u).astype(BF16)

    def finish_prev_and_gate_up_cur(prompt_only=False):
        st = down_prev()
        x = xc_ref[...] if with_ple else x_rows(xpc_ref, xsc_ref, cur_is_sample, slice(None), prompt_only)
        x = _after(x, st["y"][:, 0:1])
        xn_s[...] = _rms(x, pre_ref[...]).astype(BF16)
        for c in range(n_prep):
            g, a_s[:, c * WEIGHT_TILE:(c + 1) * WEIGHT_TILE] = gate_up_chunk(c)
            k = c - FINISH_FIRST_CHUNK
            if 0 <= k < len(stages):
                finish_stage(*stages[k], st, anchor=g, prompt_only=prompt_only)

    @pl.when(i == 0)
    def _():
        xn_s[...] = _rms(xc_ref[...] if with_ple else xpc_ref[...], pre_ref[...]).astype(BF16)

    @pl.when(i < n_prep)
    def _():
        wgu_s[i, :, 0:WEIGHT_TILE] = wg_ref[...].astype(BF16)
        wgu_s[i, :, WEIGHT_TILE:] = wu_ref[...].astype(BF16)
        _, a = gate_up_chunk(i)
        for c in range(n_prep):
            @pl.when(i == c)
            def _():
                a_s[:, c * WEIGHT_TILE:(c + 1) * WEIGHT_TILE] = a

    _store_piece(wd_s, wd_ref, i, n_prep)
    if with_ple:
        _store_piece(wpg_s, wpg_ref, i, wpg_s.shape[0] // wpg_ref.shape[0])
        _store_piece(wpp_s, wpp_ref, i, 1)

    if with_ple:
        @pl.when(jnp.logical_and(j >= 0, j < n_blocks - 1))
        def _():
            finish_prev_and_gate_up_cur()
    else:
        @pl.when(jnp.logical_and(j >= 0, j < n_prompt - 1))
        def _():
            finish_prev_and_gate_up_cur(prompt_only=True)

        @pl.when(jnp.logical_and(j >= n_prompt - 1, j < n_blocks - 1))
        def _():
            finish_prev_and_gate_up_cur()

    @pl.when(j == n_blocks - 1)
    def _():
        finish_prev()

    if with_ple:
        @pl.when(prev_is_sample)
        def _():
            os_ref[...] = o_ref[...]


def _ffn1_loop_kernel(xp_hbm, xs_hbm, pre_ref, post_ref, wg_ref, wu_ref, wd_ref, o_hbm,
                      wgu_s, wd_s, xn_s, a_s, xbuf, obuf, in_sem, out_sem, *, n_prep, n_prompt, n_blocks):
    i = pl.program_id(0)
    tm = xn_s.shape[0]
    slice_rows = tm // FINISH_SLICES
    n_x, n_o = xbuf.shape[0], obuf.shape[0]

    def x_copy(b, src):
        return pltpu.make_async_copy(src, xbuf.at[b % n_x], in_sem.at[b % n_x])

    def start_x(b):
        if isinstance(b, int):
            src = xp_hbm if b < n_prompt else xs_hbm
            x_copy(b, src.at[pl.ds((b if b < n_prompt else b - n_prompt) * tm, tm)]).start()
            return

        @pl.when(b < n_prompt)
        def _():
            x_copy(b, xp_hbm.at[pl.ds(pl.multiple_of(b * tm, tm), tm)]).start()

        @pl.when(b >= n_prompt)
        def _():
            x_copy(b, xs_hbm.at[pl.ds(pl.multiple_of((b - n_prompt) * tm, tm), tm)]).start()

    def wait_x(b):
        x_copy(b, xp_hbm.at[pl.ds(0, tm)]).wait()

    def out_copy(b):
        row0 = b * tm if isinstance(b, int) else pl.multiple_of(b * tm, tm)
        return pltpu.make_async_copy(obuf.at[b % n_o], o_hbm.at[pl.ds(row0, tm)], out_sem.at[b % n_o])

    def gate_up_chunk(c):
        gu = jnp.dot(xn_s[...], wgu_s[c], preferred_element_type=F32)
        g, u = gu[:, 0:WEIGHT_TILE], gu[:, WEIGHT_TILE:]
        return g, (_silu(g) * u).astype(BF16)

    def down(b):
        y = jnp.dot(a_s[...], wd_s[...], preferred_element_type=F32)
        return y, lax.rsqrt(jnp.mean(y * y, axis=-1, keepdims=True) + EPS)

    def finish_rows(b, y, r, s, anchor=None):
        rows = slice(s * slice_rows, (s + 1) * slice_rows)
        r = r[rows] if anchor is None else _after(r[rows], anchor[rows, 0:1])
        obuf[b % n_o, rows, :] = xbuf[b % n_x, rows, :] + 0.5 * (y[rows] * r * post_ref[...])

    @pl.when(i == 0)
    def _():
        start_x(0)
        wait_x(0)
        if n_blocks > 1:
            start_x(1)
        xn_s[...] = _rms(xbuf[0], pre_ref[...]).astype(BF16)

    @pl.when(i < n_prep)
    def _():
        wgu_s[i, :, 0:WEIGHT_TILE] = wg_ref[...].astype(BF16)
        wgu_s[i, :, WEIGHT_TILE:] = wu_ref[...].astype(BF16)
        _, a = gate_up_chunk(i)
        for c in range(n_prep):
            @pl.when(i == c)
            def _():
                a_s[:, c * WEIGHT_TILE:(c + 1) * WEIGHT_TILE] = a

    _store_piece(wd_s, wd_ref, i, n_prep)

    @pl.when(i == n_prep)
    def _():
        def step(b, carry):
            wait_x(b + 1)

            @pl.when(b + 2 < n_blocks)
            def _():
                start_x(b + 2)

            @pl.when(b >= n_o)
            def _():
                out_copy(b - n_o).wait()

            y, r = down(b)
            x = _after(xbuf[(b + 1) % n_x], y[:, 0:1])
            xn_s[...] = _rms(x, pre_ref[...]).astype(BF16)
            for c in range(n_prep):
                g, a_s[:, c * WEIGHT_TILE:(c + 1) * WEIGHT_TILE] = gate_up_chunk(c)
                s = c - FINISH_FIRST_CHUNK
                if 0 <= s < FINISH_SLICES:
                    finish_rows(b, y, r, s, anchor=g)
            out_copy(b).start()
            return carry

        lax.fori_loop(0, n_blocks - 1, step, 0)
        last = n_blocks - 1
        if last >= n_o:
            out_copy(last - n_o).wait()
        y, r = down(last)
        for s in range(FINISH_SLICES):
            finish_rows(last, y, r, s)
        out_copy(last).start()
        for b in range(max(last - n_o + 1, 0), last + 1):
            out_copy(b).wait()


def _ffn1_loop(xp, xs, pre_g, post_g, wg, wu, wd):
    tm = TOKEN_BLOCK
    d, d_ff = wg.shape
    n_prep = d_ff // WEIGHT_TILE
    n_prompt, n_sample = xp.shape[0] // tm, xs.shape[0] // tm
    n_blocks = n_prompt + n_sample
    assert FINISH_FIRST_CHUNK + FINISH_SLICES <= n_prep and n_prompt >= 1
    anywhere = pl.BlockSpec(memory_space=pl.ANY)
    return pl.pallas_call(
        functools.partial(_ffn1_loop_kernel, n_prep=n_prep, n_prompt=n_prompt, n_blocks=n_blocks),
        grid=(n_prep + 1,),
        in_specs=[anywhere, anywhere, _resident(pre_g), _resident(post_g),
                  _col_pieces(wg, WEIGHT_TILE), _col_pieces(wu, WEIGHT_TILE), _row_pieces(wd, WEIGHT_TILE)],
        out_specs=anywhere,
        out_shape=jax.ShapeDtypeStruct((n_blocks * tm, d), F32),
        scratch_shapes=[pltpu.VMEM((n_prep, d, 2 * WEIGHT_TILE), BF16), pltpu.VMEM((d_ff, d), BF16),
                        pltpu.VMEM((tm, d), BF16), pltpu.VMEM((tm, d_ff), BF16),
                        pltpu.VMEM((3, tm, d), F32), pltpu.VMEM((2, tm, d), F32),
                        pltpu.SemaphoreType.DMA((3,)), pltpu.SemaphoreType.DMA((2,))],
        compiler_params=_params(),
        name="ffn1",
    )(xp, xs, pre_g, post_g, wg, wu, wd)


def _head_rms(v, ones_slab, g):
    vv = (v * v).astype(BF16)
    ms = jnp.concatenate(
        [jnp.dot(vv[:, k * V7X_MXU_COLS:(k + 1) * V7X_MXU_COLS], ones_slab, preferred_element_type=F32)
         for k in range(CHUNK_DIM // V7X_MXU_COLS)], axis=1)
    return v * lax.rsqrt(ms + EPS) * g


def _mix_kernel(hc_ref, hv_ref, state_ref, pre_ref, post_ref, win_ref, wout_ref, convw_ref, vg_ref,
                ws_ref, b_ref, ga_ref, gb_ref,
                o_ref, convp_ref, vp_ref, convs_ref, vs_ref,
                win_s, wout_s, n_s, yab_s, carry_ref, vn_s, vt_s, ones_ref, wslab_s, bias_s,
                *, n_prep, n_prompt, n_blocks, blocks_per_seq, seq_len):
    tm = hc_ref.shape[0]
    i = pl.program_id(0)
    j = i - n_prep

    @pl.when(i < n_prep)
    def _():
        win_s[i] = win_ref[...].astype(BF16)

    _store_piece(wout_s, wout_ref, i, wout_s.shape[0] // wout_ref.shape[0])

    @pl.when(i == 0)
    def _():
        yab_s[...] = jnp.zeros_like(yab_s)
        carry_ref[...] = jnp.zeros_like(carry_ref)
        row_head = lax.broadcasted_iota(jnp.int32, ones_ref.shape, 0) // HEAD_DIM_B
        col_head = lax.broadcasted_iota(jnp.int32, ones_ref.shape, 1) // HEAD_DIM_B
        ones_ref[...] = jnp.where(row_head == col_head, 1.0 / HEAD_DIM_B, 0.0).astype(BF16)

    assert MIX_FINISH_SLICES <= MIX_SEGMENTS
    slice_rows = tm // MIX_FINISH_SLICES

    def out_proj_prev():
        y = jnp.dot(yab_s[...], wout_s[...], preferred_element_type=F32)
        return dict(y=y, r=lax.rsqrt(jnp.mean(y * y, axis=-1, keepdims=True) + EPS), done=0)

    def finish_slice(prev, anchor=None):
        rows = slice(prev["done"] * slice_rows, (prev["done"] + 1) * slice_rows)
        prev["done"] += 1
        r = prev["r"][rows] if anchor is None else _after(prev["r"][rows], anchor[rows, 0:1])
        o_ref[rows, :] = hv_ref[rows, :] + prev["y"][rows] * r * post_ref[...]

    def seg(k, prev):
        out = jnp.dot(n_s[...], win_s[k], preferred_element_type=F32)
        if prev["done"] < MIX_FINISH_SLICES:
            finish_slice(prev, anchor=out)
        return out

    def conv(z, z1, z2):
        return convw_ref[0:1, :] * z2 + convw_ref[1:2, :] * z1 + convw_ref[2:3, :] * z

    def per_head(piece, rows):
        return jnp.concatenate([jnp.broadcast_to(piece(h), (rows, HEAD_DIM_B)) for h in range(HEADS_B)], axis=1)

    def v_gain():
        return jnp.concatenate([vg_ref[h:h + 1, :] for h in range(HEADS_B)], axis=1)

    def chunk_bias(rows):
        b_t = b_ref[...].T
        return per_head(lambda h: b_t[0:rows, h:h + 1], rows)

    @pl.when(i == 0)
    def _():
        t_idx = lax.broadcasted_iota(jnp.int32, (CHUNK, HEADS_PER_SLAB * CHUNK), 0)
        s_idx = lax.broadcasted_iota(jnp.int32, (CHUNK, HEADS_PER_SLAB * CHUNK), 1) % CHUNK
        for k in range(N_SLABS):
            w_k = jnp.concatenate([ws_ref[k * HEADS_PER_SLAB + hd] for hd in range(HEADS_PER_SLAB)], axis=1)
            wslab_s[k] = jnp.where(t_idx >= s_idx, w_k, 0.0).astype(BF16)
        bias_s[...] = chunk_bias(CHUNK)

    def group_norms(conv_out, mixed, prev):
        yab_s[:, 0:CONV_DIM] = _rms(seg(0, prev) * conv_out, ga_ref[...]).astype(BF16)
        yab_s[:, CONV_DIM:] = _rms(seg(3, prev) * mixed, gb_ref[...]).astype(BF16)

    def prompt_block():
        prev = out_proj_prev()
        n_s[...] = _rms(hc_ref[...], pre_ref[...]).astype(BF16)

        v_n = _head_rms(seg(4, prev), ones_ref[...], v_gain())
        vp_ref[0] = v_n[tm - CHUNK:, :].T
        lane_head = lax.broadcasted_iota(jnp.int32, (CHUNK, V7X_MXU_COLS), 1) // HEAD_DIM_B
        chunks = []
        for c in range(tm // CHUNK):
            slabs = []
            for k in range(N_SLABS):
                cols = slice(k * V7X_MXU_COLS, (k + 1) * V7X_MXU_COLS)
                v_ck = v_n[c * CHUNK:(c + 1) * CHUNK, cols].astype(BF16)
                bd = jnp.concatenate(
                    [jnp.where(lane_head == hd, v_ck, jnp.zeros((), BF16)) for hd in range(HEADS_PER_SLAB)], axis=0)
                slabs.append(jnp.dot(wslab_s[k], bd, preferred_element_type=F32))
            chunks.append(jnp.concatenate(slabs, axis=1) + bias_s[...])
        mixed = jnp.concatenate(chunks, axis=0)

        z = seg(1, prev) * seg(2, prev)
        row = lax.broadcasted_iota(jnp.int32, z.shape, 0)
        carry = jnp.where(j % blocks_per_seq == 0, 0.0, carry_ref[...])
        prev1 = carry[1:2, :]
        prev2 = carry[0:1, :]
        z1 = jnp.where(row == 0, prev1, pltpu.roll(z, 1, axis=0))
        z2 = jnp.where(row == 0, prev2, jnp.where(row == 1, prev1, pltpu.roll(z, 2, axis=0)))
        tail = z[tm - (CONV_K - 1):, :]
        carry_ref[...] = tail
        convp_ref[0] = tail
        group_norms(conv(z, z1, z2), mixed, prev)

    def sample_block():
        prev = out_proj_prev()
        n_s[...] = _rms(hc_ref[...], pre_ref[...]).astype(BF16)

        n_seq = tm // seq_len
        per_seq = lambda a: a.reshape(n_seq, seq_len, a.shape[-1])
        every_row = lambda a: jnp.broadcast_to(a, (n_seq, seq_len, a.shape[-1]))
        v_n = _head_rms(seg(4, prev), ones_ref[...], v_gain())
        first_seq = pl.multiple_of((j - n_prompt) * n_seq, n_seq)
        for c in range(CHUNK_DIM // V7X_LANES):
            lanes = slice(c * V7X_LANES, (c + 1) * V7X_LANES)
            vn_s[c] = v_n[:, lanes]
            for t in range(seq_len):
                vt_s[t, pl.ds(first_seq, n_seq), lanes] = vn_s[c, pl.ds(t, n_seq, stride=seq_len), :]

        @pl.when(j == n_blocks - 1)
        def _():
            for t in range(seq_len):
                vs_ref[t] = vt_s[t].T
        v_seq = per_seq(v_n)
        t_row = lax.broadcasted_iota(jnp.int32, (seq_len, CHUNK_DIM), 0)
        mixed = every_row(chunk_bias(seq_len)[None])
        for s in range(seq_len):
            w_from_s = per_head(lambda h: ws_ref[h, 0:seq_len, s:s + 1], seq_len)
            mixed = mixed + jnp.where(t_row >= s, w_from_s, 0.0)[None] * every_row(v_seq[:, s:s + 1, :])
        mixed = mixed.reshape(tm, CHUNK_DIM)

        z = seg(1, prev) * seg(2, prev)
        t = lax.broadcasted_iota(jnp.int32, z.shape, 0) % seq_len
        st = state_ref[...]
        prev1 = every_row(st[:, 1:2, :]).reshape(tm, CONV_DIM)
        prev2 = every_row(st[:, 0:1, :]).reshape(tm, CONV_DIM)
        z1 = jnp.where(t == 0, prev1, pltpu.roll(z, 1, axis=0))
        z2 = jnp.where(t == 0, prev2, jnp.where(t == 1, prev1, pltpu.roll(z, 2, axis=0)))
        convs_ref[...] = per_seq(z)[:, seq_len - (CONV_K - 1):, :]
        group_norms(conv(z, z1, z2), mixed, prev)

    @pl.when(jnp.logical_and(j >= 0, j < n_prompt))
    def _():
        prompt_block()

    @pl.when(jnp.logical_and(j >= n_prompt, j < n_blocks))
    def _():
        sample_block()

    @pl.when(j == n_blocks)
    def _():
        prev = out_proj_prev()
        for _ in range(MIX_FINISH_SLICES):
            finish_slice(prev)


def _resident(a):
    return pl.BlockSpec(a.shape, lambda i: (0,) * a.ndim, pipeline_mode=pl.Buffered(1))


def _clamped(lo, n):
    return lambda i: jnp.minimum(jnp.maximum(i - lo, 0), n - 1)


def _stream(tm, d, lo, n):
    blk = _clamped(lo, n)
    return pl.BlockSpec((tm, d), lambda i: (blk(i), 0))


def _row_pieces(w, rows):
    blk = _clamped(0, w.shape[0] // rows)
    return pl.BlockSpec((rows, w.shape[1]), lambda i: (blk(i), 0))


def _col_pieces(w, cols):
    blk = _clamped(0, w.shape[1] // cols)
    return pl.BlockSpec((w.shape[0], cols), lambda i: (0, blk(i)))


def _params():
    return pltpu.CompilerParams(dimension_semantics=("arbitrary",), vmem_limit_bytes=V7X_VMEM_LIMIT_BYTES)


def _ffn(x, n_prompt, n_sample, pre_g, post_g, wg, wu, wd, ple=None, *, name):
    tm = TOKEN_BLOCK
    d, d_ff = wg.shape
    n_prep = d_ff // WEIGHT_TILE
    n_blocks = n_prompt + n_sample
    nxt, prv = n_prep - 1, n_prep
    weights = [_resident(pre_g), _resident(post_g),
               _col_pieces(wg, WEIGHT_TILE), _col_pieces(wu, WEIGHT_TILE), _row_pieces(wd, WEIGHT_TILE)]
    scratch = [pltpu.VMEM((n_prep, d, 2 * WEIGHT_TILE), BF16),
               pltpu.VMEM((d_ff, d), BF16), pltpu.VMEM((tm, d), BF16), pltpu.VMEM((tm, d_ff), BF16)]
    if ple is None:
        xp, xs = x
        in_specs = [_stream(tm, d, nxt, n_prompt), _stream(tm, d, nxt + n_prompt, n_sample),
                    _stream(tm, d, prv, n_prompt), _stream(tm, d, prv + n_prompt, n_sample)] + weights
        args = [xp, xs, xp, xs, pre_g, post_g, wg, wu, wd]
        out_specs = _stream(tm, d, prv, n_blocks)
        out_shape = jax.ShapeDtypeStruct((n_blocks * tm, d), F32)
    else:
        pp, ps, wpg, wpp, ple_g = ple
        assert wpg.shape[0] // (WEIGHT_TILE // 2) <= n_prep
        d_p = pp.shape[1]

        def sample_first(lo):
            k = _clamped(lo, n_blocks)
            return pl.BlockSpec((tm, d), lambda i: (jnp.where(k(i) < n_sample, k(i) + n_prompt, k(i) - n_sample), 0))

        in_specs = [sample_first(nxt), sample_first(prv),
                    _stream(tm, d_p, prv + n_sample, n_prompt), _stream(tm, d_p, prv, n_sample)] + weights
        in_specs += [_row_pieces(wpg, WEIGHT_TILE // 2), _row_pieces(wpp, wpp.shape[0]), _resident(ple_g)]
        args = [x, x, pp, ps, pre_g, post_g, wg, wu, wd, wpg, wpp, ple_g]
        out_specs = [_stream(tm, d, prv + n_sample, n_prompt), _stream(tm, d, prv, n_sample)]
        out_shape = [jax.ShapeDtypeStruct((n_prompt * tm, d), F32), jax.ShapeDtypeStruct((n_sample * tm, d), F32)]
        scratch += [pltpu.VMEM(wpg.shape, BF16), pltpu.VMEM(wpp.shape, BF16), pltpu.VMEM((tm, d), BF16)]
    return pl.pallas_call(
        functools.partial(_ffn_kernel, n_prep=n_prep, n_prompt=n_prompt, n_blocks=n_blocks,
                          with_ple=ple is not None),
        grid=(n_prep + n_blocks,),
        in_specs=in_specs,
        out_specs=out_specs,
        out_shape=out_shape,
        scratch_shapes=scratch,
        compiler_params=_params(),
        name=name,
    )(*args)


def _mix(h, state, batch_p, seq_p, batch_s, seq_s, mw):
    tm = MIX_TOKEN_BLOCK
    d = h.shape[1]
    n_prep = MIX_SEGMENTS
    assert seq_p % tm == 0 and (batch_s * seq_s) % tm == 0 and tm % seq_s == 0
    n_prompt, n_sample = batch_p * seq_p // tm, batch_s * seq_s // tm
    n_blocks = n_prompt + n_sample
    blocks_per_seq = seq_p // tm
    seqs_per_block = tm // seq_s
    assert d // WEIGHT_TILE <= n_prep
    small = [mw["v_g"], mw["w_s"], mw["b_s"], mw["g_a"], mw["g_b"]]
    conv_w_spec = pl.BlockSpec((None,) + mw["conv_w"].shape[1:], lambda i: (0, 0, 0), pipeline_mode=pl.Buffered(1))
    lo_s = n_prep + n_prompt
    prompt_seq = _clamped(0, batch_p)
    per_prompt_seq = lambda i: (prompt_seq((i - n_prep) // blocks_per_seq), 0, 0)
    sample_blk = _clamped(lo_s, n_sample)
    per_sample_blk = pl.BlockSpec((seqs_per_block, CONV_K - 1, CONV_DIM), lambda i: (sample_blk(i), 0, 0))
    return pl.pallas_call(
        functools.partial(_mix_kernel, n_prep=n_prep, n_prompt=n_prompt, n_blocks=n_blocks,
                          blocks_per_seq=blocks_per_seq, seq_len=seq_s),
        grid=(n_prep + n_blocks + 1,),
        in_specs=[_stream(tm, d, n_prep, n_blocks), _stream(tm, d, n_prep + 1, n_blocks),
                  per_sample_blk, _resident(mw["pre"]), _resident(mw["post"]),
                  _col_pieces(mw["w_in"], CONV_DIM), _row_pieces(mw["w_out"], WEIGHT_TILE), conv_w_spec]
                 + [_resident(a) for a in small],
        out_specs=[_stream(tm, d, n_prep + 1, n_blocks),
                   pl.BlockSpec((1, CONV_K - 1, CONV_DIM), per_prompt_seq),
                   pl.BlockSpec((1, CHUNK_DIM, CHUNK), per_prompt_seq),
                   per_sample_blk,
                   pl.BlockSpec((seq_s, CHUNK_DIM, batch_s), lambda i: (0, 0, 0))],
        out_shape=[jax.ShapeDtypeStruct(h.shape, F32),
                   jax.ShapeDtypeStruct((batch_p, CONV_K - 1, CONV_DIM), F32),
                   jax.ShapeDtypeStruct((batch_p, CHUNK_DIM, CHUNK), F32),
                   jax.ShapeDtypeStruct((batch_s, CONV_K - 1, CONV_DIM), F32),
                   jax.ShapeDtypeStruct((seq_s, CHUNK_DIM, batch_s), F32)],
        scratch_shapes=[pltpu.VMEM((MIX_SEGMENTS, d, CONV_DIM), BF16), pltpu.VMEM(mw["w_out"].shape, BF16),
                        pltpu.VMEM((tm, d), BF16), pltpu.VMEM((tm, CONV_DIM + CHUNK_DIM), BF16),
                        pltpu.VMEM((CONV_K - 1, CONV_DIM), F32),
                        pltpu.VMEM((CHUNK_DIM // V7X_LANES, tm, V7X_LANES), F32),
                        pltpu.VMEM((seq_s, batch_s, CHUNK_DIM), F32),
                        pltpu.VMEM((V7X_MXU_COLS, V7X_MXU_COLS), BF16),
                        pltpu.VMEM((N_SLABS, CHUNK, HEADS_PER_SLAB * CHUNK), BF16),
                        pltpu.VMEM((CHUNK, CHUNK_DIM), F32)],
        compiler_params=_params(),
        name="mix",
    )(h, h, state, mw["pre"], mw["post"], mw["w_in"], mw["w_out"], mw["conv_w"], *small)


def _row(g):
    return g.reshape(1, -1).astype(F32)


def _mix_weights(mix_pre_g, mix_post_g, w_in, conv_w, v_norm_g, w_s, b_s, out_g_a, out_g_b, w_out):
    return dict(pre=_row(mix_pre_g), post=_row(mix_post_g), w_in=w_in, w_out=w_out,
                conv_w=conv_w.astype(F32), v_g=v_norm_g.astype(F32), w_s=w_s.astype(F32),
                b_s=b_s.astype(F32), g_a=_row(out_g_a), g_b=_row(out_g_b))


def kernel(x_prompt, x_sample, p_prompt, p_sample, state_conv, ffn1_pre_g, ffn1_post_g, ffn1_w_gate, ffn1_w_up, ffn1_w_down, mix_pre_g, mix_post_g, w_in, conv_w, v_norm_g, w_s, b_s, out_g_a, out_g_b, w_out, ffn2_pre_g, ffn2_post_g, ffn2_w_gate, ffn2_w_up, ffn2_w_down, ple_w_gate, ple_w_proj, ple_post_g):
    depth = ffn1_pre_g.shape[0]
    bp, lp, d = x_prompt.shape
    bs, ls, _ = x_sample.shape
    tm = TOKEN_BLOCK
    assert lp % tm == 0 and (bs * ls) % tm == 0 and tm % ls == 0
    assert CONV_K - 1 <= ls <= V7X_SUBLANES and (lp - 1) % CHUNK + 1 == CHUNK
    n_prompt, n_sample = bp * lp // tm, bs * ls // tm
    x = (x_prompt.reshape(bp * lp, d), x_sample.reshape(bs * ls, d))
    conv_p, conv_s, v_p, v_s = [], [], [], []
    for i in range(depth):
        mw = _mix_weights(mix_pre_g[i], mix_post_g[i], w_in[i], conv_w[i:i + 1], v_norm_g[i], w_s[i], b_s[i],
                          out_g_a[i], out_g_b[i], w_out[i])
        ple = (p_prompt[i].reshape(bp * lp, -1), p_sample[i].reshape(bs * ls, -1),
               ple_w_gate[i], ple_w_proj[i], _row(ple_post_g[i]))

        h = _ffn1_loop(x[0], x[1], _row(ffn1_pre_g[i]), _row(ffn1_post_g[i]),
                       ffn1_w_gate[i], ffn1_w_up[i], ffn1_w_down[i])
        h, cp, vp, cs, vs = _mix(h, state_conv[i].astype(F32), bp, lp, bs, ls, mw)
        x = _ffn(h, n_prompt, n_sample, _row(ffn2_pre_g[i]), _row(ffn2_post_g[i]),
                 ffn2_w_gate[i], ffn2_w_up[i], ffn2_w_down[i], ple, name="ffn2_ple")

        conv_p.append(cp)
        conv_s.append(cs)
        v_p.append(vp.transpose(0, 2, 1).reshape(bp, CHUNK, HEADS_B, HEAD_DIM_B))
        v_s.append(vs.transpose(2, 0, 1).reshape(bs, ls, HEADS_B, HEAD_DIM_B))
    layers = lambda parts: parts[0][None] if depth == 1 else jnp.stack(parts)
    return (x[0].reshape(bp, lp, d), x[1].reshape(bs, ls, d),
            layers(conv_p), layers(conv_s), layers(v_p), layers(v_s))
```

```python
import functools

import jax
import jax.numpy as jnp
from jax import lax
from jax.experimental import pallas as pl
from jax.experimental.pallas import tpu as pltpu

CONV_K = 3
HEADS_B = 8
HEAD_DIM_B = 64
CHUNK_DIM = HEADS_B * HEAD_DIM_B
CONV_DIM = CHUNK_DIM
CHUNK = 128
EPS = 1e-6

V7X_SUBLANES = 8
V7X_LANES = 128
V7X_MXU_COLS = 256
V7X_VMEM_LIMIT_BYTES = 60 * 1024 * 1024

TOKEN_BLOCK = 512
MIX_TOKEN_BLOCK = 512
WEIGHT_TILE = V7X_MXU_COLS
HEADS_PER_SLAB = V7X_MXU_COLS // HEAD_DIM_B
N_SLABS = CHUNK_DIM // V7X_MXU_COLS
MIX_SEGMENTS = 5
FINISH_SLICES = 8
FINISH_FIRST_CHUNK = 1
MIX_FINISH_SLICES = 4

BF16 = jnp.bfloat16
F32 = jnp.float32


def _rms(x, g):
    return x * lax.rsqrt(jnp.mean(x * x, axis=-1, keepdims=True) + EPS) * g


def _dot(a, b):
    return jnp.dot(a.astype(BF16), b, preferred_element_type=F32)


def _sigmoid(x):
    return 0.5 + 0.5 * jnp.tanh(0.5 * x)


def _silu(x):
    half = 0.5 * x
    return half + half * jnp.tanh(half)


def _after(value, anchor):
    bits = lax.bitcast_convert_type(anchor, jnp.int32)
    zero = lax.shift_right_logical(lax.shift_right_logical(bits, 16), 16)
    return value + zero.astype(value.dtype)


def _store_piece(dst_ref, src_ref, step, n_pieces):
    rows = src_ref.shape[0]

    @pl.when(step < n_pieces)
    def _():
        dst_ref[pl.ds(pl.multiple_of(step * rows, rows), rows), :] = src_ref[...].astype(BF16)


def _ffn_loop_kernel(*refs, n_prep, n_prompt, n_blocks, with_ple):
    if with_ple:
        (x_hbm, pp_hbm, ps_hbm, pre_ref, post_ref, wg_ref, wu_ref, wd_ref, wpg_ref, wpp_ref, pleg_ref,
         op_hbm, os_hbm, wgu_s, wd_s, xn_s, a_s, xbuf, obuf, in_sem, out_sem,
         wpg_s, wpp_s, hb_s, pbuf, p_sem) = refs
    else:
        (xp_hbm, xs_hbm, pre_ref, post_ref, wg_ref, wu_ref, wd_ref, o_hbm,
         wgu_s, wd_s, xn_s, a_s, xbuf, obuf, in_sem, out_sem) = refs
    i = pl.program_id(0)
    tm = xn_s.shape[0]
    n_x, n_o = xbuf.shape[0], obuf.shape[0]
    n_slices = FINISH_SLICES // 2 if with_ple else FINISH_SLICES
    slice_rows = tm // n_slices
    stages = [("residual", s) for s in range(n_slices)]
    if with_ple:
        stages += [("embed_dots", 0)] + [("embed_add", s) for s in range(n_slices)]
    assert FINISH_FIRST_CHUNK + len(stages) <= n_prep

    def rows_of(b, prompt_hbm, sample_hbm):
        if isinstance(b, int):
            return (prompt_hbm.at[pl.ds(b * tm, tm)] if b < n_prompt else None,
                    sample_hbm.at[pl.ds((b - n_prompt) * tm, tm)] if b >= n_prompt else None)
        return (prompt_hbm.at[pl.ds(pl.multiple_of(b * tm, tm), tm)],
                sample_hbm.at[pl.ds(pl.multiple_of((b - n_prompt) * tm, tm), tm)])

    def start_either(b, make_copy, prompt_hbm, sample_hbm):
        in_prompt, in_sample = rows_of(b, prompt_hbm, sample_hbm)
        if isinstance(b, int):
            make_copy(in_prompt if in_sample is None else in_sample).start()
            return

        @pl.when(b < n_prompt)
        def _():
            make_copy(in_prompt).start()

        @pl.when(b >= n_prompt)
        def _():
            make_copy(in_sample).start()

    def x_copy(b):
        return lambda src: pltpu.make_async_copy(src, xbuf.at[b % n_x], in_sem.at[b % n_x])

    def start_x(b):
        if with_ple:
            row0 = b * tm if isinstance(b, int) else pl.multiple_of(b * tm, tm)
            x_copy(b)(x_hbm.at[pl.ds(row0, tm)]).start()
        else:
            start_either(b, x_copy(b), xp_hbm, xs_hbm)

    def wait_x(b):
        x_copy(b)((x_hbm if with_ple else xp_hbm).at[pl.ds(0, tm)]).wait()

    def p_copy(b):
        return lambda src: pltpu.make_async_copy(src, pbuf.at[b % 2], p_sem.at[b % 2])

    def out_copy(b):
        return lambda dst: pltpu.make_async_copy(obuf.at[b % n_o], dst, out_sem.at[b % n_o])

    def start_out(b):
        if with_ple:
            start_either(b, out_copy(b), op_hbm, os_hbm)
        else:
            row0 = b * tm if isinstance(b, int) else pl.multiple_of(b * tm, tm)
            out_copy(b)(o_hbm.at[pl.ds(row0, tm)]).start()

    def wait_out(b):
        out_copy(b)((op_hbm if with_ple else o_hbm).at[pl.ds(0, tm)]).wait()

    def gate_up_chunk(c):
        gu = jnp.dot(xn_s[...], wgu_s[c], preferred_element_type=F32)
        g, u = gu[:, 0:WEIGHT_TILE], gu[:, WEIGHT_TILE:]
        return g, (_silu(g) * u).astype(BF16)

    def down():
        y = jnp.dot(a_s[...], wd_s[...], preferred_element_type=F32)
        return dict(y=y, r=lax.rsqrt(jnp.mean(y * y, axis=-1, keepdims=True) + EPS))

    def ordered(r, rows, anchor):
        return r[rows] if anchor is None else _after(r[rows], anchor[rows, 0:1])

    def finish_stage(b, stage, s, st, anchor=None):
        rows = slice(s * slice_rows, (s + 1) * slice_rows)
        o = b % n_o
        if stage == "residual":
            h = xbuf[b % n_x, rows, :] + 0.5 * (st["y"][rows] * ordered(st["r"], rows, anchor) * post_ref[...])
            obuf[o, rows, :] = h
            if with_ple:
                hb_s[rows, :] = h.astype(BF16)
        elif stage == "embed_dots":
            gate = _sigmoid(jnp.dot(hb_s[...], wpg_s[...], preferred_element_type=F32))
            st["e"] = gate * _dot(pbuf[b % 2], wpp_s[...])
            st["r2"] = lax.rsqrt(jnp.mean(st["e"] * st["e"], axis=-1, keepdims=True) + EPS)
        else:
            obuf[o, rows, :] = obuf[o, rows, :] + st["e"][rows] * ordered(st["r2"], rows, anchor) * pleg_ref[...]

    @pl.when(i == 0)
    def _():
        start_x(0)
        if with_ple:
            start_either(0, p_copy(0), pp_hbm, ps_hbm)
        wait_x(0)
        if n_blocks > 1:
            start_x(1)
        xn_s[...] = _rms(xbuf[0], pre_ref[...]).astype(BF16)

    @pl.when(i < n_prep)
    def _():
        wgu_s[i, :, 0:WEIGHT_TILE] = wg_ref[...].astype(BF16)
        wgu_s[i, :, WEIGHT_TILE:] = wu_ref[...].astype(BF16)
        _, a = gate_up_chunk(i)
        for c in range(n_prep):
            @pl.when(i == c)
            def _():
                a_s[:, c * WEIGHT_TILE:(c + 1) * WEIGHT_TILE] = a

    _store_piece(wd_s, wd_ref, i, n_prep)
    if with_ple:
        _store_piece(wpg_s, wpg_ref, i, wpg_s.shape[0] // wpg_ref.shape[0])
        _store_piece(wpp_s, wpp_ref, i, 1)

    @pl.when(i == n_prep)
    def _():
        def step(b, carry):
            wait_x(b + 1)
            if with_ple:
                p_copy(b)(pp_hbm.at[pl.ds(0, tm)]).wait()
                start_either(b + 1, p_copy(b + 1), pp_hbm, ps_hbm)

            @pl.when(b + 2 < n_blocks)
            def _():
                start_x(b + 2)

            @pl.when(b >= n_o)
            def _():
                wait_out(b - n_o)

            st = down()
            x = _after(xbuf[(b + 1) % n_x], st["y"][:, 0:1])
            xn_s[...] = _rms(x, pre_ref[...]).astype(BF16)
            for c in range(n_prep):
                g, a_s[:, c * WEIGHT_TILE:(c + 1) * WEIGHT_TILE] = gate_up_chunk(c)
                k = c - FINISH_FIRST_CHUNK
                if 0 <= k < len(stages):
                    finish_stage(b, *stages[k], st, anchor=g)
            start_out(b)
            return carry

        lax.fori_loop(0, n_blocks - 1, step, 0)
        last = n_blocks - 1
        if with_ple:
            p_copy(last)(pp_hbm.at[pl.ds(0, tm)]).wait()
        if last >= n_o:
            wait_out(last - n_o)
        st = down()
        for stage, s in stages:
            finish_stage(last, stage, s, st)
        start_out(last)
        for b in range(max(last - n_o + 1, 0), last + 1):
            wait_out(b)


def _ffn_loop(x, pre_g, post_g, wg, wu, wd, ple=None, *, n_prompt, n_sample, name):
    tm = TOKEN_BLOCK
    d, d_ff = wg.shape
    n_prep = d_ff // WEIGHT_TILE
    n_blocks = n_prompt + n_sample
    assert n_prompt >= 1
    anywhere = pl.BlockSpec(memory_space=pl.ANY)
    weights = [_resident(pre_g), _resident(post_g),
               _col_pieces(wg, WEIGHT_TILE), _col_pieces(wu, WEIGHT_TILE), _row_pieces(wd, WEIGHT_TILE)]
    scratch = [pltpu.VMEM((n_prep, d, 2 * WEIGHT_TILE), BF16), pltpu.VMEM((d_ff, d), BF16),
               pltpu.VMEM((tm, d), BF16), pltpu.VMEM((tm, d_ff), BF16),
               pltpu.VMEM((3, tm, d), F32), pltpu.VMEM((2, tm, d), F32),
               pltpu.SemaphoreType.DMA((3,)), pltpu.SemaphoreType.DMA((2,))]
    if ple is None:
        in_specs = [anywhere, anywhere] + weights
        args = [x[0], x[1], pre_g, post_g, wg, wu, wd]
        out_specs = anywhere
        out_shape = jax.ShapeDtypeStruct((n_blocks * tm, d), F32)
    else:
        pp, ps, wpg, wpp, ple_g = ple
        assert wpg.shape[0] // (WEIGHT_TILE // 2) <= n_prep
        in_specs = [anywhere, anywhere, anywhere] + weights
        in_specs += [_row_pieces(wpg, WEIGHT_TILE // 2), _row_pieces(wpp, wpp.shape[0]), _resident(ple_g)]
        args = [x, pp, ps, pre_g, post_g, wg, wu, wd, wpg, wpp, ple_g]
        out_specs = [anywhere, anywhere]
        out_shape = [jax.ShapeDtypeStruct((n_prompt * tm, d), F32), jax.ShapeDtypeStruct((n_sample * tm, d), F32)]
        scratch += [pltpu.VMEM(wpg.shape, BF16), pltpu.VMEM(wpp.shape, BF16), pltpu.VMEM((tm, d), BF16),
                    pltpu.VMEM((2, tm, pp.shape[1]), F32), pltpu.SemaphoreType.DMA((2,))]
    return pl.pallas_call(
        functools.partial(_ffn_loop_kernel, n_prep=n_prep, n_prompt=n_prompt, n_blocks=n_blocks,
                          with_ple=ple is not None),
        grid=(n_prep + 1,),
        in_specs=in_specs,
        out_specs=out_specs,
        out_shape=out_shape,
        scratch_shapes=scratch,
        compiler_params=_params(),
        name=name,
    )(*args)


def _head_rms(v, ones_slab, g):
    vv = (v * v).astype(BF16)
    ms = jnp.concatenate(
        [jnp.dot(vv[:, k * V7X_MXU_COLS:(k + 1) * V7X_MXU_COLS], ones_slab, preferred_element_type=F32)
         for k in range(CHUNK_DIM // V7X_MXU_COLS)], axis=1)
    return v * lax.rsqrt(ms + EPS) * g


def _mix_kernel(hc_ref, hv_ref, state_ref, pre_ref, post_ref, win_ref, wout_ref, convw_ref, vg_ref,
                ws_ref, b_ref, ga_ref, gb_ref,
                o_ref, convp_ref, vp_ref, convs_ref, vs_ref,
                win_s, wout_s, n_s, yab_s, carry_ref, vn_s, vt_s, ones_ref, wslab_s, bias_s,
                *, n_prep, n_prompt, n_blocks, blocks_per_seq, seq_len):
    tm = hc_ref.shape[0]
    i = pl.program_id(0)
    j = i - n_prep

    @pl.when(i < n_prep)
    def _():
        win_s[i] = win_ref[...].astype(BF16)

    _store_piece(wout_s, wout_ref, i, wout_s.shape[0] // wout_ref.shape[0])

    @pl.when(i == 0)
    def _():
        yab_s[...] = jnp.zeros_like(yab_s)
        carry_ref[...] = jnp.zeros_like(carry_ref)
        row_head = lax.broadcasted_iota(jnp.int32, ones_ref.shape, 0) // HEAD_DIM_B
        col_head = lax.broadcasted_iota(jnp.int32, ones_ref.shape, 1) // HEAD_DIM_B
        ones_ref[...] = jnp.where(row_head == col_head, 1.0 / HEAD_DIM_B, 0.0).astype(BF16)

    assert MIX_FINISH_SLICES <= MIX_SEGMENTS
    slice_rows = tm // MIX_FINISH_SLICES

    def out_proj_prev():
        y = jnp.dot(yab_s[...], wout_s[...], preferred_element_type=F32)
        return dict(y=y, r=lax.rsqrt(jnp.mean(y * y, axis=-1, keepdims=True) + EPS), done=0)

    def finish_slice(prev, anchor=None):
        rows = slice(prev["done"] * slice_rows, (prev["done"] + 1) * slice_rows)
        prev["done"] += 1
        r = prev["r"][rows] if anchor is None else _after(prev["r"][rows], anchor[rows, 0:1])
        o_ref[rows, :] = hv_ref[rows, :] + prev["y"][rows] * r * post_ref[...]

    def seg(k, prev):
        out = jnp.dot(n_s[...], win_s[k], preferred_element_type=F32)
        if prev["done"] < MIX_FINISH_SLICES:
            finish_slice(prev, anchor=out)
        return out

    def conv(z, z1, z2):
        return convw_ref[0:1, :] * z2 + convw_ref[1:2, :] * z1 + convw_ref[2:3, :] * z

    def per_head(piece, rows):
        return jnp.concatenate([jnp.broadcast_to(piece(h), (rows, HEAD_DIM_B)) for h in range(HEADS_B)], axis=1)

    def v_gain():
        return jnp.concatenate([vg_ref[h:h + 1, :] for h in range(HEADS_B)], axis=1)

    def chunk_bias(rows):
        b_t = b_ref[...].T
        return per_head(lambda h: b_t[0:rows, h:h + 1], rows)

    @pl.when(i == 0)
    def _():
        t_idx = lax.broadcasted_iota(jnp.int32, (CHUNK, HEADS_PER_SLAB * CHUNK), 0)
        s_idx = lax.broadcasted_iota(jnp.int32, (CHUNK, HEADS_PER_SLAB * CHUNK), 1) % CHUNK
        for k in range(N_SLABS):
            w_k = jnp.concatenate([ws_ref[k * HEADS_PER_SLAB + hd] for hd in range(HEADS_PER_SLAB)], axis=1)
            wslab_s[k] = jnp.where(t_idx >= s_idx, w_k, 0.0).astype(BF16)
        bias_s[...] = chunk_bias(CHUNK)

    def group_norms(conv_out, mixed, prev):
        yab_s[:, 0:CONV_DIM] = _rms(seg(0, prev) * conv_out, ga_ref[...]).astype(BF16)
        yab_s[:, CONV_DIM:] = _rms(seg(3, prev) * mixed, gb_ref[...]).astype(BF16)

    def prompt_block():
        prev = out_proj_prev()
        n_s[...] = _rms(hc_ref[...], pre_ref[...]).astype(BF16)

        v_n = _head_rms(seg(4, prev), ones_ref[...], v_gain())
        vp_ref[0] = v_n[tm - CHUNK:, :].T
        lane_head = lax.broadcasted_iota(jnp.int32, (CHUNK, V7X_MXU_COLS), 1) // HEAD_DIM_B
        chunks = []
        for c in range(tm // CHUNK):
            slabs = []
            for k in range(N_SLABS):
                cols = slice(k * V7X_MXU_COLS, (k + 1) * V7X_MXU_COLS)
                v_ck = v_n[c * CHUNK:(c + 1) * CHUNK, cols].astype(BF16)
                bd = jnp.concatenate(
                    [jnp.where(lane_head == hd, v_ck, jnp.zeros((), BF16)) for hd in range(HEADS_PER_SLAB)], axis=0)
                slabs.append(jnp.dot(wslab_s[k], bd, preferred_element_type=F32))
            chunks.append(jnp.concatenate(slabs, axis=1) + bias_s[...])
        mixed = jnp.concatenate(chunks, axis=0)

        z = seg(1, prev) * seg(2, prev)
        row = lax.broadcasted_iota(jnp.int32, z.shape, 0)
        carry = jnp.where(j % blocks_per_seq == 0, 0.0, carry_ref[...])
        prev1 = carry[1:2, :]
        prev2 = carry[0:1, :]
        z1 = jnp.where(row == 0, prev1, pltpu.roll(z, 1, axis=0))
        z2 = jnp.where(row == 0, prev2, jnp.where(row == 1, prev1, pltpu.roll(z, 2, axis=0)))
        tail = z[tm - (CONV_K - 1):, :]
        carry_ref[...] = tail
        convp_ref[0] = tail
        group_norms(conv(z, z1, z2), mixed, prev)

    def sample_block():
        prev = out_proj_prev()
        n_s[...] = _rms(hc_ref[...], pre_ref[...]).astype(BF16)

        n_seq = tm // seq_len
        per_seq = lambda a: a.reshape(n_seq, seq_len, a.shape[-1])
        every_row = lambda a: jnp.broadcast_to(a, (n_seq, seq_len, a.shape[-1]))
        v_n = _head_rms(seg(4, prev), ones_ref[...], v_gain())
        first_seq = pl.multiple_of((j - n_prompt) * n_seq, n_seq)
        for c in range(CHUNK_DIM // V7X_LANES):
            lanes = slice(c * V7X_LANES, (c + 1) * V7X_LANES)
            vn_s[c] = v_n[:, lanes]
            for t in range(seq_len):
                vt_s[t, pl.ds(first_seq, n_seq), lanes] = vn_s[c, pl.ds(t, n_seq, stride=seq_len), :]

        @pl.when(j == n_blocks - 1)
        def _():
            for t in range(seq_len):
                vs_ref[t] = vt_s[t].T
        v_seq = per_seq(v_n)
        t_row = lax.broadcasted_iota(jnp.int32, (seq_len, CHUNK_DIM), 0)
        mixed = every_row(chunk_bias(seq_len)[None])
        for s in range(seq_len):
            w_from_s = per_head(lambda h: ws_ref[h, 0:seq_len, s:s + 1], seq_len)
            mixed = mixed + jnp.where(t_row >= s, w_from_s, 0.0)[None] * every_row(v_seq[:, s:s + 1, :])
        mixed = mixed.reshape(tm, CHUNK_DIM)

        z = seg(1, prev) * seg(2, prev)
        t = lax.broadcasted_iota(jnp.int32, z.shape, 0) % seq_len
        st = state_ref[...]
        prev1 = every_row(st[:, 1:2, :]).reshape(tm, CONV_DIM)
        prev2 = every_row(st[:, 0:1, :]).reshape(tm, CONV_DIM)
        z1 = jnp.where(t == 0, prev1, pltpu.roll(z, 1, axis=0))
        z2 = jnp.where(t == 0, prev2, jnp.where(t == 1, prev1, pltpu.roll(z, 2, axis=0)))
        convs_ref[...] = per_seq(z)[:, seq_len - (CONV_K - 1):, :]
        group_norms(conv(z, z1, z2), mixed, prev)

    @pl.when(jnp.logical_and(j >= 0, j < n_prompt))
    def _():
        prompt_block()

    @pl.when(jnp.logical_and(j >= n_prompt, j < n_blocks))
    def _():
        sample_block()

    @pl.when(j == n_blocks)
    def _():
        prev = out_proj_prev()
        for _ in range(MIX_FINISH_SLICES):
            finish_slice(prev)


def _resident(a):
    return pl.BlockSpec(a.shape, lambda i: (0,) * a.ndim, pipeline_mode=pl.Buffered(1))


def _clamped(lo, n):
    return lambda i: jnp.minimum(jnp.maximum(i - lo, 0), n - 1)


def _stream(tm, d, lo, n):
    blk = _clamped(lo, n)
    return pl.BlockSpec((tm, d), lambda i: (blk(i), 0))


def _row_pieces(w, rows):
    blk = _clamped(0, w.shape[0] // rows)
    return pl.BlockSpec((rows, w.shape[1]), lambda i: (blk(i), 0))


def _col_pieces(w, cols):
    blk = _clamped(0, w.shape[1] // cols)
    return pl.BlockSpec((w.shape[0], cols), lambda i: (0, blk(i)))


def _params():
    return pltpu.CompilerParams(dimension_semantics=("arbitrary",), vmem_limit_bytes=V7X_VMEM_LIMIT_BYTES)


def _mix(h, state, batch_p, seq_p, batch_s, seq_s, mw):
    tm = MIX_TOKEN_BLOCK
    d = h.shape[1]
    n_prep = MIX_SEGMENTS
    assert seq_p % tm == 0 and (batch_s * seq_s) % tm == 0 and tm % seq_s == 0
    n_prompt, n_sample = batch_p * seq_p // tm, batch_s * seq_s // tm
    n_blocks = n_prompt + n_sample
    blocks_per_seq = seq_p // tm
    seqs_per_block = tm // seq_s
    assert d // WEIGHT_TILE <= n_prep
    small = [mw["v_g"], mw["w_s"], mw["b_s"], mw["g_a"], mw["g_b"]]
    conv_w_spec = pl.BlockSpec((None,) + mw["conv_w"].shape[1:], lambda i: (0, 0, 0), pipeline_mode=pl.Buffered(1))
    lo_s = n_prep + n_prompt
    prompt_seq = _clamped(0, batch_p)
    per_prompt_seq = lambda i: (prompt_seq((i - n_prep) // blocks_per_seq), 0, 0)
    sample_blk = _clamped(lo_s, n_sample)
    per_sample_blk = pl.BlockSpec((seqs_per_block, CONV_K - 1, CONV_DIM), lambda i: (sample_blk(i), 0, 0))
    return pl.pallas_call(
        functools.partial(_mix_kernel, n_prep=n_prep, n_prompt=n_prompt, n_blocks=n_blocks,
                          blocks_per_seq=blocks_per_seq, seq_len=seq_s),
        grid=(n_prep + n_blocks + 1,),
        in_specs=[_stream(tm, d, n_prep, n_blocks), _stream(tm, d, n_prep + 1, n_blocks),
                  per_sample_blk, _resident(mw["pre"]), _resident(mw["post"]),
                  _col_pieces(mw["w_in"], CONV_DIM), _row_pieces(mw["w_out"], WEIGHT_TILE), conv_w_spec]
                 + [_resident(a) for a in small],
        out_specs=[_stream(tm, d, n_prep + 1, n_blocks),
                   pl.BlockSpec((1, CONV_K - 1, CONV_DIM), per_prompt_seq),
                   pl.BlockSpec((1, CHUNK_DIM, CHUNK), per_prompt_seq),
                   per_sample_blk,
                   pl.BlockSpec((seq_s, CHUNK_DIM, batch_s), lambda i: (0, 0, 0))],
        out_shape=[jax.ShapeDtypeStruct(h.shape, F32),
                   jax.ShapeDtypeStruct((batch_p, CONV_K - 1, CONV_DIM), F32),
                   jax.ShapeDtypeStruct((batch_p, CHUNK_DIM, CHUNK), F32),
                   jax.ShapeDtypeStruct((batch_s, CONV_K - 1, CONV_DIM), F32),
                   jax.ShapeDtypeStruct((seq_s, CHUNK_DIM, batch_s), F32)],
        scratch_shapes=[pltpu.VMEM((MIX_SEGMENTS, d, CONV_DIM), BF16), pltpu.VMEM(mw["w_out"].shape, BF16),
                        pltpu.VMEM((tm, d), BF16), pltpu.VMEM((tm, CONV_DIM + CHUNK_DIM), BF16),
                        pltpu.VMEM((CONV_K - 1, CONV_DIM), F32),
                        pltpu.VMEM((CHUNK_DIM // V7X_LANES, tm, V7X_LANES), F32),
                        pltpu.VMEM((seq_s, batch_s, CHUNK_DIM), F32),
                        pltpu.VMEM((V7X_MXU_COLS, V7X_MXU_COLS), BF16),
                        pltpu.VMEM((N_SLABS, CHUNK, HEADS_PER_SLAB * CHUNK), BF16),
                        pltpu.VMEM((CHUNK, CHUNK_DIM), F32)],
        compiler_params=_params(),
        name="mix",
    )(h, h, state, mw["pre"], mw["post"], mw["w_in"], mw["w_out"], mw["conv_w"], *small)


def _row(g):
    return g.reshape(1, -1).astype(F32)


def _mix_weights(mix_pre_g, mix_post_g, w_in, conv_w, v_norm_g, w_s, b_s, out_g_a, out_g_b, w_out):
    return dict(pre=_row(mix_pre_g), post=_row(mix_post_g), w_in=w_in, w_out=w_out,
                conv_w=conv_w.astype(F32), v_g=v_norm_g.astype(F32), w_s=w_s.astype(F32),
                b_s=b_s.astype(F32), g_a=_row(out_g_a), g_b=_row(out_g_b))


def kernel(x_prompt, x_sample, p_prompt, p_sample, state_conv, ffn1_pre_g, ffn1_post_g, ffn1_w_gate, ffn1_w_up, ffn1_w_down, mix_pre_g, mix_post_g, w_in, conv_w, v_norm_g, w_s, b_s, out_g_a, out_g_b, w_out, ffn2_pre_g, ffn2_post_g, ffn2_w_gate, ffn2_w_up, ffn2_w_down, ple_w_gate, ple_w_proj, ple_post_g):
    depth = ffn1_pre_g.shape[0]
    bp, lp, d = x_prompt.shape
    bs, ls, _ = x_sample.shape
    tm = TOKEN_BLOCK
    assert lp % tm == 0 and (bs * ls) % tm == 0 and tm % ls == 0
    assert CONV_K - 1 <= ls <= V7X_SUBLANES and (lp - 1) % CHUNK + 1 == CHUNK
    n_prompt, n_sample = bp * lp // tm, bs * ls // tm
    x = (x_prompt.reshape(bp * lp, d), x_sample.reshape(bs * ls, d))
    conv_p, conv_s, v_p, v_s = [], [], [], []
    for i in range(depth):
        mw = _mix_weights(mix_pre_g[i], mix_post_g[i], w_in[i], conv_w[i:i + 1], v_norm_g[i], w_s[i], b_s[i],
                          out_g_a[i], out_g_b[i], w_out[i])
        ple = (p_prompt[i].reshape(bp * lp, -1), p_sample[i].reshape(bs * ls, -1),
               ple_w_gate[i], ple_w_proj[i], _row(ple_post_g[i]))

        blocks = dict(n_prompt=n_prompt, n_sample=n_sample)
        h = _ffn_loop(x, _row(ffn1_pre_g[i]), _row(ffn1_post_g[i]),
                      ffn1_w_gate[i], ffn1_w_up[i], ffn1_w_down[i], name="ffn1", **blocks)
        h, cp, vp, cs, vs = _mix(h, state_conv[i].astype(F32), bp, lp, bs, ls, mw)
        x = _ffn_loop(h, _row(ffn2_pre_g[i]), _row(ffn2_post_g[i]),
                      ffn2_w_gate[i], ffn2_w_up[i], ffn2_w_down[i], ple, name="ffn2_ple", **blocks)

        conv_p.append(cp)
        conv_s.append(cs)
        v_p.append(vp.transpose(0, 2, 1).reshape(bp, CHUNK, HEADS_B, HEAD_DIM_B))
        v_s.append(vs.transpose(2, 0, 1).reshape(bs, ls, HEADS_B, HEAD_DIM_B))
    layers = lambda parts: parts[0][None] if depth == 1 else jnp.stack(parts)
    return (x[0].reshape(bp, lp, d), x[1].reshape(bs, ls, d),
            layers(conv_p), layers(conv_s), layers(v_p), layers(v_s))
```

```python
import functools

import jax
import jax.numpy as jnp
from jax import lax
from jax.experimental import pallas as pl
from jax.experimental.pallas import tpu as pltpu

CONV_K = 3
HEADS_B = 8
HEAD_DIM_B = 64
CHUNK_DIM = HEADS_B * HEAD_DIM_B
CONV_DIM = CHUNK_DIM
CHUNK = 128
EPS = 1e-6

V7X_SUBLANES = 8
V7X_LANES = 128
V7X_MXU_COLS = 256
V7X_VMEM_LIMIT_BYTES = 60 * 1024 * 1024

TOKEN_BLOCK = 512
MIX_TOKEN_BLOCK = 512
WEIGHT_TILE = V7X_MXU_COLS
HEADS_PER_SLAB = V7X_MXU_COLS // HEAD_DIM_B
N_SLABS = CHUNK_DIM // V7X_MXU_COLS
MIX_SEGMENTS = 5
FINISH_SLICES = 8
FINISH_FIRST_CHUNK = 1
MIX_FINISH_SLICES = 4

BF16 = jnp.bfloat16
F32 = jnp.float32


def _rms(x, g):
    return x * lax.rsqrt(jnp.mean(x * x, axis=-1, keepdims=True) + EPS) * g


def _dot(a, b):
    return jnp.dot(a.astype(BF16), b, preferred_element_type=F32)


def _sigmoid(x):
    return 0.5 + 0.5 * jnp.tanh(0.5 * x)


def _silu(x):
    half = 0.5 * x
    return half + half * jnp.tanh(half)


def _after(value, anchor):
    bits = lax.bitcast_convert_type(anchor, jnp.int32)
    zero = lax.shift_right_logical(lax.shift_right_logical(bits, 16), 16)
    return value + zero.astype(value.dtype)


def _store_piece(dst_ref, src_ref, step, n_pieces):
    rows = src_ref.shape[0]

    @pl.when(step < n_pieces)
    def _():
        dst_ref[pl.ds(pl.multiple_of(step * rows, rows), rows), :] = src_ref[...].astype(BF16)


def _ffn_loop_kernel(*refs, n_prep, n_prompt, n_blocks, with_ple):
    if with_ple:
        (x_hbm, pp_hbm, ps_hbm, pre_ref, post_ref, wg_ref, wu_ref, wd_ref, wpg_ref, wpp_ref, pleg_ref,
         op_hbm, os_hbm, wgu_s, wd_s, xn_s, a_s, xbuf, obuf, in_sem, out_sem,
         wpg_s, wpp_s, hb_s, pbuf, p_sem) = refs
    else:
        (xp_hbm, xs_hbm, pre_ref, post_ref, wg_ref, wu_ref, wd_ref, o_hbm,
         wgu_s, wd_s, xn_s, a_s, xbuf, obuf, in_sem, out_sem) = refs
    i = pl.program_id(0)
    tm = xn_s.shape[0]
    n_x, n_o = xbuf.shape[0], obuf.shape[0]
    n_slices = FINISH_SLICES // 2 if with_ple else FINISH_SLICES
    slice_rows = tm // n_slices
    stages = [("residual", s) for s in range(n_slices)]
    if with_ple:
        stages += [("embed_dots", 0)] + [("embed_add", s) for s in range(n_slices)]
    assert FINISH_FIRST_CHUNK + len(stages) <= n_prep

    def rows_of(b, prompt_hbm, sample_hbm):
        if isinstance(b, int):
            return (prompt_hbm.at[pl.ds(b * tm, tm)] if b < n_prompt else None,
                    sample_hbm.at[pl.ds((b - n_prompt) * tm, tm)] if b >= n_prompt else None)
        return (prompt_hbm.at[pl.ds(pl.multiple_of(b * tm, tm), tm)],
                sample_hbm.at[pl.ds(pl.multiple_of((b - n_prompt) * tm, tm), tm)])

    def start_either(b, make_copy, prompt_hbm, sample_hbm):
        in_prompt, in_sample = rows_of(b, prompt_hbm, sample_hbm)
        if isinstance(b, int):
            make_copy(in_prompt if in_sample is None else in_sample).start()
            return

        @pl.when(b < n_prompt)
        def _():
            make_copy(in_prompt).start()

        @pl.when(b >= n_prompt)
        def _():
            make_copy(in_sample).start()

    def x_copy(b):
        return lambda src: pltpu.make_async_copy(src, xbuf.at[b % n_x], in_sem.at[b % n_x])

    def start_x(b):
        if with_ple:
            row0 = b * tm if isinstance(b, int) else pl.multiple_of(b * tm, tm)
            x_copy(b)(x_hbm.at[pl.ds(row0, tm)]).start()
        else:
            start_either(b, x_copy(b), xp_hbm, xs_hbm)

    def wait_x(b):
        x_copy(b)((x_hbm if with_ple else xp_hbm).at[pl.ds(0, tm)]).wait()

    def p_copy(b):
        return lambda src: pltpu.make_async_copy(src, pbuf.at[b % 2], p_sem.at[b % 2])

    def out_copy(b):
        return lambda dst: pltpu.make_async_copy(obuf.at[b % n_o], dst, out_sem.at[b % n_o])

    def start_out(b):
        if with_ple:
            start_either(b, out_copy(b), op_hbm, os_hbm)
        else:
            row0 = b * tm if isinstance(b, int) else pl.multiple_of(b * tm, tm)
            out_copy(b)(o_hbm.at[pl.ds(row0, tm)]).start()

    def wait_out(b):
        out_copy(b)((op_hbm if with_ple else o_hbm).at[pl.ds(0, tm)]).wait()

    def gate_up_chunk(c):
        gu = jnp.dot(xn_s[...], wgu_s[c], preferred_element_type=F32)
        g, u = gu[:, 0:WEIGHT_TILE], gu[:, WEIGHT_TILE:]
        return g, (_silu(g) * u).astype(BF16)

    def down():
        y = jnp.dot(a_s[...], wd_s[...], preferred_element_type=F32)
        return dict(y=y, r=lax.rsqrt(jnp.mean(y * y, axis=-1, keepdims=True) + EPS))

    def ordered(r, rows, anchor):
        return r[rows] if anchor is None else _after(r[rows], anchor[rows, 0:1])

    def finish_stage(b, stage, s, st, anchor=None):
        rows = slice(s * slice_rows, (s + 1) * slice_rows)
        o = b % n_o
        if stage == "residual":
            h = xbuf[b % n_x, rows, :] + 0.5 * (st["y"][rows] * ordered(st["r"], rows, anchor) * post_ref[...])
            obuf[o, rows, :] = h
            if with_ple:
                hb_s[rows, :] = h.astype(BF16)
        elif stage == "embed_dots":
            gate = _sigmoid(jnp.dot(hb_s[...], wpg_s[...], preferred_element_type=F32))
            st["e"] = gate * _dot(pbuf[b % 2], wpp_s[...])
            st["r2"] = lax.rsqrt(jnp.mean(st["e"] * st["e"], axis=-1, keepdims=True) + EPS)
        else:
            obuf[o, rows, :] = obuf[o, rows, :] + st["e"][rows] * ordered(st["r2"], rows, anchor) * pleg_ref[...]

    @pl.when(i == 0)
    def _():
        start_x(0)
        if with_ple:
            start_either(0, p_copy(0), pp_hbm, ps_hbm)
        wait_x(0)
        if n_blocks > 1:
            start_x(1)
        xn_s[...] = _rms(xbuf[0], pre_ref[...]).astype(BF16)

    @pl.when(i < n_prep)
    def _():
        wgu_s[i, :, 0:WEIGHT_TILE] = wg_ref[...].astype(BF16)
        wgu_s[i, :, WEIGHT_TILE:] = wu_ref[...].astype(BF16)
        _, a = gate_up_chunk(i)
        for c in range(n_prep):
            @pl.when(i == c)
            def _():
                a_s[:, c * WEIGHT_TILE:(c + 1) * WEIGHT_TILE] = a

    _store_piece(wd_s, wd_ref, i, n_prep)
    if with_ple:
        _store_piece(wpg_s, wpg_ref, i, wpg_s.shape[0] // wpg_ref.shape[0])
        _store_piece(wpp_s, wpp_ref, i, 1)

    @pl.when(i == n_prep)
    def _():
        def step(b, carry):
            wait_x(b + 1)
            if with_ple:
                p_copy(b)(pp_hbm.at[pl.ds(0, tm)]).wait()
                start_either(b + 1, p_copy(b + 1), pp_hbm, ps_hbm)

            @pl.when(b + 2 < n_blocks)
            def _():
                start_x(b + 2)

            @pl.when(b >= n_o)
            def _():
                wait_out(b - n_o)

            st = down()
            x = _after(xbuf[(b + 1) % n_x], st["y"][:, 0:1])
            xn_s[...] = _rms(x, pre_ref[...]).astype(BF16)
            for c in range(n_prep):
                g, a_s[:, c * WEIGHT_TILE:(c + 1) * WEIGHT_TILE] = gate_up_chunk(c)
                k = c - FINISH_FIRST_CHUNK
                if 0 <= k < len(stages):
                    finish_stage(b, *stages[k], st, anchor=g)
            start_out(b)
            return carry

        lax.fori_loop(0, n_blocks - 1, step, 0)
        last = n_blocks - 1
        if with_ple:
            p_copy(last)(pp_hbm.at[pl.ds(0, tm)]).wait()
        if last >= n_o:
            wait_out(last - n_o)
        st = down()
        for stage, s in stages:
            finish_stage(last, stage, s, st)
        start_out(last)
        for b in range(max(last - n_o + 1, 0), last + 1):
            wait_out(b)


def _ffn_loop(x, pre_g, post_g, wg, wu, wd, ple=None, *, n_prompt, n_sample, name):
    tm = TOKEN_BLOCK
    d, d_ff = wg.shape
    n_prep = d_ff // WEIGHT_TILE
    n_blocks = n_prompt + n_sample
    assert n_prompt >= 1
    anywhere = pl.BlockSpec(memory_space=pl.ANY)
    weights = [_resident(pre_g), _resident(post_g),
               _col_pieces(wg, WEIGHT_TILE), _col_pieces(wu, WEIGHT_TILE), _row_pieces(wd, WEIGHT_TILE)]
    scratch = [pltpu.VMEM((n_prep, d, 2 * WEIGHT_TILE), BF16), pltpu.VMEM((d_ff, d), BF16),
               pltpu.VMEM((tm, d), BF16), pltpu.VMEM((tm, d_ff), BF16),
               pltpu.VMEM((3, tm, d), F32), pltpu.VMEM((2, tm, d), F32),
               pltpu.SemaphoreType.DMA((3,)), pltpu.SemaphoreType.DMA((2,))]
    if ple is None:
        in_specs = [anywhere, anywhere] + weights
        args = [x[0], x[1], pre_g, post_g, wg, wu, wd]
        out_specs = anywhere
        out_shape = jax.ShapeDtypeStruct((n_blocks * tm, d), F32)
    else:
        pp, ps, wpg, wpp, ple_g = ple
        assert wpg.shape[0] // (WEIGHT_TILE // 2) <= n_prep
        in_specs = [anywhere, anywhere, anywhere] + weights
        in_specs += [_row_pieces(wpg, WEIGHT_TILE // 2), _row_pieces(wpp, wpp.shape[0]), _resident(ple_g)]
        args = [x, pp, ps, pre_g, post_g, wg, wu, wd, wpg, wpp, ple_g]
        out_specs = [anywhere, anywhere]
        out_shape = [jax.ShapeDtypeStruct((n_prompt * tm, d), F32), jax.ShapeDtypeStruct((n_sample * tm, d), F32)]
        scratch += [pltpu.VMEM(wpg.shape, BF16), pltpu.VMEM(wpp.shape, BF16), pltpu.VMEM((tm, d), BF16),
                    pltpu.VMEM((2, tm, pp.shape[1]), F32), pltpu.SemaphoreType.DMA((2,))]
    return pl.pallas_call(
        functools.partial(_ffn_loop_kernel, n_prep=n_prep, n_prompt=n_prompt, n_blocks=n_blocks,
                          with_ple=ple is not None),
        grid=(n_prep + 1,),
        in_specs=in_specs,
        out_specs=out_specs,
        out_shape=out_shape,
        scratch_shapes=scratch,
        compiler_params=_params(),
        name=name,
    )(*args)


def _head_rms(v, ones_slab, g):
    vv = (v * v).astype(BF16)
    ms = jnp.concatenate(
        [jnp.dot(vv[:, k * V7X_MXU_COLS:(k + 1) * V7X_MXU_COLS], ones_slab, preferred_element_type=F32)
         for k in range(CHUNK_DIM // V7X_MXU_COLS)], axis=1)
    return v * lax.rsqrt(ms + EPS) * g


def _mix_kernel(h_hbm, state_ref, pre_ref, post_ref, win_ref, wout_ref, convw_ref, vg_ref,
                ws_ref, b_ref, ga_ref, gb_ref,
                o_hbm, convp_ref, vp_ref, convs_ref, vs_ref,
                win_s, wout_s, n_s, yab_s, carry_ref, vn_s, vt_s, ones_ref, wslab_s, bias_s,
                hbuf, obuf, in_sem, out_sem,
                *, n_prep, n_prompt, n_blocks, blocks_per_seq, seq_len):
    tm = n_s.shape[0]
    i = pl.program_id(0)
    n_h, n_o = hbuf.shape[0], obuf.shape[0]

    def block_rows(ref, b):
        return ref.at[pl.ds(b * tm if isinstance(b, int) else pl.multiple_of(b * tm, tm), tm)]

    def h_copy(b):
        return pltpu.make_async_copy(block_rows(h_hbm, b), hbuf.at[b % n_h], in_sem.at[b % n_h])

    def out_copy(b):
        return pltpu.make_async_copy(obuf.at[b % n_o], block_rows(o_hbm, b), out_sem.at[b % n_o])

    @pl.when(i < n_prep)
    def _():
        win_s[i] = win_ref[...].astype(BF16)

    _store_piece(wout_s, wout_ref, i, wout_s.shape[0] // wout_ref.shape[0])

    @pl.when(i == 0)
    def _():
        h_copy(0).start()
        if n_blocks > 1:
            h_copy(1).start()
        row_head = lax.broadcasted_iota(jnp.int32, ones_ref.shape, 0) // HEAD_DIM_B
        col_head = lax.broadcasted_iota(jnp.int32, ones_ref.shape, 1) // HEAD_DIM_B
        ones_ref[...] = jnp.where(row_head == col_head, 1.0 / HEAD_DIM_B, 0.0).astype(BF16)

    assert MIX_FINISH_SLICES <= MIX_SEGMENTS
    slice_rows = tm // MIX_FINISH_SLICES

    def out_proj(b):
        y = jnp.dot(yab_s[...], wout_s[...], preferred_element_type=F32)
        return dict(b=b, y=y, r=lax.rsqrt(jnp.mean(y * y, axis=-1, keepdims=True) + EPS), done=0)

    def finish_slice(prev, anchor=None):
        rows = slice(prev["done"] * slice_rows, (prev["done"] + 1) * slice_rows)
        prev["done"] += 1
        r = prev["r"][rows] if anchor is None else _after(prev["r"][rows], anchor[rows, 0:1])
        b = prev["b"]
        obuf[b % n_o, rows, :] = hbuf[b % n_h, rows, :] + prev["y"][rows] * r * post_ref[...]

    def seg(k, prev):
        out = jnp.dot(n_s[...], win_s[k], preferred_element_type=F32)
        if prev is not None and prev["done"] < MIX_FINISH_SLICES:
            finish_slice(prev, anchor=out)
        return out

    def conv(z, z1, z2):
        return convw_ref[0:1, :] * z2 + convw_ref[1:2, :] * z1 + convw_ref[2:3, :] * z

    def per_head(piece, rows):
        return jnp.concatenate([jnp.broadcast_to(piece(h), (rows, HEAD_DIM_B)) for h in range(HEADS_B)], axis=1)

    def v_gain():
        return jnp.concatenate([vg_ref[h:h + 1, :] for h in range(HEADS_B)], axis=1)

    def chunk_bias(rows):
        b_t = b_ref[...].T
        return per_head(lambda h: b_t[0:rows, h:h + 1], rows)

    @pl.when(i == 0)
    def _():
        t_idx = lax.broadcasted_iota(jnp.int32, (CHUNK, HEADS_PER_SLAB * CHUNK), 0)
        s_idx = lax.broadcasted_iota(jnp.int32, (CHUNK, HEADS_PER_SLAB * CHUNK), 1) % CHUNK
        for k in range(N_SLABS):
            w_k = jnp.concatenate([ws_ref[k * HEADS_PER_SLAB + hd] for hd in range(HEADS_PER_SLAB)], axis=1)
            wslab_s[k] = jnp.where(t_idx >= s_idx, w_k, 0.0).astype(BF16)
        bias_s[...] = chunk_bias(CHUNK)

    def group_norms(conv_out, mixed, prev):
        yab_s[:, 0:CONV_DIM] = _rms(seg(0, prev) * conv_out, ga_ref[...]).astype(BF16)
        yab_s[:, CONV_DIM:] = _rms(seg(3, prev) * mixed, gb_ref[...]).astype(BF16)

    def prompt_block(b, prev):
        n_s[...] = _rms(hbuf[b % n_h], pre_ref[...]).astype(BF16)
        seq = b // blocks_per_seq

        v_n = _head_rms(seg(4, prev), ones_ref[...], v_gain())
        vp_ref[seq] = v_n[tm - CHUNK:, :].T
        lane_head = lax.broadcasted_iota(jnp.int32, (CHUNK, V7X_MXU_COLS), 1) // HEAD_DIM_B
        chunks = []
        for c in range(tm // CHUNK):
            slabs = []
            for k in range(N_SLABS):
                cols = slice(k * V7X_MXU_COLS, (k + 1) * V7X_MXU_COLS)
                v_ck = v_n[c * CHUNK:(c + 1) * CHUNK, cols].astype(BF16)
                bd = jnp.concatenate(
                    [jnp.where(lane_head == hd, v_ck, jnp.zeros((), BF16)) for hd in range(HEADS_PER_SLAB)], axis=0)
                slabs.append(jnp.dot(wslab_s[k], bd, preferred_element_type=F32))
            chunks.append(jnp.concatenate(slabs, axis=1) + bias_s[...])
        mixed = jnp.concatenate(chunks, axis=0)

        z = seg(1, prev) * seg(2, prev)
        row = lax.broadcasted_iota(jnp.int32, z.shape, 0)
        if isinstance(b, int) and b % blocks_per_seq == 0:
            carry = jnp.zeros(carry_ref.shape, F32)
        else:
            carry = jnp.where(b % blocks_per_seq == 0, 0.0, carry_ref[...])
        prev1 = carry[1:2, :]
        prev2 = carry[0:1, :]
        z1 = jnp.where(row == 0, prev1, pltpu.roll(z, 1, axis=0))
        z2 = jnp.where(row == 0, prev2, jnp.where(row == 1, prev1, pltpu.roll(z, 2, axis=0)))
        tail = z[tm - (CONV_K - 1):, :]
        carry_ref[...] = tail
        convp_ref[seq] = tail
        group_norms(conv(z, z1, z2), mixed, prev)

    def sample_block(b, prev):
        n_s[...] = _rms(hbuf[b % n_h], pre_ref[...]).astype(BF16)

        n_seq = tm // seq_len
        per_seq = lambda a: a.reshape(n_seq, seq_len, a.shape[-1])
        every_row = lambda a: jnp.broadcast_to(a, (n_seq, seq_len, a.shape[-1]))
        v_n = _head_rms(seg(4, prev), ones_ref[...], v_gain())
        first_seq = pl.multiple_of((b - n_prompt) * n_seq, n_seq)
        for c in range(CHUNK_DIM // V7X_LANES):
            lanes = slice(c * V7X_LANES, (c + 1) * V7X_LANES)
            vn_s[c] = v_n[:, lanes]
            for t in range(seq_len):
                vt_s[t, pl.ds(first_seq, n_seq), lanes] = vn_s[c, pl.ds(t, n_seq, stride=seq_len), :]
        v_seq = per_seq(v_n)
        t_row = lax.broadcasted_iota(jnp.int32, (seq_len, CHUNK_DIM), 0)
        mixed = every_row(chunk_bias(seq_len)[None])
        for s in range(seq_len):
            w_from_s = per_head(lambda h: ws_ref[h, 0:seq_len, s:s + 1], seq_len)
            mixed = mixed + jnp.where(t_row >= s, w_from_s, 0.0)[None] * every_row(v_seq[:, s:s + 1, :])
        mixed = mixed.reshape(tm, CHUNK_DIM)

        z = seg(1, prev) * seg(2, prev)
        t = lax.broadcasted_iota(jnp.int32, z.shape, 0) % seq_len
        st = state_ref[pl.ds(first_seq, n_seq)]
        prev1 = every_row(st[:, 1:2, :]).reshape(tm, CONV_DIM)
        prev2 = every_row(st[:, 0:1, :]).reshape(tm, CONV_DIM)
        z1 = jnp.where(t == 0, prev1, pltpu.roll(z, 1, axis=0))
        z2 = jnp.where(t == 0, prev2, jnp.where(t == 1, prev1, pltpu.roll(z, 2, axis=0)))
        convs_ref[pl.ds(first_seq, n_seq)] = per_seq(z)[:, seq_len - (CONV_K - 1):, :]
        group_norms(conv(z, z1, z2), mixed, prev)

    def iteration(block):
        def body(b, carry):
            h_copy(b).wait()

            @pl.when(b + 1 < n_blocks)
            def _():
                h_copy(b + 1).start()

            @pl.when(b - 1 >= n_o)
            def _():
                out_copy(b - 1 - n_o).wait()

            block(b, out_proj(b - 1))
            out_copy(b - 1).start()
            return carry
        return body

    @pl.when(i == n_prep)
    def _():
        h_copy(0).wait()
        prompt_block(0, None)
        lax.fori_loop(1, n_prompt, iteration(prompt_block), 0)
        lax.fori_loop(n_prompt, n_blocks, iteration(sample_block), 0)
        for t in range(seq_len):
            vs_ref[t] = vt_s[t].T
        last = n_blocks - 1
        if last >= n_o:
            out_copy(last - n_o).wait()
        prev = out_proj(last)
        for _ in range(MIX_FINISH_SLICES):
            finish_slice(prev)
        out_copy(last).start()
        for b in range(max(last - n_o + 1, 0), last + 1):
            out_copy(b).wait()


def _resident(a):
    return pl.BlockSpec(a.shape, lambda i: (0,) * a.ndim, pipeline_mode=pl.Buffered(1))


def _clamped(lo, n):
    return lambda i: jnp.minimum(jnp.maximum(i - lo, 0), n - 1)


def _stream(tm, d, lo, n):
    blk = _clamped(lo, n)
    return pl.BlockSpec((tm, d), lambda i: (blk(i), 0))


def _row_pieces(w, rows):
    blk = _clamped(0, w.shape[0] // rows)
    return pl.BlockSpec((rows, w.shape[1]), lambda i: (blk(i), 0))


def _col_pieces(w, cols):
    blk = _clamped(0, w.shape[1] // cols)
    return pl.BlockSpec((w.shape[0], cols), lambda i: (0, blk(i)))


def _params():
    return pltpu.CompilerParams(dimension_semantics=("arbitrary",), vmem_limit_bytes=V7X_VMEM_LIMIT_BYTES)


def _mix(h, state, batch_p, seq_p, batch_s, seq_s, mw):
    tm = MIX_TOKEN_BLOCK
    d = h.shape[1]
    n_prep = MIX_SEGMENTS
    assert seq_p % tm == 0 and (batch_s * seq_s) % tm == 0 and tm % seq_s == 0
    n_prompt, n_sample = batch_p * seq_p // tm, batch_s * seq_s // tm
    n_blocks = n_prompt + n_sample
    blocks_per_seq = seq_p // tm
    assert d // WEIGHT_TILE <= n_prep and n_prompt >= 1
    small = [mw["v_g"], mw["w_s"], mw["b_s"], mw["g_a"], mw["g_b"]]
    conv_w_spec = pl.BlockSpec((None,) + mw["conv_w"].shape[1:], lambda i: (0, 0, 0), pipeline_mode=pl.Buffered(1))
    anywhere = pl.BlockSpec(memory_space=pl.ANY)
    whole = lambda shape: pl.BlockSpec(shape, lambda i: (0,) * len(shape))
    state_shapes = [(batch_p, CONV_K - 1, CONV_DIM), (batch_p, CHUNK_DIM, CHUNK),
                    (batch_s, CONV_K - 1, CONV_DIM), (seq_s, CHUNK_DIM, batch_s)]
    return pl.pallas_call(
        functools.partial(_mix_kernel, n_prep=n_prep, n_prompt=n_prompt, n_blocks=n_blocks,
                          blocks_per_seq=blocks_per_seq, seq_len=seq_s),
        grid=(n_prep + 1,),
        in_specs=[anywhere, _resident(state), _resident(mw["pre"]), _resident(mw["post"]),
                  _col_pieces(mw["w_in"], CONV_DIM), _row_pieces(mw["w_out"], WEIGHT_TILE), conv_w_spec]
                 + [_resident(a) for a in small],
        out_specs=[anywhere] + [whole(shape) for shape in state_shapes],
        out_shape=[jax.ShapeDtypeStruct(h.shape, F32)] + [jax.ShapeDtypeStruct(shape, F32) for shape in state_shapes],
        scratch_shapes=[pltpu.VMEM((MIX_SEGMENTS, d, CONV_DIM), BF16), pltpu.VMEM(mw["w_out"].shape, BF16),
                        pltpu.VMEM((tm, d), BF16), pltpu.VMEM((tm, CONV_DIM + CHUNK_DIM), BF16),
                        pltpu.VMEM((CONV_K - 1, CONV_DIM), F32),
                        pltpu.VMEM((CHUNK_DIM // V7X_LANES, tm, V7X_LANES), F32),
                        pltpu.VMEM((seq_s, batch_s, CHUNK_DIM), F32),
                        pltpu.VMEM((V7X_MXU_COLS, V7X_MXU_COLS), BF16),
                        pltpu.VMEM((N_SLABS, CHUNK, HEADS_PER_SLAB * CHUNK), BF16),
                        pltpu.VMEM((CHUNK, CHUNK_DIM), F32),
                        pltpu.VMEM((3, tm, d), F32), pltpu.VMEM((2, tm, d), F32),
                        pltpu.SemaphoreType.DMA((3,)), pltpu.SemaphoreType.DMA((2,))],
        compiler_params=_params(),
        name="mix",
    )(h, state, mw["pre"], mw["post"], mw["w_in"], mw["w_out"], mw["conv_w"], *small)


def _row(g):
    return g.reshape(1, -1).astype(F32)


def _mix_weights(mix_pre_g, mix_post_g, w_in, conv_w, v_norm_g, w_s, b_s, out_g_a, out_g_b, w_out):
    return dict(pre=_row(mix_pre_g), post=_row(mix_post_g), w_in=w_in, w_out=w_out,
                conv_w=conv_w.astype(F32), v_g=v_norm_g.astype(F32), w_s=w_s.astype(F32),
                b_s=b_s.astype(F32), g_a=_row(out_g_a), g_b=_row(out_g_b))


def kernel(x_prompt, x_sample, p_prompt, p_sample, state_conv, ffn1_pre_g, ffn1_post_g, ffn1_w_gate, ffn1_w_up, ffn1_w_down, mix_pre_g, mix_post_g, w_in, conv_w, v_norm_g, w_s, b_s, out_g_a, out_g_b, w_out, ffn2_pre_g, ffn2_post_g, ffn2_w_gate, ffn2_w_up, ffn2_w_down, ple_w_gate, ple_w_proj, ple_post_g):
    depth = ffn1_pre_g.shape[0]
    bp, lp, d = x_prompt.shape
    bs, ls, _ = x_sample.shape
    tm = TOKEN_BLOCK
    assert lp % tm == 0 and (bs * ls) % tm == 0 and tm % ls == 0
    assert CONV_K - 1 <= ls <= V7X_SUBLANES and (lp - 1) % CHUNK + 1 == CHUNK
    n_prompt, n_sample = bp * lp // tm, bs * ls // tm
    x = (x_prompt.reshape(bp * lp, d), x_sample.reshape(bs * ls, d))
    conv_p, conv_s, v_p, v_s = [], [], [], []
    for i in range(depth):
        mw = _mix_weights(mix_pre_g[i], mix_post_g[i], w_in[i], conv_w[i:i + 1], v_norm_g[i], w_s[i], b_s[i],
                          out_g_a[i], out_g_b[i], w_out[i])
        ple = (p_prompt[i].reshape(bp * lp, -1), p_sample[i].reshape(bs * ls, -1),
               ple_w_gate[i], ple_w_proj[i], _row(ple_post_g[i]))

        blocks = dict(n_prompt=n_prompt, n_sample=n_sample)
        h = _ffn_loop(x, _row(ffn1_pre_g[i]), _row(ffn1_post_g[i]),
                      ffn1_w_gate[i], ffn1_w_up[i], ffn1_w_down[i], name="ffn1", **blocks)
        h, cp, vp, cs, vs = _mix(h, state_conv[i].astype(F32), bp, lp, bs, ls, mw)
        x = _ffn_loop(h, _row(ffn2_pre_g[i]), _row(ffn2_post_g[i]),
                      ffn2_w_gate[i], ffn2_w_up[i], ffn2_w_down[i], ple, name="ffn2_ple", **blocks)

        conv_p.append(cp)
        conv_s.append(cs)
        v_p.append(vp.transpose(0, 2, 1).reshape(bp, CHUNK, HEADS_B, HEAD_DIM_B))
        v_s.append(vs.transpose(2, 0, 1).reshape(bs, ls, HEADS_B, HEAD_DIM_B))
    layers = lambda parts: parts[0][None] if depth == 1 else jnp.stack(parts)
    return (x[0].reshape(bp, lp, d), x[1].reshape(bs, ls, d),
            layers(conv_p), layers(conv_s), layers(v_p), layers(v_s))
```

```python
import functools

import jax
import jax.numpy as jnp
from jax import lax
from jax.experimental import pallas as pl
from jax.experimental.pallas import tpu as pltpu

CONV_K = 3
HEADS_B = 8
HEAD_DIM_B = 64
CHUNK_DIM = HEADS_B * HEAD_DIM_B
CONV_DIM = CHUNK_DIM
CHUNK = 128
EPS = 1e-6

V7X_SUBLANES = 8
V7X_LANES = 128
V7X_MXU_COLS = 256
V7X_VMEM_LIMIT_BYTES = 60 * 1024 * 1024

TOKEN_BLOCK = 512
MIX_TOKEN_BLOCK = 512
WEIGHT_TILE = V7X_MXU_COLS
HEADS_PER_SLAB = V7X_MXU_COLS // HEAD_DIM_B
N_SLABS = CHUNK_DIM // V7X_MXU_COLS
MIX_SEGMENTS = 5
FINISH_SLICES = 8
FINISH_FIRST_CHUNK = 1
MIX_FINISH_SLICES = 4

BF16 = jnp.bfloat16
F32 = jnp.float32


def _rms(x, g):
    return x * lax.rsqrt(jnp.mean(x * x, axis=-1, keepdims=True) + EPS) * g


def _dot(a, b):
    return jnp.dot(a.astype(BF16), b, preferred_element_type=F32)


def _sigmoid(x):
    return 0.5 + 0.5 * jnp.tanh(0.5 * x)


def _silu(x):
    half = 0.5 * x
    return half + half * jnp.tanh(half)


def _after(value, anchor):
    bits = lax.bitcast_convert_type(anchor, jnp.int32)
    zero = lax.shift_right_logical(lax.shift_right_logical(bits, 16), 16)
    return value + zero.astype(value.dtype)


def _store_piece(dst_ref, src_ref, step, n_pieces):
    rows = src_ref.shape[0]

    @pl.when(step < n_pieces)
    def _():
        dst_ref[pl.ds(pl.multiple_of(step * rows, rows), rows), :] = src_ref[...].astype(BF16)


def _ffn_loop_kernel(*refs, n_prep, n_prompt, n_blocks, with_ple):
    if with_ple:
        (x_hbm, pp_hbm, ps_hbm, pre_ref, post_ref, wg_ref, wu_ref, wd_ref, wpg_ref, wpp_ref, pleg_ref,
         op_hbm, os_hbm, wgu_s, wd_s, xn_s, a_s, xbuf, obuf, in_sem, out_sem, wst_g, wst_u, wst_d, w_sem,
         wpg_s, wpp_s, hb_s, pbuf, p_sem, wpg_st, wpp_st, e_sem) = refs
    else:
        (xp_hbm, xs_hbm, pre_ref, post_ref, wg_ref, wu_ref, wd_ref, o_hbm,
         wgu_s, wd_s, xn_s, a_s, xbuf, obuf, in_sem, out_sem, wst_g, wst_u, wst_d, w_sem) = refs
    tm = xn_s.shape[0]
    n_x, n_o = xbuf.shape[0], obuf.shape[0]
    n_slices = FINISH_SLICES // 2 if with_ple else FINISH_SLICES
    slice_rows = tm // n_slices
    stages = [("residual", s) for s in range(n_slices)]
    if with_ple:
        stages += [("embed_dots", 0)] + [("embed_add", s) for s in range(n_slices)]
    assert FINISH_FIRST_CHUNK + len(stages) <= n_prep

    def rows_of(b, prompt_hbm, sample_hbm):
        if isinstance(b, int):
            return (prompt_hbm.at[pl.ds(b * tm, tm)] if b < n_prompt else None,
                    sample_hbm.at[pl.ds((b - n_prompt) * tm, tm)] if b >= n_prompt else None)
        return (prompt_hbm.at[pl.ds(pl.multiple_of(b * tm, tm), tm)],
                sample_hbm.at[pl.ds(pl.multiple_of((b - n_prompt) * tm, tm), tm)])

    def start_either(b, make_copy, prompt_hbm, sample_hbm):
        in_prompt, in_sample = rows_of(b, prompt_hbm, sample_hbm)
        if isinstance(b, int):
            make_copy(in_prompt if in_sample is None else in_sample).start()
            return

        @pl.when(b < n_prompt)
        def _():
            make_copy(in_prompt).start()

        @pl.when(b >= n_prompt)
        def _():
            make_copy(in_sample).start()

    def x_copy(b):
        return lambda src: pltpu.make_async_copy(src, xbuf.at[b % n_x], in_sem.at[b % n_x])

    def start_x(b):
        if with_ple:
            row0 = b * tm if isinstance(b, int) else pl.multiple_of(b * tm, tm)
            x_copy(b)(x_hbm.at[pl.ds(row0, tm)]).start()
        else:
            start_either(b, x_copy(b), xp_hbm, xs_hbm)

    def wait_x(b):
        x_copy(b)((x_hbm if with_ple else xp_hbm).at[pl.ds(0, tm)]).wait()

    def p_copy(b):
        return lambda src: pltpu.make_async_copy(src, pbuf.at[b % 2], p_sem.at[b % 2])

    def out_copy(b):
        return lambda dst: pltpu.make_async_copy(obuf.at[b % n_o], dst, out_sem.at[b % n_o])

    def start_out(b):
        if with_ple:
            start_either(b, out_copy(b), op_hbm, os_hbm)
        else:
            row0 = b * tm if isinstance(b, int) else pl.multiple_of(b * tm, tm)
            out_copy(b)(o_hbm.at[pl.ds(row0, tm)]).start()

    def wait_out(b):
        out_copy(b)((op_hbm if with_ple else o_hbm).at[pl.ds(0, tm)]).wait()

    def gate_up_chunk(c):
        gu = jnp.dot(xn_s[...], wgu_s[c], preferred_element_type=F32)
        g, u = gu[:, 0:WEIGHT_TILE], gu[:, WEIGHT_TILE:]
        return g, (_silu(g) * u).astype(BF16)

    def down():
        y = jnp.dot(a_s[...], wd_s[...], preferred_element_type=F32)
        return dict(y=y, r=lax.rsqrt(jnp.mean(y * y, axis=-1, keepdims=True) + EPS))

    def ordered(r, rows, anchor):
        return r[rows] if anchor is None else _after(r[rows], anchor[rows, 0:1])

    def finish_stage(b, stage, s, st, anchor=None):
        rows = slice(s * slice_rows, (s + 1) * slice_rows)
        o = b % n_o
        if stage == "residual":
            h = xbuf[b % n_x, rows, :] + 0.5 * (st["y"][rows] * ordered(st["r"], rows, anchor) * post_ref[...])
            obuf[o, rows, :] = h
            if with_ple:
                hb_s[rows, :] = h.astype(BF16)
        elif stage == "embed_dots":
            gate = _sigmoid(jnp.dot(hb_s[...], wpg_s[...], preferred_element_type=F32))
            st["e"] = gate * _dot(pbuf[b % 2], wpp_s[...])
            st["r2"] = lax.rsqrt(jnp.mean(st["e"] * st["e"], axis=-1, keepdims=True) + EPS)
        else:
            obuf[o, rows, :] = obuf[o, rows, :] + st["e"][rows] * ordered(st["r2"], rows, anchor) * pleg_ref[...]

    def weight_copies(c):
        slot, cols = c % 2, slice(c * WEIGHT_TILE, (c + 1) * WEIGHT_TILE)
        return [pltpu.make_async_copy(wg_ref.at[:, cols], wst_g.at[slot], w_sem.at[0, slot]),
                pltpu.make_async_copy(wu_ref.at[:, cols], wst_u.at[slot], w_sem.at[1, slot]),
                pltpu.make_async_copy(wd_ref.at[cols, :], wst_d.at[slot], w_sem.at[2, slot])]

    def embed_weight_copies():
        return [pltpu.make_async_copy(wpg_ref, wpg_st, e_sem.at[0]), pltpu.make_async_copy(wpp_ref, wpp_st, e_sem.at[1])]

    start_x(0)
    for c in range(min(2, n_prep)):
        for copy in weight_copies(c):
            copy.start()
    if with_ple:
        start_either(0, p_copy(0), pp_hbm, ps_hbm)
        for copy in embed_weight_copies():
            copy.start()
    wait_x(0)
    if n_blocks > 1:
        start_x(1)
    xn_s[...] = _rms(xbuf[0], pre_ref[...]).astype(BF16)
    for c in range(n_prep):
        cols = slice(c * WEIGHT_TILE, (c + 1) * WEIGHT_TILE)
        for copy in weight_copies(c):
            copy.wait()
        wgu_s[c, :, 0:WEIGHT_TILE] = wst_g[c % 2].astype(BF16)
        wgu_s[c, :, WEIGHT_TILE:] = wst_u[c % 2].astype(BF16)
        wd_s[cols, :] = wst_d[c % 2].astype(BF16)
        if c + 2 < n_prep:
            for copy in weight_copies(c + 2):
                copy.start()
        _, a_s[:, cols] = gate_up_chunk(c)
    if with_ple:
        for copy in embed_weight_copies():
            copy.wait()
        wpg_s[...] = wpg_st[...].astype(BF16)
        wpp_s[...] = wpp_st[...].astype(BF16)

    def run_blocks():
        def step(b, carry):
            wait_x(b + 1)
            if with_ple:
                p_copy(b)(pp_hbm.at[pl.ds(0, tm)]).wait()
                start_either(b + 1, p_copy(b + 1), pp_hbm, ps_hbm)

            @pl.when(b + 2 < n_blocks)
            def _():
                start_x(b + 2)

            @pl.when(b >= n_o)
            def _():
                wait_out(b - n_o)

            st = down()
            x = _after(xbuf[(b + 1) % n_x], st["y"][:, 0:1])
            xn_s[...] = _rms(x, pre_ref[...]).astype(BF16)
            for c in range(n_prep):
                g, a_s[:, c * WEIGHT_TILE:(c + 1) * WEIGHT_TILE] = gate_up_chunk(c)
                k = c - FINISH_FIRST_CHUNK
                if 0 <= k < len(stages):
                    finish_stage(b, *stages[k], st, anchor=g)
            start_out(b)
            return carry

        lax.fori_loop(0, n_blocks - 1, step, 0)
        last = n_blocks - 1
        if with_ple:
            p_copy(last)(pp_hbm.at[pl.ds(0, tm)]).wait()
        if last >= n_o:
            wait_out(last - n_o)
        st = down()
        for stage, s in stages:
            finish_stage(last, stage, s, st)
        start_out(last)
        for b in range(max(last - n_o + 1, 0), last + 1):
            wait_out(b)

    run_blocks()


def _ffn_loop(x, pre_g, post_g, wg, wu, wd, ple=None, *, n_prompt, n_sample, name):
    tm = TOKEN_BLOCK
    d, d_ff = wg.shape
    n_prep = d_ff // WEIGHT_TILE
    n_blocks = n_prompt + n_sample
    assert n_prompt >= 1
    anywhere = pl.BlockSpec(memory_space=pl.ANY)
    weights = [_resident(pre_g), _resident(post_g), anywhere, anywhere, anywhere]
    scratch = [pltpu.VMEM((n_prep, d, 2 * WEIGHT_TILE), BF16), pltpu.VMEM((d_ff, d), BF16),
               pltpu.VMEM((tm, d), BF16), pltpu.VMEM((tm, d_ff), BF16),
               pltpu.VMEM((3, tm, d), F32), pltpu.VMEM((2, tm, d), F32),
               pltpu.SemaphoreType.DMA((3,)), pltpu.SemaphoreType.DMA((2,)),
               pltpu.VMEM((2, d, WEIGHT_TILE), F32), pltpu.VMEM((2, d, WEIGHT_TILE), F32),
               pltpu.VMEM((2, WEIGHT_TILE, d), F32), pltpu.SemaphoreType.DMA((3, 2))]
    if ple is None:
        in_specs = [anywhere, anywhere] + weights
        args = [x[0], x[1], pre_g, post_g, wg, wu, wd]
        out_specs = anywhere
        out_shape = jax.ShapeDtypeStruct((n_blocks * tm, d), F32)
    else:
        pp, ps, wpg, wpp, ple_g = ple
        in_specs = [anywhere, anywhere, anywhere] + weights + [anywhere, anywhere, _resident(ple_g)]
        args = [x, pp, ps, pre_g, post_g, wg, wu, wd, wpg, wpp, ple_g]
        out_specs = [anywhere, anywhere]
        out_shape = [jax.ShapeDtypeStruct((n_prompt * tm, d), F32), jax.ShapeDtypeStruct((n_sample * tm, d), F32)]
        scratch += [pltpu.VMEM(wpg.shape, BF16), pltpu.VMEM(wpp.shape, BF16), pltpu.VMEM((tm, d), BF16),
                    pltpu.VMEM((2, tm, pp.shape[1]), F32), pltpu.SemaphoreType.DMA((2,)),
                    pltpu.VMEM(wpg.shape, F32), pltpu.VMEM(wpp.shape, F32), pltpu.SemaphoreType.DMA((2,))]
    return pl.pallas_call(
        functools.partial(_ffn_loop_kernel, n_prep=n_prep, n_prompt=n_prompt, n_blocks=n_blocks,
                          with_ple=ple is not None),
        grid=(1,),
        in_specs=in_specs,
        out_specs=out_specs,
        out_shape=out_shape,
        scratch_shapes=scratch,
        compiler_params=_params(),
        name=name,
    )(*args)


def _head_rms(v, ones_slab, g):
    vv = (v * v).astype(BF16)
    ms = jnp.concatenate(
        [jnp.dot(vv[:, k * V7X_MXU_COLS:(k + 1) * V7X_MXU_COLS], ones_slab, preferred_element_type=F32)
         for k in range(CHUNK_DIM // V7X_MXU_COLS)], axis=1)
    return v * lax.rsqrt(ms + EPS) * g


def _mix_kernel(hc_ref, hv_ref, state_ref, pre_ref, post_ref, win_ref, wout_ref, convw_ref, vg_ref,
                ws_ref, b_ref, ga_ref, gb_ref,
                o_ref, convp_ref, vp_ref, convs_ref, vs_ref,
                win_s, wout_s, n_s, yab_s, carry_ref, vn_s, vt_s, ones_ref, wslab_s, bias_s,
                *, n_prep, n_prompt, n_blocks, blocks_per_seq, seq_len):
    tm = hc_ref.shape[0]
    i = pl.program_id(0)
    j = i - n_prep

    @pl.when(i < n_prep)
    def _():
        win_s[i] = win_ref[...].astype(BF16)

    _store_piece(wout_s, wout_ref, i, wout_s.shape[0] // wout_ref.shape[0])

    @pl.when(i == 0)
    def _():
        yab_s[...] = jnp.zeros_like(yab_s)
        carry_ref[...] = jnp.zeros_like(carry_ref)
        row_head = lax.broadcasted_iota(jnp.int32, ones_ref.shape, 0) // HEAD_DIM_B
        col_head = lax.broadcasted_iota(jnp.int32, ones_ref.shape, 1) // HEAD_DIM_B
        ones_ref[...] = jnp.where(row_head == col_head, 1.0 / HEAD_DIM_B, 0.0).astype(BF16)

    assert MIX_FINISH_SLICES <= MIX_SEGMENTS
    slice_rows = tm // MIX_FINISH_SLICES

    def out_proj_prev():
        y = jnp.dot(yab_s[...], wout_s[...], preferred_element_type=F32)
        return dict(y=y, r=lax.rsqrt(jnp.mean(y * y, axis=-1, keepdims=True) + EPS), done=0)

    def finish_slice(prev, anchor=None):
        rows = slice(prev["done"] * slice_rows, (prev["done"] + 1) * slice_rows)
        prev["done"] += 1
        r = prev["r"][rows] if anchor is None else _after(prev["r"][rows], anchor[rows, 0:1])
        o_ref[rows, :] = hv_ref[rows, :] + prev["y"][rows] * r * post_ref[...]

    def seg(k, prev):
        out = jnp.dot(n_s[...], win_s[k], preferred_element_type=F32)
        if prev["done"] < MIX_FINISH_SLICES:
            finish_slice(prev, anchor=out)
        return out

    def conv(z, z1, z2):
        return convw_ref[0:1, :] * z2 + convw_ref[1:2, :] * z1 + convw_ref[2:3, :] * z

    def per_head(piece, rows):
        return jnp.concatenate([jnp.broadcast_to(piece(h), (rows, HEAD_DIM_B)) for h in range(HEADS_B)], axis=1)

    def v_gain():
        return jnp.concatenate([vg_ref[h:h + 1, :] for h in range(HEADS_B)], axis=1)

    def chunk_bias(rows):
        b_t = b_ref[...].T
        return per_head(lambda h: b_t[0:rows, h:h + 1], rows)

    @pl.when(i == 0)
    def _():
        t_idx = lax.broadcasted_iota(jnp.int32, (CHUNK, HEADS_PER_SLAB * CHUNK), 0)
        s_idx = lax.broadcasted_iota(jnp.int32, (CHUNK, HEADS_PER_SLAB * CHUNK), 1) % CHUNK
        for k in range(N_SLABS):
            w_k = jnp.concatenate([ws_ref[k * HEADS_PER_SLAB + hd] for hd in range(HEADS_PER_SLAB)], axis=1)
            wslab_s[k] = jnp.where(t_idx >= s_idx, w_k, 0.0).astype(BF16)
        bias_s[...] = chunk_bias(CHUNK)

    def group_norms(conv_out, mixed, prev):
        yab_s[:, 0:CONV_DIM] = _rms(seg(0, prev) * conv_out, ga_ref[...]).astype(BF16)
        yab_s[:, CONV_DIM:] = _rms(seg(3, prev) * mixed, gb_ref[...]).astype(BF16)

    def prompt_block():
        prev = out_proj_prev()
        n_s[...] = _rms(hc_ref[...], pre_ref[...]).astype(BF16)

        v_n = _head_rms(seg(4, prev), ones_ref[...], v_gain())
        vp_ref[0] = v_n[tm - CHUNK:, :].T
        lane_head = lax.broadcasted_iota(jnp.int32, (CHUNK, V7X_MXU_COLS), 1) // HEAD_DIM_B
        chunks = []
        for c in range(tm // CHUNK):
            slabs = []
            for k in range(N_SLABS):
                cols = slice(k * V7X_MXU_COLS, (k + 1) * V7X_MXU_COLS)
                v_ck = v_n[c * CHUNK:(c + 1) * CHUNK, cols].astype(BF16)
                bd = jnp.concatenate(
                    [jnp.where(lane_head == hd, v_ck, jnp.zeros((), BF16)) for hd in range(HEADS_PER_SLAB)], axis=0)
                slabs.append(jnp.dot(wslab_s[k], bd, preferred_element_type=F32))
            chunks.append(jnp.concatenate(slabs, axis=1) + bias_s[...])
        mixed = jnp.concatenate(chunks, axis=0)

        z = seg(1, prev) * seg(2, prev)
        row = lax.broadcasted_iota(jnp.int32, z.shape, 0)
        carry = jnp.where(j % blocks_per_seq == 0, 0.0, carry_ref[...])
        prev1 = carry[1:2, :]
        prev2 = carry[0:1, :]
        z1 = jnp.where(row == 0, prev1, pltpu.roll(z, 1, axis=0))
        z2 = jnp.where(row == 0, prev2, jnp.where(row == 1, prev1, pltpu.roll(z, 2, axis=0)))
        tail = z[tm - (CONV_K - 1):, :]
        carry_ref[...] = tail
        convp_ref[0] = tail
        group_norms(conv(z, z1, z2), mixed, prev)

    def sample_block():
        prev = out_proj_prev()
        n_s[...] = _rms(hc_ref[...], pre_ref[...]).astype(BF16)

        n_seq = tm // seq_len
        per_seq = lambda a: a.reshape(n_seq, seq_len, a.shape[-1])
        every_row = lambda a: jnp.broadcast_to(a, (n_seq, seq_len, a.shape[-1]))
        v_n = _head_rms(seg(4, prev), ones_ref[...], v_gain())
        first_seq = pl.multiple_of((j - n_prompt) * n_seq, n_seq)
        for c in range(CHUNK_DIM // V7X_LANES):
            lanes = slice(c * V7X_LANES, (c + 1) * V7X_LANES)
            vn_s[c] = v_n[:, lanes]
            for t in range(seq_len):
                vt_s[t, pl.ds(first_seq, n_seq), lanes] = vn_s[c, pl.ds(t, n_seq, stride=seq_len), :]

        @pl.when(j == n_blocks - 1)
        def _():
            for t in range(seq_len):
                vs_ref[t] = vt_s[t].T
        v_seq = per_seq(v_n)
        t_row = lax.broadcasted_iota(jnp.int32, (seq_len, CHUNK_DIM), 0)
        mixed = every_row(chunk_bias(seq_len)[None])
        for s in range(seq_len):
            w_from_s = per_head(lambda h: ws_ref[h, 0:seq_len, s:s + 1], seq_len)
            mixed = mixed + jnp.where(t_row >= s, w_from_s, 0.0)[None] * every_row(v_seq[:, s:s + 1, :])
        mixed = mixed.reshape(tm, CHUNK_DIM)

        z = seg(1, prev) * seg(2, prev)
        t = lax.broadcasted_iota(jnp.int32, z.shape, 0) % seq_len
        st = state_ref[...]
        prev1 = every_row(st[:, 1:2, :]).reshape(tm, CONV_DIM)
        prev2 = every_row(st[:, 0:1, :]).reshape(tm, CONV_DIM)
        z1 = jnp.where(t == 0, prev1, pltpu.roll(z, 1, axis=0))
        z2 = jnp.where(t == 0, prev2, jnp.where(t == 1, prev1, pltpu.roll(z, 2, axis=0)))
        convs_ref[...] = per_seq(z)[:, seq_len - (CONV_K - 1):, :]
        group_norms(conv(z, z1, z2), mixed, prev)

    @pl.when(jnp.logical_and(j >= 0, j < n_prompt))
    def _():
        prompt_block()

    @pl.when(jnp.logical_and(j >= n_prompt, j < n_blocks))
    def _():
        sample_block()

    @pl.when(j == n_blocks)
    def _():
        prev = out_proj_prev()
        for _ in range(MIX_FINISH_SLICES):
            finish_slice(prev)


def _resident(a):
    return pl.BlockSpec(a.shape, lambda i: (0,) * a.ndim, pipeline_mode=pl.Buffered(1))


def _clamped(lo, n):
    return lambda i: jnp.minimum(jnp.maximum(i - lo, 0), n - 1)


def _stream(tm, d, lo, n):
    blk = _clamped(lo, n)
    return pl.BlockSpec((tm, d), lambda i: (blk(i), 0))


def _row_pieces(w, rows):
    blk = _clamped(0, w.shape[0] // rows)
    return pl.BlockSpec((rows, w.shape[1]), lambda i: (blk(i), 0))


def _col_pieces(w, cols):
    blk = _clamped(0, w.shape[1] // cols)
    return pl.BlockSpec((w.shape[0], cols), lambda i: (0, blk(i)))


def _params():
    return pltpu.CompilerParams(dimension_semantics=("arbitrary",), vmem_limit_bytes=V7X_VMEM_LIMIT_BYTES)


def _mix(h, state, batch_p, seq_p, batch_s, seq_s, mw):
    tm = MIX_TOKEN_BLOCK
    d = h.shape[1]
    n_prep = MIX_SEGMENTS
    assert seq_p % tm == 0 and (batch_s * seq_s) % tm == 0 and tm % seq_s == 0
    n_prompt, n_sample = batch_p * seq_p // tm, batch_s * seq_s // tm
    n_blocks = n_prompt + n_sample
    blocks_per_seq = seq_p // tm
    seqs_per_block = tm // seq_s
    assert d // WEIGHT_TILE <= n_prep
    small = [mw["v_g"], mw["w_s"], mw["b_s"], mw["g_a"], mw["g_b"]]
    conv_w_spec = pl.BlockSpec((None,) + mw["conv_w"].shape[1:], lambda i: (0, 0, 0), pipeline_mode=pl.Buffered(1))
    lo_s = n_prep + n_prompt
    prompt_seq = _clamped(0, batch_p)
    per_prompt_seq = lambda i: (prompt_seq((i - n_prep) // blocks_per_seq), 0, 0)
    sample_blk = _clamped(lo_s, n_sample)
    per_sample_blk = pl.BlockSpec((seqs_per_block, CONV_K - 1, CONV_DIM), lambda i: (sample_blk(i), 0, 0))
    return pl.pallas_call(
        functools.partial(_mix_kernel, n_prep=n_prep, n_prompt=n_prompt, n_blocks=n_blocks,
                          blocks_per_seq=blocks_per_seq, seq_len=seq_s),
        grid=(n_prep + n_blocks + 1,),
        in_specs=[_stream(tm, d, n_prep, n_blocks), _stream(tm, d, n_prep + 1, n_blocks),
                  per_sample_blk, _resident(mw["pre"]), _resident(mw["post"]),
                  _col_pieces(mw["w_in"], CONV_DIM), _row_pieces(mw["w_out"], WEIGHT_TILE), conv_w_spec]
                 + [_resident(a) for a in small],
        out_specs=[_stream(tm, d, n_prep + 1, n_blocks),
                   pl.BlockSpec((1, CONV_K - 1, CONV_DIM), per_prompt_seq),
                   pl.BlockSpec((1, CHUNK_DIM, CHUNK), per_prompt_seq),
                   per_sample_blk,
                   pl.BlockSpec((seq_s, CHUNK_DIM, batch_s), lambda i: (0, 0, 0))],
        out_shape=[jax.ShapeDtypeStruct(h.shape, F32),
                   jax.ShapeDtypeStruct((batch_p, CONV_K - 1, CONV_DIM), F32),
                   jax.ShapeDtypeStruct((batch_p, CHUNK_DIM, CHUNK), F32),
                   jax.ShapeDtypeStruct((batch_s, CONV_K - 1, CONV_DIM), F32),
                   jax.ShapeDtypeStruct((seq_s, CHUNK_DIM, batch_s), F32)],
        scratch_shapes=[pltpu.VMEM((MIX_SEGMENTS, d, CONV_DIM), BF16), pltpu.VMEM(mw["w_out"].shape, BF16),
                        pltpu.VMEM((tm, d), BF16), pltpu.VMEM((tm, CONV_DIM + CHUNK_DIM), BF16),
                        pltpu.VMEM((CONV_K - 1, CONV_DIM), F32),
                        pltpu.VMEM((CHUNK_DIM // V7X_LANES, tm, V7X_LANES), F32),
                        pltpu.VMEM((seq_s, batch_s, CHUNK_DIM), F32),
                        pltpu.VMEM((V7X_MXU_COLS, V7X_MXU_COLS), BF16),
                        pltpu.VMEM((N_SLABS, CHUNK, HEADS_PER_SLAB * CHUNK), BF16),
                        pltpu.VMEM((CHUNK, CHUNK_DIM), F32)],
        compiler_params=_params(),
        name="mix",
    )(h, h, state, mw["pre"], mw["post"], mw["w_in"], mw["w_out"], mw["conv_w"], *small)


def _row(g):
    return g.reshape(1, -1).astype(F32)


def _mix_weights(mix_pre_g, mix_post_g, w_in, conv_w, v_norm_g, w_s, b_s, out_g_a, out_g_b, w_out):
    return dict(pre=_row(mix_pre_g), post=_row(mix_post_g), w_in=w_in, w_out=w_out,
                conv_w=conv_w.astype(F32), v_g=v_norm_g.astype(F32), w_s=w_s.astype(F32),
                b_s=b_s.astype(F32), g_a=_row(out_g_a), g_b=_row(out_g_b))


def kernel(x_prompt, x_sample, p_prompt, p_sample, state_conv, ffn1_pre_g, ffn1_post_g, ffn1_w_gate, ffn1_w_up, ffn1_w_down, mix_pre_g, mix_post_g, w_in, conv_w, v_norm_g, w_s, b_s, out_g_a, out_g_b, w_out, ffn2_pre_g, ffn2_post_g, ffn2_w_gate, ffn2_w_up, ffn2_w_down, ple_w_gate, ple_w_proj, ple_post_g):
    depth = ffn1_pre_g.shape[0]
    bp, lp, d = x_prompt.shape
    bs, ls, _ = x_sample.shape
    tm = TOKEN_BLOCK
    assert lp % tm == 0 and (bs * ls) % tm == 0 and tm % ls == 0
    assert CONV_K - 1 <= ls <= V7X_SUBLANES and (lp - 1) % CHUNK + 1 == CHUNK
    n_prompt, n_sample = bp * lp // tm, bs * ls // tm
    x = (x_prompt.reshape(bp * lp, d), x_sample.reshape(bs * ls, d))
    conv_p, conv_s, v_p, v_s = [], [], [], []
    for i in range(depth):
        mw = _mix_weights(mix_pre_g[i], mix_post_g[i], w_in[i], conv_w[i:i + 1], v_norm_g[i], w_s[i], b_s[i],
                          out_g_a[i], out_g_b[i], w_out[i])
        ple = (p_prompt[i].reshape(bp * lp, -1), p_sample[i].reshape(bs * ls, -1),
               ple_w_gate[i], ple_w_proj[i], _row(ple_post_g[i]))

        blocks = dict(n_prompt=n_prompt, n_sample=n_sample)
        h = _ffn_loop(x, _row(ffn1_pre_g[i]), _row(ffn1_post_g[i]),
                      ffn1_w_gate[i], ffn1_w_up[i], ffn1_w_down[i], name="ffn1", **blocks)
        h, cp, vp, cs, vs = _mix(h, state_conv[i].astype(F32), bp, lp, bs, ls, mw)
        x = _ffn_loop(h, _row(ffn2_pre_g[i]), _row(ffn2_post_g[i]),
                      ffn2_w_gate[i], ffn2_w_up[i], ffn2_w_down[i], ple, name="ffn2_ple", **blocks)

        conv_p.append(cp)
        conv_s.append(cs)
        v_p.append(vp.transpose(0, 2, 1).reshape(bp, CHUNK, HEADS_B, HEAD_DIM_B))
        v_s.append(vs.transpose(2, 0, 1).reshape(bs, ls, HEADS_B, HEAD_DIM_B))
    layers = lambda parts: parts[0][None] if depth == 1 else jnp.stack(parts)
    return (x[0].reshape(bp, lp, d), x[1].reshape(bs, ls, d),
            layers(conv_p), layers(conv_s), layers(v_p), layers(v_s))
```

```python
import functools

import jax
import jax.numpy as jnp
from jax import lax
from jax.experimental import pallas as pl
from jax.experimental.pallas import tpu as pltpu

CONV_K = 3
HEADS_B = 8
HEAD_DIM_B = 64
CHUNK_DIM = HEADS_B * HEAD_DIM_B
CONV_DIM = CHUNK_DIM
CHUNK = 128
EPS = 1e-6

V7X_SUBLANES = 8
V7X_LANES = 128
V7X_MXU_COLS = 256
V7X_VMEM_LIMIT_BYTES = 60 * 1024 * 1024

TOKEN_BLOCK = 512
INPUT_RING = 3
DOUBLE_BUFFER = 2
MIX_TOKEN_BLOCK = 512
WEIGHT_TILE = V7X_MXU_COLS
HEADS_PER_SLAB = V7X_MXU_COLS // HEAD_DIM_B
N_SLABS = CHUNK_DIM // V7X_MXU_COLS
MIX_SEGMENTS = 5
FINISH_SLICES = 8
FINISH_FIRST_CHUNK = 1
MIX_FINISH_SLICES = 4

BF16 = jnp.bfloat16
F32 = jnp.float32


def _rms(x, g):
    return x * lax.rsqrt(jnp.mean(x * x, axis=-1, keepdims=True) + EPS) * g


def _dot(a, b):
    return jnp.dot(a.astype(BF16), b, preferred_element_type=F32)


def _sigmoid(x):
    return 0.5 + 0.5 * jnp.tanh(0.5 * x)


def _silu(x):
    half = 0.5 * x
    return half + half * jnp.tanh(half)


def _after(value, anchor):
    bits = lax.bitcast_convert_type(anchor, jnp.int32)
    zero = lax.shift_right_logical(lax.shift_right_logical(bits, 16), 16)
    return value + zero.astype(value.dtype)


def _store_piece(dst_ref, src_ref, step, n_pieces):
    rows = src_ref.shape[0]

    @pl.when(step < n_pieces)
    def _():
        dst_ref[pl.ds(pl.multiple_of(step * rows, rows), rows), :] = src_ref[...].astype(BF16)


def _ffn_loop_kernel(*refs, n_prep, n_prompt, n_blocks, with_ple):
    if with_ple:
        (x_hbm, pp_hbm, ps_hbm, pre_ref, post_ref, wg_ref, wu_ref, wd_ref, wpg_ref, wpp_ref, pleg_ref,
         op_hbm, os_hbm, wgu_s, wd_s, xn_s, a_s, xbuf, obuf, in_sem, out_sem, wst_g, wst_u, wst_d, w_sem,
         wpg_s, wpp_s, hb_s, pbuf, p_sem, wpg_st, wpp_st, e_sem) = refs
    else:
        (xp_hbm, xs_hbm, pre_ref, post_ref, wg_ref, wu_ref, wd_ref, o_hbm,
         wgu_s, wd_s, xn_s, a_s, xbuf, obuf, in_sem, out_sem, wst_g, wst_u, wst_d, w_sem) = refs
    tm = xn_s.shape[0]
    n_x, n_o = xbuf.shape[0], obuf.shape[0]
    n_slices = FINISH_SLICES // 2 if with_ple else FINISH_SLICES
    slice_rows = tm // n_slices
    stages = [("residual", s) for s in range(n_slices)]
    if with_ple:
        stages += [("embed_dots", 0)] + [("embed_add", s) for s in range(n_slices)]
    assert FINISH_FIRST_CHUNK + len(stages) <= n_prep

    def rows_of(b, prompt_hbm, sample_hbm):
        if isinstance(b, int):
            return (prompt_hbm.at[pl.ds(b * tm, tm)] if b < n_prompt else None,
                    sample_hbm.at[pl.ds((b - n_prompt) * tm, tm)] if b >= n_prompt else None)
        return (prompt_hbm.at[pl.ds(pl.multiple_of(b * tm, tm), tm)],
                sample_hbm.at[pl.ds(pl.multiple_of((b - n_prompt) * tm, tm), tm)])

    def start_either(b, make_copy, prompt_hbm, sample_hbm):
        in_prompt, in_sample = rows_of(b, prompt_hbm, sample_hbm)
        if isinstance(b, int):
            make_copy(in_prompt if in_sample is None else in_sample).start()
            return

        @pl.when(b < n_prompt)
        def _():
            make_copy(in_prompt).start()

        @pl.when(b >= n_prompt)
        def _():
            make_copy(in_sample).start()

    def x_copy(b):
        return lambda src: pltpu.make_async_copy(src, xbuf.at[b % n_x], in_sem.at[b % n_x])

    def start_x(b):
        if with_ple:
            row0 = b * tm if isinstance(b, int) else pl.multiple_of(b * tm, tm)
            x_copy(b)(x_hbm.at[pl.ds(row0, tm)]).start()
        else:
            start_either(b, x_copy(b), xp_hbm, xs_hbm)

    def wait_x(b):
        x_copy(b)((x_hbm if with_ple else xp_hbm).at[pl.ds(0, tm)]).wait()

    def p_copy(b):
        return lambda src: pltpu.make_async_copy(src, pbuf.at[b % n_o], p_sem.at[b % n_o])

    def out_copy(b):
        return lambda dst: pltpu.make_async_copy(obuf.at[b % n_o], dst, out_sem.at[b % n_o])

    def start_out(b):
        if with_ple:
            start_either(b, out_copy(b), op_hbm, os_hbm)
        else:
            row0 = b * tm if isinstance(b, int) else pl.multiple_of(b * tm, tm)
            out_copy(b)(o_hbm.at[pl.ds(row0, tm)]).start()

    def wait_out(b):
        out_copy(b)((op_hbm if with_ple else o_hbm).at[pl.ds(0, tm)]).wait()

    def gate_up_chunk(c):
        gu = jnp.dot(xn_s[...], wgu_s[c], preferred_element_type=F32)
        g, u = gu[:, 0:WEIGHT_TILE], gu[:, WEIGHT_TILE:]
        return g, (_silu(g) * u).astype(BF16)

    def down():
        y = jnp.dot(a_s[...], wd_s[...], preferred_element_type=F32)
        return dict(y=y, r=lax.rsqrt(jnp.mean(y * y, axis=-1, keepdims=True) + EPS))

    def ordered(r, rows, anchor):
        return r[rows] if anchor is None else _after(r[rows], anchor[rows, 0:1])

    def finish_stage(b, stage, s, st, anchor=None):
        rows = slice(s * slice_rows, (s + 1) * slice_rows)
        o = b % n_o
        if stage == "residual":
            h = xbuf[b % n_x, rows, :] + 0.5 * (st["y"][rows] * ordered(st["r"], rows, anchor) * post_ref[...])
            obuf[o, rows, :] = h
            if with_ple:
                hb_s[rows, :] = h.astype(BF16)
        elif stage == "embed_dots":
            gate = _sigmoid(jnp.dot(hb_s[...], wpg_s[...], preferred_element_type=F32))
            st["e"] = gate * _dot(pbuf[b % n_o], wpp_s[...])
            st["r2"] = lax.rsqrt(jnp.mean(st["e"] * st["e"], axis=-1, keepdims=True) + EPS)
        else:
            obuf[o, rows, :] = obuf[o, rows, :] + st["e"][rows] * ordered(st["r2"], rows, anchor) * pleg_ref[...]

    def weight_copies(c):
        slot, cols = c % wst_g.shape[0], slice(c * WEIGHT_TILE, (c + 1) * WEIGHT_TILE)
        return [pltpu.make_async_copy(wg_ref.at[:, cols], wst_g.at[slot], w_sem.at[0, slot]),
                pltpu.make_async_copy(wu_ref.at[:, cols], wst_u.at[slot], w_sem.at[1, slot]),
                pltpu.make_async_copy(wd_ref.at[cols, :], wst_d.at[slot], w_sem.at[2, slot])]

    def embed_weight_copies():
        return [pltpu.make_async_copy(wpg_ref, wpg_st, e_sem.at[0]), pltpu.make_async_copy(wpp_ref, wpp_st, e_sem.at[1])]

    n_w = wst_g.shape[0]
    start_x(0)
    for c in range(min(n_w, n_prep)):
        for copy in weight_copies(c):
            copy.start()
    if with_ple:
        start_either(0, p_copy(0), pp_hbm, ps_hbm)
        for copy in embed_weight_copies():
            copy.start()
    wait_x(0)
    if n_blocks > 1:
        start_x(1)
    xn_s[...] = _rms(xbuf[0], pre_ref[...]).astype(BF16)
    for c in range(n_prep):
        cols = slice(c * WEIGHT_TILE, (c + 1) * WEIGHT_TILE)
        for copy in weight_copies(c):
            copy.wait()
        wgu_s[c, :, 0:WEIGHT_TILE] = wst_g[c % n_w].astype(BF16)
        wgu_s[c, :, WEIGHT_TILE:] = wst_u[c % n_w].astype(BF16)
        wd_s[cols, :] = wst_d[c % n_w].astype(BF16)
        if c + n_w < n_prep:
            for copy in weight_copies(c + n_w):
                copy.start()
        _, a_s[:, cols] = gate_up_chunk(c)
    if with_ple:
        for copy in embed_weight_copies():
            copy.wait()
        wpg_s[...] = wpg_st[...].astype(BF16)
        wpp_s[...] = wpp_st[...].astype(BF16)

    def run_blocks():
        def step(b, carry):
            wait_x(b + 1)
            if with_ple:
                p_copy(b)(pp_hbm.at[pl.ds(0, tm)]).wait()
                start_either(b + 1, p_copy(b + 1), pp_hbm, ps_hbm)

            @pl.when(b + 2 < n_blocks)
            def _():
                start_x(b + 2)

            @pl.when(b >= n_o)
            def _():
                wait_out(b - n_o)

            st = down()
            x = _after(xbuf[(b + 1) % n_x], st["y"][:, 0:1])
            xn_s[...] = _rms(x, pre_ref[...]).astype(BF16)
            for c in range(n_prep):
                g, a_s[:, c * WEIGHT_TILE:(c + 1) * WEIGHT_TILE] = gate_up_chunk(c)
                k = c - FINISH_FIRST_CHUNK
                if 0 <= k < len(stages):
                    finish_stage(b, *stages[k], st, anchor=g)
            start_out(b)
            return carry

        lax.fori_loop(0, n_blocks - 1, step, 0)
        last = n_blocks - 1
        if with_ple:
            p_copy(last)(pp_hbm.at[pl.ds(0, tm)]).wait()
        if last >= n_o:
            wait_out(last - n_o)
        st = down()
        for stage, s in stages:
            finish_stage(last, stage, s, st)
        start_out(last)
        for b in range(max(last - n_o + 1, 0), last + 1):
            wait_out(b)

    run_blocks()


def _ffn_loop(x, pre_g, post_g, wg, wu, wd, ple=None, *, n_prompt, n_sample, name):
    tm = TOKEN_BLOCK
    d, d_ff = wg.shape
    n_prep = d_ff // WEIGHT_TILE
    n_blocks = n_prompt + n_sample
    assert n_prompt >= 1
    anywhere = pl.BlockSpec(memory_space=pl.ANY)
    weights = [_resident(pre_g), _resident(post_g), anywhere, anywhere, anywhere]
    scratch = [pltpu.VMEM((n_prep, d, 2 * WEIGHT_TILE), BF16), pltpu.VMEM((d_ff, d), BF16),
               pltpu.VMEM((tm, d), BF16), pltpu.VMEM((tm, d_ff), BF16),
               pltpu.VMEM((INPUT_RING, tm, d), F32), pltpu.VMEM((DOUBLE_BUFFER, tm, d), F32),
               pltpu.SemaphoreType.DMA((INPUT_RING,)), pltpu.SemaphoreType.DMA((DOUBLE_BUFFER,)),
               pltpu.VMEM((DOUBLE_BUFFER, d, WEIGHT_TILE), F32), pltpu.VMEM((DOUBLE_BUFFER, d, WEIGHT_TILE), F32),
               pltpu.VMEM((DOUBLE_BUFFER, WEIGHT_TILE, d), F32), pltpu.SemaphoreType.DMA((3, DOUBLE_BUFFER))]
    if ple is None:
        in_specs = [anywhere, anywhere] + weights
        args = [x[0], x[1], pre_g, post_g, wg, wu, wd]
        out_specs = anywhere
        out_shape = jax.ShapeDtypeStruct((n_blocks * tm, d), F32)
    else:
        pp, ps, wpg, wpp, ple_g = ple
        in_specs = [anywhere, anywhere, anywhere] + weights + [anywhere, anywhere, _resident(ple_g)]
        args = [x, pp, ps, pre_g, post_g, wg, wu, wd, wpg, wpp, ple_g]
        out_specs = [anywhere, anywhere]
        out_shape = [jax.ShapeDtypeStruct((n_prompt * tm, d), F32), jax.ShapeDtypeStruct((n_sample * tm, d), F32)]
        scratch += [pltpu.VMEM(wpg.shape, BF16), pltpu.VMEM(wpp.shape, BF16), pltpu.VMEM((tm, d), BF16),
                    pltpu.VMEM((DOUBLE_BUFFER, tm, pp.shape[1]), F32), pltpu.SemaphoreType.DMA((DOUBLE_BUFFER,)),
                    pltpu.VMEM(wpg.shape, F32), pltpu.VMEM(wpp.shape, F32), pltpu.SemaphoreType.DMA((2,))]
    return pl.pallas_call(
        functools.partial(_ffn_loop_kernel, n_prep=n_prep, n_prompt=n_prompt, n_blocks=n_blocks,
                          with_ple=ple is not None),
        grid=(1,),
        in_specs=in_specs,
        out_specs=out_specs,
        out_shape=out_shape,
        scratch_shapes=scratch,
        compiler_params=_params(),
        name=name,
    )(*args)


def _head_rms(v, ones_slab, g):
    vv = (v * v).astype(BF16)
    ms = jnp.concatenate(
        [jnp.dot(vv[:, k * V7X_MXU_COLS:(k + 1) * V7X_MXU_COLS], ones_slab, preferred_element_type=F32)
         for k in range(CHUNK_DIM // V7X_MXU_COLS)], axis=1)
    return v * lax.rsqrt(ms + EPS) * g


def _mix_kernel(hc_ref, hv_ref, state_ref, pre_ref, post_ref, win_ref, wout_ref, convw_ref, vg_ref,
                ws_ref, b_ref, ga_ref, gb_ref,
                o_ref, convp_ref, vp_ref, convs_ref, vs_ref,
                win_s, wout_s, n_s, yab_s, carry_ref, vn_s, vt_s, ones_ref, wslab_s, bias_s,
                *, n_prep, n_prompt, n_blocks, blocks_per_seq, seq_len):
    tm = hc_ref.shape[0]
    i = pl.program_id(0)
    j = i - n_prep

    @pl.when(i < n_prep)
    def _():
        win_s[i] = win_ref[...].astype(BF16)

    _store_piece(wout_s, wout_ref, i, wout_s.shape[0] // wout_ref.shape[0])

    @pl.when(i == 0)
    def _():
        yab_s[...] = jnp.zeros_like(yab_s)
        carry_ref[...] = jnp.zeros_like(carry_ref)
        row_head = lax.broadcasted_iota(jnp.int32, ones_ref.shape, 0) // HEAD_DIM_B
        col_head = lax.broadcasted_iota(jnp.int32, ones_ref.shape, 1) // HEAD_DIM_B
        ones_ref[...] = jnp.where(row_head == col_head, 1.0 / HEAD_DIM_B, 0.0).astype(BF16)

    assert MIX_FINISH_SLICES <= MIX_SEGMENTS
    slice_rows = tm // MIX_FINISH_SLICES

    def out_proj_prev():
        y = jnp.dot(yab_s[...], wout_s[...], preferred_element_type=F32)
        return dict(y=y, r=lax.rsqrt(jnp.mean(y * y, axis=-1, keepdims=True) + EPS), done=0)

    def finish_slice(prev, anchor=None):
        rows = slice(prev["done"] * slice_rows, (prev["done"] + 1) * slice_rows)
        prev["done"] += 1
        r = prev["r"][rows] if anchor is None else _after(prev["r"][rows], anchor[rows, 0:1])
        o_ref[rows, :] = hv_ref[rows, :] + prev["y"][rows] * r * post_ref[...]

    def seg(k, prev):
        out = jnp.dot(n_s[...], win_s[k], preferred_element_type=F32)
        if prev["done"] < MIX_FINISH_SLICES:
            finish_slice(prev, anchor=out)
        return out

    def conv(z, z1, z2):
        return convw_ref[0:1, :] * z2 + convw_ref[1:2, :] * z1 + convw_ref[2:3, :] * z

    def per_head(piece, rows):
        return jnp.concatenate([jnp.broadcast_to(piece(h), (rows, HEAD_DIM_B)) for h in range(HEADS_B)], axis=1)

    def v_gain():
        return jnp.concatenate([vg_ref[h:h + 1, :] for h in range(HEADS_B)], axis=1)

    def chunk_bias(rows):
        b_t = b_ref[...].T
        return per_head(lambda h: b_t[0:rows, h:h + 1], rows)

    @pl.when(i == 0)
    def _():
        t_idx = lax.broadcasted_iota(jnp.int32, (CHUNK, HEADS_PER_SLAB * CHUNK), 0)
        s_idx = lax.broadcasted_iota(jnp.int32, (CHUNK, HEADS_PER_SLAB * CHUNK), 1) % CHUNK
        for k in range(N_SLABS):
            w_k = jnp.concatenate([ws_ref[k * HEADS_PER_SLAB + hd] for hd in range(HEADS_PER_SLAB)], axis=1)
            wslab_s[k] = jnp.where(t_idx >= s_idx, w_k, 0.0).astype(BF16)
        bias_s[...] = chunk_bias(CHUNK)

    def group_norms(conv_out, mixed, prev):
        yab_s[:, 0:CONV_DIM] = _rms(seg(0, prev) * conv_out, ga_ref[...]).astype(BF16)
        yab_s[:, CONV_DIM:] = _rms(seg(3, prev) * mixed, gb_ref[...]).astype(BF16)

    def prompt_block():
        prev = out_proj_prev()
        n_s[...] = _rms(hc_ref[...], pre_ref[...]).astype(BF16)

        v_n = _head_rms(seg(4, prev), ones_ref[...], v_gain())
        vp_ref[0] = v_n[tm - CHUNK:, :].T
        lane_head = lax.broadcasted_iota(jnp.int32, (CHUNK, V7X_MXU_COLS), 1) // HEAD_DIM_B
        chunks = []
        for c in range(tm // CHUNK):
            slabs = []
            for k in range(N_SLABS):
                cols = slice(k * V7X_MXU_COLS, (k + 1) * V7X_MXU_COLS)
                v_ck = v_n[c * CHUNK:(c + 1) * CHUNK, cols].astype(BF16)
                bd = jnp.concatenate(
                    [jnp.where(lane_head == hd, v_ck, jnp.zeros((), BF16)) for hd in range(HEADS_PER_SLAB)], axis=0)
                slabs.append(jnp.dot(wslab_s[k], bd, preferred_element_type=F32))
            chunks.append(jnp.concatenate(slabs, axis=1) + bias_s[...])
        mixed = jnp.concatenate(chunks, axis=0)

        z = seg(1, prev) * seg(2, prev)
        row = lax.broadcasted_iota(jnp.int32, z.shape, 0)
        carry = jnp.where(j % blocks_per_seq == 0, 0.0, carry_ref[...])
        prev1 = carry[1:2, :]
        prev2 = carry[0:1, :]
        z1 = jnp.where(row == 0, prev1, pltpu.roll(z, 1, axis=0))
        z2 = jnp.where(row == 0, prev2, jnp.where(row == 1, prev1, pltpu.roll(z, 2, axis=0)))
        tail = z[tm - (CONV_K - 1):, :]
        carry_ref[...] = tail
        convp_ref[0] = tail
        group_norms(conv(z, z1, z2), mixed, prev)

    def sample_block():
        prev = out_proj_prev()
        n_s[...] = _rms(hc_ref[...], pre_ref[...]).astype(BF16)

        n_seq = tm // seq_len
        per_seq = lambda a: a.reshape(n_seq, seq_len, a.shape[-1])
        every_row = lambda a: jnp.broadcast_to(a, (n_seq, seq_len, a.shape[-1]))
        v_n = _head_rms(seg(4, prev), ones_ref[...], v_gain())
        first_seq = pl.multiple_of((j - n_prompt) * n_seq, n_seq)
        for c in range(CHUNK_DIM // V7X_LANES):
            lanes = slice(c * V7X_LANES, (c + 1) * V7X_LANES)
            vn_s[c] = v_n[:, lanes]
            for t in range(seq_len):
                vt_s[t, pl.ds(first_seq, n_seq), lanes] = vn_s[c, pl.ds(t, n_seq, stride=seq_len), :]

        @pl.when(j == n_blocks - 1)
        def _():
            for t in range(seq_len):
                vs_ref[t] = vt_s[t].T
        v_seq = per_seq(v_n)
        t_row = lax.broadcasted_iota(jnp.int32, (seq_len, CHUNK_DIM), 0)
        mixed = every_row(chunk_bias(seq_len)[None])
        for s in range(seq_len):
            w_from_s = per_head(lambda h: ws_ref[h, 0:seq_len, s:s + 1], seq_len)
            mixed = mixed + jnp.where(t_row >= s, w_from_s, 0.0)[None] * every_row(v_seq[:, s:s + 1, :])
        mixed = mixed.reshape(tm, CHUNK_DIM)

        z = seg(1, prev) * seg(2, prev)
        t = lax.broadcasted_iota(jnp.int32, z.shape, 0) % seq_len
        st = state_ref[...]
        prev1 = every_row(st[:, 1:2, :]).reshape(tm, CONV_DIM)
        prev2 = every_row(st[:, 0:1, :]).reshape(tm, CONV_DIM)
        z1 = jnp.where(t == 0, prev1, pltpu.roll(z, 1, axis=0))
        z2 = jnp.where(t == 0, prev2, jnp.where(t == 1, prev1, pltpu.roll(z, 2, axis=0)))
        convs_ref[...] = per_seq(z)[:, seq_len - (CONV_K - 1):, :]
        group_norms(conv(z, z1, z2), mixed, prev)

    @pl.when(jnp.logical_and(j >= 0, j < n_prompt))
    def _():
        prompt_block()

    @pl.when(jnp.logical_and(j >= n_prompt, j < n_blocks))
    def _():
        sample_block()

    @pl.when(j == n_blocks)
    def _():
        prev = out_proj_prev()
        for _ in range(MIX_FINISH_SLICES):
            finish_slice(prev)


def _resident(a):
    return pl.BlockSpec(a.shape, lambda i: (0,) * a.ndim, pipeline_mode=pl.Buffered(1))


def _clamped(lo, n):
    return lambda i: jnp.minimum(jnp.maximum(i - lo, 0), n - 1)


def _stream(tm, d, lo, n):
    blk = _clamped(lo, n)
    return pl.BlockSpec((tm, d), lambda i: (blk(i), 0))


def _row_pieces(w, rows):
    blk = _clamped(0, w.shape[0] // rows)
    return pl.BlockSpec((rows, w.shape[1]), lambda i: (blk(i), 0))


def _col_pieces(w, cols):
    blk = _clamped(0, w.shape[1] // cols)
    return pl.BlockSpec((w.shape[0], cols), lambda i: (0, blk(i)))


def _params():
    return pltpu.CompilerParams(dimension_semantics=("arbitrary",), vmem_limit_bytes=V7X_VMEM_LIMIT_BYTES)


def _mix(h, state, batch_p, seq_p, batch_s, seq_s, mw):
    tm = MIX_TOKEN_BLOCK
    d = h.shape[1]
    n_prep = MIX_SEGMENTS
    assert seq_p % tm == 0 and (batch_s * seq_s) % tm == 0 and tm % seq_s == 0
    n_prompt, n_sample = batch_p * seq_p // tm, batch_s * seq_s // tm
    n_blocks = n_prompt + n_sample
    blocks_per_seq = seq_p // tm
    seqs_per_block = tm // seq_s
    assert d // WEIGHT_TILE <= n_prep
    small = [mw["v_g"], mw["w_s"], mw["b_s"], mw["g_a"], mw["g_b"]]
    conv_w_spec = pl.BlockSpec((None,) + mw["conv_w"].shape[1:], lambda i: (0, 0, 0), pipeline_mode=pl.Buffered(1))
    lo_s = n_prep + n_prompt
    prompt_seq = _clamped(0, batch_p)
    per_prompt_seq = lambda i: (prompt_seq((i - n_prep) // blocks_per_seq), 0, 0)
    sample_blk = _clamped(lo_s, n_sample)
    per_sample_blk = pl.BlockSpec((seqs_per_block, CONV_K - 1, CONV_DIM), lambda i: (sample_blk(i), 0, 0))
    return pl.pallas_call(
        functools.partial(_mix_kernel, n_prep=n_prep, n_prompt=n_prompt, n_blocks=n_blocks,
                          blocks_per_seq=blocks_per_seq, seq_len=seq_s),
        grid=(n_prep + n_blocks + 1,),
        in_specs=[_stream(tm, d, n_prep, n_blocks), _stream(tm, d, n_prep + 1, n_blocks),
                  per_sample_blk, _resident(mw["pre"]), _resident(mw["post"]),
                  _col_pieces(mw["w_in"], CONV_DIM), _row_pieces(mw["w_out"], WEIGHT_TILE), conv_w_spec]
                 + [_resident(a) for a in small],
        out_specs=[_stream(tm, d, n_prep + 1, n_blocks),
                   pl.BlockSpec((1, CONV_K - 1, CONV_DIM), per_prompt_seq),
                   pl.BlockSpec((1, CHUNK_DIM, CHUNK), per_prompt_seq),
                   per_sample_blk,
                   pl.BlockSpec((seq_s, CHUNK_DIM, batch_s), lambda i: (0, 0, 0))],
        out_shape=[jax.ShapeDtypeStruct(h.shape, F32),
                   jax.ShapeDtypeStruct((batch_p, CONV_K - 1, CONV_DIM), F32),
                   jax.ShapeDtypeStruct((batch_p, CHUNK_DIM, CHUNK), F32),
                   jax.ShapeDtypeStruct((batch_s, CONV_K - 1, CONV_DIM), F32),
                   jax.ShapeDtypeStruct((seq_s, CHUNK_DIM, batch_s), F32)],
        scratch_shapes=[pltpu.VMEM((MIX_SEGMENTS, d, CONV_DIM), BF16), pltpu.VMEM(mw["w_out"].shape, BF16),
                        pltpu.VMEM((tm, d), BF16), pltpu.VMEM((tm, CONV_DIM + CHUNK_DIM), BF16),
                        pltpu.VMEM((CONV_K - 1, CONV_DIM), F32),
                        pltpu.VMEM((CHUNK_DIM // V7X_LANES, tm, V7X_LANES), F32),
                        pltpu.VMEM((seq_s, batch_s, CHUNK_DIM), F32),
                        pltpu.VMEM((V7X_MXU_COLS, V7X_MXU_COLS), BF16),
                        pltpu.VMEM((N_SLABS, CHUNK, HEADS_PER_SLAB * CHUNK), BF16),
                        pltpu.VMEM((CHUNK, CHUNK_DIM), F32)],
        compiler_params=_params(),
        name="mix",
    )(h, h, state, mw["pre"], mw["post"], mw["w_in"], mw["w_out"], mw["conv_w"], *small)


def _row(g):
    return g.reshape(1, -1).astype(F32)


def _mix_weights(mix_pre_g, mix_post_g, w_in, conv_w, v_norm_g, w_s, b_s, out_g_a, out_g_b, w_out):
    return dict(pre=_row(mix_pre_g), post=_row(mix_post_g), w_in=w_in, w_out=w_out,
                conv_w=conv_w.astype(F32), v_g=v_norm_g.astype(F32), w_s=w_s.astype(F32),
                b_s=b_s.astype(F32), g_a=_row(out_g_a), g_b=_row(out_g_b))


def kernel(x_prompt, x_sample, p_prompt, p_sample, state_conv, ffn1_pre_g, ffn1_post_g, ffn1_w_gate, ffn1_w_up, ffn1_w_down, mix_pre_g, mix_post_g, w_in, conv_w, v_norm_g, w_s, b_s, out_g_a, out_g_b, w_out, ffn2_pre_g, ffn2_post_g, ffn2_w_gate, ffn2_w_up, ffn2_w_down, ple_w_gate, ple_w_proj, ple_post_g):
    depth = ffn1_pre_g.shape[0]
    bp, lp, d = x_prompt.shape
    bs, ls, _ = x_sample.shape
    tm = TOKEN_BLOCK
    assert lp % tm == 0 and (bs * ls) % tm == 0 and tm % ls == 0
    assert CONV_K - 1 <= ls <= V7X_SUBLANES and (lp - 1) % CHUNK + 1 == CHUNK
    n_prompt, n_sample = bp * lp // tm, bs * ls // tm
    x = (x_prompt.reshape(bp * lp, d), x_sample.reshape(bs * ls, d))
    conv_p, conv_s, v_p, v_s = [], [], [], []
    for i in range(depth):
        mw = _mix_weights(mix_pre_g[i], mix_post_g[i], w_in[i], conv_w[i:i + 1], v_norm_g[i], w_s[i], b_s[i],
                          out_g_a[i], out_g_b[i], w_out[i])
        ple = (p_prompt[i].reshape(bp * lp, -1), p_sample[i].reshape(bs * ls, -1),
               ple_w_gate[i], ple_w_proj[i], _row(ple_post_g[i]))

        blocks = dict(n_prompt=n_prompt, n_sample=n_sample)
        h = _ffn_loop(x, _row(ffn1_pre_g[i]), _row(ffn1_post_g[i]),
                      ffn1_w_gate[i], ffn1_w_up[i], ffn1_w_down[i], name="ffn1", **blocks)
        h, cp, vp, cs, vs = _mix(h, state_conv[i].astype(F32), bp, lp, bs, ls, mw)
        x = _ffn_loop(h, _row(ffn2_pre_g[i]), _row(ffn2_post_g[i]),
                      ffn2_w_gate[i], ffn2_w_up[i], ffn2_w_down[i], ple, name="ffn2_ple", **blocks)

        conv_p.append(cp)
        conv_s.append(cs)
        v_p.append(vp.transpose(0, 2, 1).reshape(bp, CHUNK, HEADS_B, HEAD_DIM_B))
        v_s.append(vs.transpose(2, 0, 1).reshape(bs, ls, HEADS_B, HEAD_DIM_B))
    layers = lambda parts: parts[0][None] if depth == 1 else jnp.stack(parts)
    return (x[0].reshape(bp, lp, d), x[1].reshape(bs, ls, d),
            layers(conv_p), layers(conv_s), layers(v_p), layers(v_s))
```

```python
import functools

import jax
import jax.numpy as jnp
from jax import lax
from jax.experimental import pallas as pl
from jax.experimental.pallas import tpu as pltpu

CONV_K = 3
HEADS_B = 8
HEAD_DIM_B = 64
CHUNK_DIM = HEADS_B * HEAD_DIM_B
CONV_DIM = CHUNK_DIM
CHUNK = 128
EPS = 1e-6

V7X_SUBLANES = 8
V7X_LANES = 128
V7X_MXU_COLS = 256
V7X_VMEM_LIMIT_BYTES = 60 * 1024 * 1024

TOKEN_BLOCK = 512
INPUT_RING = 3
WEIGHT_RING = 4
DOUBLE_BUFFER = 2
MIX_TOKEN_BLOCK = 512
WEIGHT_TILE = V7X_MXU_COLS
HEADS_PER_SLAB = V7X_MXU_COLS // HEAD_DIM_B
N_SLABS = CHUNK_DIM // V7X_MXU_COLS
MIX_SEGMENTS = 5
FINISH_SLICES = 8
FINISH_FIRST_CHUNK = 1
MIX_FINISH_SLICES = 4

BF16 = jnp.bfloat16
F32 = jnp.float32


def _rms(x, g):
    return x * lax.rsqrt(jnp.mean(x * x, axis=-1, keepdims=True) + EPS) * g


def _dot(a, b):
    return jnp.dot(a.astype(BF16), b, preferred_element_type=F32)


def _sigmoid(x):
    return 0.5 + 0.5 * jnp.tanh(0.5 * x)


def _silu(x):
    half = 0.5 * x
    return half + half * jnp.tanh(half)


def _after(value, anchor):
    bits = lax.bitcast_convert_type(anchor, jnp.int32)
    zero = lax.shift_right_logical(lax.shift_right_logical(bits, 16), 16)
    return value + zero.astype(value.dtype)


def _store_piece(dst_ref, src_ref, step, n_pieces):
    rows = src_ref.shape[0]

    @pl.when(step < n_pieces)
    def _():
        dst_ref[pl.ds(pl.multiple_of(step * rows, rows), rows), :] = src_ref[...].astype(BF16)


def _ffn_loop_kernel(*refs, n_prep, n_prompt, n_blocks, with_ple):
    if with_ple:
        (x_hbm, pp_hbm, ps_hbm, pre_ref, post_ref, wg_ref, wu_ref, wd_ref, wpg_ref, wpp_ref, pleg_ref,
         op_hbm, os_hbm, wgu_s, wd_s, xn_s, a_s, xbuf, obuf, in_sem, out_sem, wst_g, wst_u, wst_d, w_sem,
         wpg_s, wpp_s, hb_s, pbuf, p_sem, wpg_st, wpp_st, e_sem) = refs
    else:
        (xp_hbm, xs_hbm, pre_ref, post_ref, wg_ref, wu_ref, wd_ref, o_hbm,
         wgu_s, wd_s, xn_s, a_s, xbuf, obuf, in_sem, out_sem, wst_g, wst_u, wst_d, w_sem) = refs
    tm = xn_s.shape[0]
    n_x, n_o = xbuf.shape[0], obuf.shape[0]
    n_slices = FINISH_SLICES // 2 if with_ple else FINISH_SLICES
    slice_rows = tm // n_slices
    stages = [("residual", s) for s in range(n_slices)]
    if with_ple:
        stages += [("embed_dots", 0)] + [("embed_add", s) for s in range(n_slices)]
    assert FINISH_FIRST_CHUNK + len(stages) <= n_prep

    def rows_of(b, prompt_hbm, sample_hbm):
        if isinstance(b, int):
            return (prompt_hbm.at[pl.ds(b * tm, tm)] if b < n_prompt else None,
                    sample_hbm.at[pl.ds((b - n_prompt) * tm, tm)] if b >= n_prompt else None)
        return (prompt_hbm.at[pl.ds(pl.multiple_of(b * tm, tm), tm)],
                sample_hbm.at[pl.ds(pl.multiple_of((b - n_prompt) * tm, tm), tm)])

    def start_either(b, make_copy, prompt_hbm, sample_hbm):
        in_prompt, in_sample = rows_of(b, prompt_hbm, sample_hbm)
        if isinstance(b, int):
            make_copy(in_prompt if in_sample is None else in_sample).start()
            return

        @pl.when(b < n_prompt)
        def _():
            make_copy(in_prompt).start()

        @pl.when(b >= n_prompt)
        def _():
            make_copy(in_sample).start()

    def x_copy(b):
        return lambda src: pltpu.make_async_copy(src, xbuf.at[b % n_x], in_sem.at[b % n_x])

    def start_x(b):
        if with_ple:
            row0 = b * tm if isinstance(b, int) else pl.multiple_of(b * tm, tm)
            x_copy(b)(x_hbm.at[pl.ds(row0, tm)]).start()
        else:
            start_either(b, x_copy(b), xp_hbm, xs_hbm)

    def wait_x(b):
        x_copy(b)((x_hbm if with_ple else xp_hbm).at[pl.ds(0, tm)]).wait()

    def p_copy(b):
        return lambda src: pltpu.make_async_copy(src, pbuf.at[b % n_o], p_sem.at[b % n_o])

    def out_copy(b):
        return lambda dst: pltpu.make_async_copy(obuf.at[b % n_o], dst, out_sem.at[b % n_o])

    def start_out(b):
        if with_ple:
            start_either(b, out_copy(b), op_hbm, os_hbm)
        else:
            row0 = b * tm if isinstance(b, int) else pl.multiple_of(b * tm, tm)
            out_copy(b)(o_hbm.at[pl.ds(row0, tm)]).start()

    def wait_out(b):
        out_copy(b)((op_hbm if with_ple else o_hbm).at[pl.ds(0, tm)]).wait()

    def gate_up_chunk(c):
        gu = jnp.dot(xn_s[...], wgu_s[c], preferred_element_type=F32)
        g, u = gu[:, 0:WEIGHT_TILE], gu[:, WEIGHT_TILE:]
        return g, (_silu(g) * u).astype(BF16)

    def down():
        y = jnp.dot(a_s[...], wd_s[...], preferred_element_type=F32)
        return dict(y=y, r=lax.rsqrt(jnp.mean(y * y, axis=-1, keepdims=True) + EPS))

    def ordered(r, rows, anchor):
        return r[rows] if anchor is None else _after(r[rows], anchor[rows, 0:1])

    def finish_stage(b, stage, s, st, anchor=None):
        rows = slice(s * slice_rows, (s + 1) * slice_rows)
        o = b % n_o
        if stage == "residual":
            h = xbuf[b % n_x, rows, :] + 0.5 * (st["y"][rows] * ordered(st["r"], rows, anchor) * post_ref[...])
            obuf[o, rows, :] = h
            if with_ple:
                hb_s[rows, :] = h.astype(BF16)
        elif stage == "embed_dots":
            gate = _sigmoid(jnp.dot(hb_s[...], wpg_s[...], preferred_element_type=F32))
            st["e"] = gate * _dot(pbuf[b % n_o], wpp_s[...])
            st["r2"] = lax.rsqrt(jnp.mean(st["e"] * st["e"], axis=-1, keepdims=True) + EPS)
        else:
            obuf[o, rows, :] = obuf[o, rows, :] + st["e"][rows] * ordered(st["r2"], rows, anchor) * pleg_ref[...]

    def weight_copies(c):
        slot, cols = c % wst_g.shape[0], slice(c * WEIGHT_TILE, (c + 1) * WEIGHT_TILE)
        return [pltpu.make_async_copy(wg_ref.at[:, cols], wst_g.at[slot], w_sem.at[0, slot]),
                pltpu.make_async_copy(wu_ref.at[:, cols], wst_u.at[slot], w_sem.at[1, slot]),
                pltpu.make_async_copy(wd_ref.at[cols, :], wst_d.at[slot], w_sem.at[2, slot])]

    def embed_weight_copies():
        return [pltpu.make_async_copy(wpg_ref, wpg_st, e_sem.at[0]), pltpu.make_async_copy(wpp_ref, wpp_st, e_sem.at[1])]

    n_w = wst_g.shape[0]
    start_x(0)
    for c in range(min(n_w, n_prep)):
        for copy in weight_copies(c):
            copy.start()
    if with_ple:
        start_either(0, p_copy(0), pp_hbm, ps_hbm)
        for copy in embed_weight_copies():
            copy.start()
    wait_x(0)
    if n_blocks > 1:
        start_x(1)
    xn_s[...] = _rms(xbuf[0], pre_ref[...]).astype(BF16)
    for c in range(n_prep):
        cols = slice(c * WEIGHT_TILE, (c + 1) * WEIGHT_TILE)
        for copy in weight_copies(c):
            copy.wait()
        wgu_s[c, :, 0:WEIGHT_TILE] = wst_g[c % n_w].astype(BF16)
        wgu_s[c, :, WEIGHT_TILE:] = wst_u[c % n_w].astype(BF16)
        wd_s[cols, :] = wst_d[c % n_w].astype(BF16)
        if c + n_w < n_prep:
            for copy in weight_copies(c + n_w):
                copy.start()
        _, a_s[:, cols] = gate_up_chunk(c)
    if with_ple:
        for copy in embed_weight_copies():
            copy.wait()
        wpg_s[...] = wpg_st[...].astype(BF16)
        wpp_s[...] = wpp_st[...].astype(BF16)

    def run_blocks():
        def step(b, carry):
            wait_x(b + 1)
            if with_ple:
                p_copy(b)(pp_hbm.at[pl.ds(0, tm)]).wait()
                start_either(b + 1, p_copy(b + 1), pp_hbm, ps_hbm)

            @pl.when(b + 2 < n_blocks)
            def _():
                start_x(b + 2)

            @pl.when(b >= n_o)
            def _():
                wait_out(b - n_o)

            st = down()
            x = _after(xbuf[(b + 1) % n_x], st["y"][:, 0:1])
            xn_s[...] = _rms(x, pre_ref[...]).astype(BF16)
            for c in range(n_prep):
                g, a_s[:, c * WEIGHT_TILE:(c + 1) * WEIGHT_TILE] = gate_up_chunk(c)
                k = c - FINISH_FIRST_CHUNK
                if 0 <= k < len(stages):
                    finish_stage(b, *stages[k], st, anchor=g)
            start_out(b)
            return carry

        lax.fori_loop(0, n_blocks - 1, step, 0)
        last = n_blocks - 1
        if with_ple:
            p_copy(last)(pp_hbm.at[pl.ds(0, tm)]).wait()
        if last >= n_o:
            wait_out(last - n_o)
        st = down()
        for stage, s in stages:
            finish_stage(last, stage, s, st)
        start_out(last)
        for b in range(max(last - n_o + 1, 0), last + 1):
            wait_out(b)

    run_blocks()


def _ffn_loop(x, pre_g, post_g, wg, wu, wd, ple=None, *, n_prompt, n_sample, name):
    tm = TOKEN_BLOCK
    d, d_ff = wg.shape
    n_prep = d_ff // WEIGHT_TILE
    n_blocks = n_prompt + n_sample
    assert n_prompt >= 1
    anywhere = pl.BlockSpec(memory_space=pl.ANY)
    weights = [_resident(pre_g), _resident(post_g), anywhere, anywhere, anywhere]
    scratch = [pltpu.VMEM((n_prep, d, 2 * WEIGHT_TILE), BF16), pltpu.VMEM((d_ff, d), BF16),
               pltpu.VMEM((tm, d), BF16), pltpu.VMEM((tm, d_ff), BF16),
               pltpu.VMEM((INPUT_RING, tm, d), F32), pltpu.VMEM((DOUBLE_BUFFER, tm, d), F32),
               pltpu.SemaphoreType.DMA((INPUT_RING,)), pltpu.SemaphoreType.DMA((DOUBLE_BUFFER,)),
               pltpu.VMEM((WEIGHT_RING, d, WEIGHT_TILE), F32), pltpu.VMEM((WEIGHT_RING, d, WEIGHT_TILE), F32),
               pltpu.VMEM((WEIGHT_RING, WEIGHT_TILE, d), F32), pltpu.SemaphoreType.DMA((3, WEIGHT_RING))]
    if ple is None:
        in_specs = [anywhere, anywhere] + weights
        args = [x[0], x[1], pre_g, post_g, wg, wu, wd]
        out_specs = anywhere
        out_shape = jax.ShapeDtypeStruct((n_blocks * tm, d), F32)
    else:
        pp, ps, wpg, wpp, ple_g = ple
        in_specs = [anywhere, anywhere, anywhere] + weights + [anywhere, anywhere, _resident(ple_g)]
        args = [x, pp, ps, pre_g, post_g, wg, wu, wd, wpg, wpp, ple_g]
        out_specs = [anywhere, anywhere]
        out_shape = [jax.ShapeDtypeStruct((n_prompt * tm, d), F32), jax.ShapeDtypeStruct((n_sample * tm, d), F32)]
        scratch += [pltpu.VMEM(wpg.shape, BF16), pltpu.VMEM(wpp.shape, BF16), pltpu.VMEM((tm, d), BF16),
                    pltpu.VMEM((DOUBLE_BUFFER, tm, pp.shape[1]), F32), pltpu.SemaphoreType.DMA((DOUBLE_BUFFER,)),
                    pltpu.VMEM(wpg.shape, F32), pltpu.VMEM(wpp.shape, F32), pltpu.SemaphoreType.DMA((2,))]
    return pl.pallas_call(
        functools.partial(_ffn_loop_kernel, n_prep=n_prep, n_prompt=n_prompt, n_blocks=n_blocks,
                          with_ple=ple is not None),
        grid=(1,),
        in_specs=in_specs,
        out_specs=out_specs,
        out_shape=out_shape,
        scratch_shapes=scratch,
        compiler_params=_params(),
        name=name,
    )(*args)


def _head_rms(v, ones_slab, g):
    vv = (v * v).astype(BF16)
    ms = jnp.concatenate(
        [jnp.dot(vv[:, k * V7X_MXU_COLS:(k + 1) * V7X_MXU_COLS], ones_slab, preferred_element_type=F32)
         for k in range(CHUNK_DIM // V7X_MXU_COLS)], axis=1)
    return v * lax.rsqrt(ms + EPS) * g


def _mix_kernel(hc_ref, hv_ref, state_ref, pre_ref, post_ref, win_ref, wout_ref, convw_ref, vg_ref,
                ws_ref, b_ref, ga_ref, gb_ref,
                o_ref, convp_ref, vp_ref, convs_ref, vs_ref,
                win_s, wout_s, n_s, yab_s, carry_ref, vn_s, vt_s, ones_ref, wslab_s, bias_s,
                *, n_prep, n_prompt, n_blocks, blocks_per_seq, seq_len):
    tm = hc_ref.shape[0]
    i = pl.program_id(0)
    j = i - n_prep

    @pl.when(i < n_prep)
    def _():
        win_s[i] = win_ref[...].astype(BF16)

    _store_piece(wout_s, wout_ref, i, wout_s.shape[0] // wout_ref.shape[0])

    @pl.when(i == 0)
    def _():
        yab_s[...] = jnp.zeros_like(yab_s)
        carry_ref[...] = jnp.zeros_like(carry_ref)
        row_head = lax.broadcasted_iota(jnp.int32, ones_ref.shape, 0) // HEAD_DIM_B
        col_head = lax.broadcasted_iota(jnp.int32, ones_ref.shape, 1) // HEAD_DIM_B
        ones_ref[...] = jnp.where(row_head == col_head, 1.0 / HEAD_DIM_B, 0.0).astype(BF16)

    assert MIX_FINISH_SLICES <= MIX_SEGMENTS
    slice_rows = tm // MIX_FINISH_SLICES

    def out_proj_prev():
        y = jnp.dot(yab_s[...], wout_s[...], preferred_element_type=F32)
        return dict(y=y, r=lax.rsqrt(jnp.mean(y * y, axis=-1, keepdims=True) + EPS), done=0)

    def finish_slice(prev, anchor=None):
        rows = slice(prev["done"] * slice_rows, (prev["done"] + 1) * slice_rows)
        prev["done"] += 1
        r = prev["r"][rows] if anchor is None else _after(prev["r"][rows], anchor[rows, 0:1])
        o_ref[rows, :] = hv_ref[rows, :] + prev["y"][rows] * r * post_ref[...]

    def seg(k, prev):
        out = jnp.dot(n_s[...], win_s[k], preferred_element_type=F32)
        if prev["done"] < MIX_FINISH_SLICES:
            finish_slice(prev, anchor=out)
        return out

    def conv(z, z1, z2):
        return convw_ref[0:1, :] * z2 + convw_ref[1:2, :] * z1 + convw_ref[2:3, :] * z

    def per_head(piece, rows):
        return jnp.concatenate([jnp.broadcast_to(piece(h), (rows, HEAD_DIM_B)) for h in range(HEADS_B)], axis=1)

    def v_gain():
        return jnp.concatenate([vg_ref[h:h + 1, :] for h in range(HEADS_B)], axis=1)

    def chunk_bias(rows):
        b_t = b_ref[...].T
        return per_head(lambda h: b_t[0:rows, h:h + 1], rows)

    @pl.when(i == 0)
    def _():
        t_idx = lax.broadcasted_iota(jnp.int32, (CHUNK, HEADS_PER_SLAB * CHUNK), 0)
        s_idx = lax.broadcasted_iota(jnp.int32, (CHUNK, HEADS_PER_SLAB * CHUNK), 1) % CHUNK
        for k in range(N_SLABS):
            w_k = jnp.concatenate([ws_ref[k * HEADS_PER_SLAB + hd] for hd in range(HEADS_PER_SLAB)], axis=1)
            wslab_s[k] = jnp.where(t_idx >= s_idx, w_k, 0.0).astype(BF16)
        bias_s[...] = chunk_bias(CHUNK)

    def group_norms(conv_out, mixed, prev):
        yab_s[:, 0:CONV_DIM] = _rms(seg(0, prev) * conv_out, ga_ref[...]).astype(BF16)
        yab_s[:, CONV_DIM:] = _rms(seg(3, prev) * mixed, gb_ref[...]).astype(BF16)

    def prompt_block():
        prev = out_proj_prev()
        n_s[...] = _rms(hc_ref[...], pre_ref[...]).astype(BF16)

        v_n = _head_rms(seg(4, prev), ones_ref[...], v_gain())
        vp_ref[0] = v_n[tm - CHUNK:, :].T
        lane_head = lax.broadcasted_iota(jnp.int32, (CHUNK, V7X_MXU_COLS), 1) // HEAD_DIM_B
        chunks = []
        for c in range(tm // CHUNK):
            slabs = []
            for k in range(N_SLABS):
                cols = slice(k * V7X_MXU_COLS, (k + 1) * V7X_MXU_COLS)
                v_ck = v_n[c * CHUNK:(c + 1) * CHUNK, cols].astype(BF16)
                bd = jnp.concatenate(
                    [jnp.where(lane_head == hd, v_ck, jnp.zeros((), BF16)) for hd in range(HEADS_PER_SLAB)], axis=0)
                slabs.append(jnp.dot(wslab_s[k], bd, preferred_element_type=F32))
            chunks.append(jnp.concatenate(slabs, axis=1) + bias_s[...])
        mixed = jnp.concatenate(chunks, axis=0)

        z = seg(1, prev) * seg(2, prev)
        row = lax.broadcasted_iota(jnp.int32, z.shape, 0)
        carry = jnp.where(j % blocks_per_seq == 0, 0.0, carry_ref[...])
        prev1 = carry[1:2, :]
        prev2 = carry[0:1, :]
        z1 = jnp.where(row == 0, prev1, pltpu.roll(z, 1, axis=0))
        z2 = jnp.where(row == 0, prev2, jnp.where(row == 1, prev1, pltpu.roll(z, 2, axis=0)))
        tail = z[tm - (CONV_K - 1):, :]
        carry_ref[...] = tail
        convp_ref[0] = tail
        group_norms(conv(z, z1, z2), mixed, prev)

    def sample_block():
        prev = out_proj_prev()
        n_s[...] = _rms(hc_ref[...], pre_ref[...]).astype(BF16)

        n_seq = tm // seq_len
        per_seq = lambda a: a.reshape(n_seq, seq_len, a.shape[-1])
        every_row = lambda a: jnp.broadcast_to(a, (n_seq, seq_len, a.shape[-1]))
        v_n = _head_rms(seg(4, prev), ones_ref[...], v_gain())
        first_seq = pl.multiple_of((j - n_prompt) * n_seq, n_seq)
        for c in range(CHUNK_DIM // V7X_LANES):
            lanes = slice(c * V7X_LANES, (c + 1) * V7X_LANES)
            vn_s[c] = v_n[:, lanes]
            for t in range(seq_len):
                vt_s[t, pl.ds(first_seq, n_seq), lanes] = vn_s[c, pl.ds(t, n_seq, stride=seq_len), :]

        @pl.when(j == n_blocks - 1)
        def _():
            for t in range(seq_len):
                vs_ref[t] = vt_s[t].T
        v_seq = per_seq(v_n)
        t_row = lax.broadcasted_iota(jnp.int32, (seq_len, CHUNK_DIM), 0)
        mixed = every_row(chunk_bias(seq_len)[None])
        for s in range(seq_len):
            w_from_s = per_head(lambda h: ws_ref[h, 0:seq_len, s:s + 1], seq_len)
            mixed = mixed + jnp.where(t_row >= s, w_from_s, 0.0)[None] * every_row(v_seq[:, s:s + 1, :])
        mixed = mixed.reshape(tm, CHUNK_DIM)

        z = seg(1, prev) * seg(2, prev)
        t = lax.broadcasted_iota(jnp.int32, z.shape, 0) % seq_len
        st = state_ref[...]
        prev1 = every_row(st[:, 1:2, :]).reshape(tm, CONV_DIM)
        prev2 = every_row(st[:, 0:1, :]).reshape(tm, CONV_DIM)
        z1 = jnp.where(t == 0, prev1, pltpu.roll(z, 1, axis=0))
        z2 = jnp.where(t == 0, prev2, jnp.where(t == 1, prev1, pltpu.roll(z, 2, axis=0)))
        convs_ref[...] = per_seq(z)[:, seq_len - (CONV_K - 1):, :]
        group_norms(conv(z, z1, z2), mixed, prev)

    @pl.when(jnp.logical_and(j >= 0, j < n_prompt))
    def _():
        prompt_block()

    @pl.when(jnp.logical_and(j >= n_prompt, j < n_blocks))
    def _():
        sample_block()

    @pl.when(j == n_blocks)
    def _():
        prev = out_proj_prev()
        for _ in range(MIX_FINISH_SLICES):
            finish_slice(prev)


def _resident(a):
    return pl.BlockSpec(a.shape, lambda i: (0,) * a.ndim, pipeline_mode=pl.Buffered(1))


def _clamped(lo, n):
    return lambda i: jnp.minimum(jnp.maximum(i - lo, 0), n - 1)


def _stream(tm, d, lo, n):
    blk = _clamped(lo, n)
    return pl.BlockSpec((tm, d), lambda i: (blk(i), 0))


def _row_pieces(w, rows):
    blk = _clamped(0, w.shape[0] // rows)
    return pl.BlockSpec((rows, w.shape[1]), lambda i: (blk(i), 0))


def _col_pieces(w, cols):
    blk = _clamped(0, w.shape[1] // cols)
    return pl.BlockSpec((w.shape[0], cols), lambda i: (0, blk(i)))


def _params():
    return pltpu.CompilerParams(dimension_semantics=("arbitrary",), vmem_limit_bytes=V7X_VMEM_LIMIT_BYTES)


def _mix(h, state, batch_p, seq_p, batch_s, seq_s, mw):
    tm = MIX_TOKEN_BLOCK
    d = h.shape[1]
    n_prep = MIX_SEGMENTS
    assert seq_p % tm == 0 and (batch_s * seq_s) % tm == 0 and tm % seq_s == 0
    n_prompt, n_sample = batch_p * seq_p // tm, batch_s * seq_s // tm
    n_blocks = n_prompt + n_sample
    blocks_per_seq = seq_p // tm
    seqs_per_block = tm // seq_s
    assert d // WEIGHT_TILE <= n_prep
    small = [mw["v_g"], mw["w_s"], mw["b_s"], mw["g_a"], mw["g_b"]]
    conv_w_spec = pl.BlockSpec((None,) + mw["conv_w"].shape[1:], lambda i: (0, 0, 0), pipeline_mode=pl.Buffered(1))
    lo_s = n_prep + n_prompt
    prompt_seq = _clamped(0, batch_p)
    per_prompt_seq = lambda i: (prompt_seq((i - n_prep) // blocks_per_seq), 0, 0)
    sample_blk = _clamped(lo_s, n_sample)
    per_sample_blk = pl.BlockSpec((seqs_per_block, CONV_K - 1, CONV_DIM), lambda i: (sample_blk(i), 0, 0))
    return pl.pallas_call(
        functools.partial(_mix_kernel, n_prep=n_prep, n_prompt=n_prompt, n_blocks=n_blocks,
                          blocks_per_seq=blocks_per_seq, seq_len=seq_s),
        grid=(n_prep + n_blocks + 1,),
        in_specs=[_stream(tm, d, n_prep, n_blocks), _stream(tm, d, n_prep + 1, n_blocks),
                  per_sample_blk, _resident(mw["pre"]), _resident(mw["post"]),
                  _col_pieces(mw["w_in"], CONV_DIM), _row_pieces(mw["w_out"], WEIGHT_TILE), conv_w_spec]
                 + [_resident(a) for a in small],
        out_specs=[_stream(tm, d, n_prep + 1, n_blocks),
                   pl.BlockSpec((1, CONV_K - 1, CONV_DIM), per_prompt_seq),
                   pl.BlockSpec((1, CHUNK_DIM, CHUNK), per_prompt_seq),
                   per_sample_blk,
                   pl.BlockSpec((seq_s, CHUNK_DIM, batch_s), lambda i: (0, 0, 0))],
        out_shape=[jax.ShapeDtypeStruct(h.shape, F32),
                   jax.ShapeDtypeStruct((batch_p, CONV_K - 1, CONV_DIM), F32),
                   jax.ShapeDtypeStruct((batch_p, CHUNK_DIM, CHUNK), F32),
                   jax.ShapeDtypeStruct((batch_s, CONV_K - 1, CONV_DIM), F32),
                   jax.ShapeDtypeStruct((seq_s, CHUNK_DIM, batch_s), F32)],
        scratch_shapes=[pltpu.VMEM((MIX_SEGMENTS, d, CONV_DIM), BF16), pltpu.VMEM(mw["w_out"].shape, BF16),
                        pltpu.VMEM((tm, d), BF16), pltpu.VMEM((tm, CONV_DIM + CHUNK_DIM), BF16),
                        pltpu.VMEM((CONV_K - 1, CONV_DIM), F32),
                        pltpu.VMEM((CHUNK_DIM // V7X_LANES, tm, V7X_LANES), F32),
                        pltpu.VMEM((seq_s, batch_s, CHUNK_DIM), F32),
                        pltpu.VMEM((V7X_MXU_COLS, V7X_MXU_COLS), BF16),
                        pltpu.VMEM((N_SLABS, CHUNK, HEADS_PER_SLAB * CHUNK), BF16),
                        pltpu.VMEM((CHUNK, CHUNK_DIM), F32)],
        compiler_params=_params(),
        name="mix",
    )(h, h, state, mw["pre"], mw["post"], mw["w_in"], mw["w_out"], mw["conv_w"], *small)


def _row(g):
    return g.reshape(1, -1).astype(F32)


def _mix_weights(mix_pre_g, mix_post_g, w_in, conv_w, v_norm_g, w_s, b_s, out_g_a, out_g_b, w_out):
    return dict(pre=_row(mix_pre_g), post=_row(mix_post_g), w_in=w_in, w_out=w_out,
                conv_w=conv_w.astype(F32), v_g=v_norm_g.astype(F32), w_s=w_s.astype(F32),
                b_s=b_s.astype(F32), g_a=_row(out_g_a), g_b=_row(out_g_b))


def kernel(x_prompt, x_sample, p_prompt, p_sample, state_conv, ffn1_pre_g, ffn1_post_g, ffn1_w_gate, ffn1_w_up, ffn1_w_down, mix_pre_g, mix_post_g, w_in, conv_w, v_norm_g, w_s, b_s, out_g_a, out_g_b, w_out, ffn2_pre_g, ffn2_post_g, ffn2_w_gate, ffn2_w_up, ffn2_w_down, ple_w_gate, ple_w_proj, ple_post_g):
    depth = ffn1_pre_g.shape[0]
    bp, lp, d = x_prompt.shape
    bs, ls, _ = x_sample.shape
    tm = TOKEN_BLOCK
    assert lp % tm == 0 and (bs * ls) % tm == 0 and tm % ls == 0
    assert CONV_K - 1 <= ls <= V7X_SUBLANES and (lp - 1) % CHUNK + 1 == CHUNK
    n_prompt, n_sample = bp * lp // tm, bs * ls // tm
    x = (x_prompt.reshape(bp * lp, d), x_sample.reshape(bs * ls, d))
    conv_p, conv_s, v_p, v_s = [], [], [], []
    for i in range(depth):
        mw = _mix_weights(mix_pre_g[i], mix_post_g[i], w_in[i], conv_w[i:i + 1], v_norm_g[i], w_s[i], b_s[i],
                          out_g_a[i], out_g_b[i], w_out[i])
        ple = (p_prompt[i].reshape(bp * lp, -1), p_sample[i].reshape(bs * ls, -1),
               ple_w_gate[i], ple_w_proj[i], _row(ple_post_g[i]))

        blocks = dict(n_prompt=n_prompt, n_sample=n_sample)
        h = _ffn_loop(x, _row(ffn1_pre_g[i]), _row(ffn1_post_g[i]),
                      ffn1_w_gate[i], ffn1_w_up[i], ffn1_w_down[i], name="ffn1", **blocks)
        h, cp, vp, cs, vs = _mix(h, state_conv[i].astype(F32), bp, lp, bs, ls, mw)
        x = _ffn_loop(h, _row(ffn2_pre_g[i]), _row(ffn2_post_g[i]),
                      ffn2_w_gate[i], ffn2_w_up[i], ffn2_w_down[i], ple, name="ffn2_ple", **blocks)

        conv_p.append(cp)
        conv_s.append(cs)
        v_p.append(vp.transpose(0, 2, 1).reshape(bp, CHUNK, HEADS_B, HEAD_DIM_B))
        v_s.append(vs.transpose(2, 0, 1).reshape(bs, ls, HEADS_B, HEAD_DIM_B))
    layers = lambda parts: parts[0][None] if depth == 1 else jnp.stack(parts)
    return (x[0].reshape(bp, lp, d), x[1].reshape(bs, ls, d),
            layers(conv_p), layers(conv_s), layers(v_p), layers(v_s))
```

```python
import functools

import jax
import jax.numpy as jnp
from jax import lax
from jax.experimental import pallas as pl
from jax.experimental.pallas import tpu as pltpu

CONV_K = 3
HEADS_B = 8
HEAD_DIM_B = 64
CHUNK_DIM = HEADS_B * HEAD_DIM_B
CONV_DIM = CHUNK_DIM
CHUNK = 128
EPS = 1e-6

V7X_SUBLANES = 8
V7X_LANES = 128
V7X_MXU_COLS = 256
V7X_VMEM_LIMIT_BYTES = 60 * 1024 * 1024

TOKEN_BLOCK = 512
INPUT_RING = 3
WEIGHT_RING = 4
DOUBLE_BUFFER = 2
MIX_TOKEN_BLOCK = 512
WEIGHT_TILE = V7X_MXU_COLS
HEADS_PER_SLAB = V7X_MXU_COLS // HEAD_DIM_B
N_SLABS = CHUNK_DIM // V7X_MXU_COLS
MIX_SEGMENTS = 5
FINISH_SLICES = 8
FINISH_FIRST_CHUNK = 1
MIX_FINISH_SLICES = 4

BF16 = jnp.bfloat16
F32 = jnp.float32


def _rms(x, g):
    return x * lax.rsqrt(jnp.mean(x * x, axis=-1, keepdims=True) + EPS) * g


def _dot(a, b):
    return jnp.dot(a.astype(BF16), b, preferred_element_type=F32)


def _sigmoid(x):
    return 0.5 + 0.5 * jnp.tanh(0.5 * x)


def _silu(x):
    half = 0.5 * x
    return half + half * jnp.tanh(half)


def _after(value, anchor):
    bits = lax.bitcast_convert_type(anchor, jnp.int32)
    zero = lax.shift_right_logical(lax.shift_right_logical(bits, 16), 16)
    return value + zero.astype(value.dtype)


def _store_piece(dst_ref, src_ref, step, n_pieces):
    rows = src_ref.shape[0]

    @pl.when(step < n_pieces)
    def _():
        dst_ref[pl.ds(pl.multiple_of(step * rows, rows), rows), :] = src_ref[...].astype(BF16)


def _ffn_loop_kernel(*refs, n_prep, n_prompt, n_blocks, with_ple):
    if with_ple:
        (x_hbm, pp_hbm, ps_hbm, pre_ref, post_ref, wg_ref, wu_ref, wd_ref, wpg_ref, wpp_ref, pleg_ref,
         op_hbm, os_hbm, wgu_s, wd_s, xn_s, a_s, xbuf, obuf, in_sem, out_sem, wst_g, wst_u, wst_d, w_sem,
         wpg_s, wpp_s, hb_s, pbuf, p_sem, wpg_st, wpp_st, e_sem) = refs
    else:
        (xp_hbm, xs_hbm, pre_ref, post_ref, wg_ref, wu_ref, wd_ref, o_hbm,
         wgu_s, wd_s, xn_s, a_s, xbuf, obuf, in_sem, out_sem, wst_g, wst_u, wst_d, w_sem) = refs
    tm = xn_s.shape[0]
    n_x, n_o = xbuf.shape[0], obuf.shape[0]
    n_slices = FINISH_SLICES // 2 if with_ple else FINISH_SLICES
    slice_rows = tm // n_slices
    stages = [("residual", s) for s in range(n_slices)]
    if with_ple:
        stages += [("embed_dots", 0)] + [("embed_add", s) for s in range(n_slices)]
    assert FINISH_FIRST_CHUNK + len(stages) <= n_prep

    def rows_of(b, prompt_hbm, sample_hbm):
        if isinstance(b, int):
            return (prompt_hbm.at[pl.ds(b * tm, tm)] if b < n_prompt else None,
                    sample_hbm.at[pl.ds((b - n_prompt) * tm, tm)] if b >= n_prompt else None)
        return (prompt_hbm.at[pl.ds(pl.multiple_of(b * tm, tm), tm)],
                sample_hbm.at[pl.ds(pl.multiple_of((b - n_prompt) * tm, tm), tm)])

    def start_either(b, make_copy, prompt_hbm, sample_hbm):
        in_prompt, in_sample = rows_of(b, prompt_hbm, sample_hbm)
        if isinstance(b, int):
            make_copy(in_prompt if in_sample is None else in_sample).start()
            return

        @pl.when(b < n_prompt)
        def _():
            make_copy(in_prompt).start()

        @pl.when(b >= n_prompt)
        def _():
            make_copy(in_sample).start()

    def x_copy(b):
        return lambda src: pltpu.make_async_copy(src, xbuf.at[b % n_x], in_sem.at[b % n_x])

    def start_x(b):
        if with_ple:
            row0 = b * tm if isinstance(b, int) else pl.multiple_of(b * tm, tm)
            x_copy(b)(x_hbm.at[pl.ds(row0, tm)]).start()
        else:
            start_either(b, x_copy(b), xp_hbm, xs_hbm)

    def wait_x(b):
        x_copy(b)((x_hbm if with_ple else xp_hbm).at[pl.ds(0, tm)]).wait()

    def p_copy(b):
        return lambda src: pltpu.make_async_copy(src, pbuf.at[b % n_o], p_sem.at[b % n_o])

    def out_copy(b):
        return lambda dst: pltpu.make_async_copy(obuf.at[b % n_o], dst, out_sem.at[b % n_o])

    def start_out(b):
        if with_ple:
            start_either(b, out_copy(b), op_hbm, os_hbm)
        else:
            row0 = b * tm if isinstance(b, int) else pl.multiple_of(b * tm, tm)
            out_copy(b)(o_hbm.at[pl.ds(row0, tm)]).start()

    def wait_out(b):
        out_copy(b)((op_hbm if with_ple else o_hbm).at[pl.ds(0, tm)]).wait()

    def gate_up_chunk(c):
        gu = jnp.dot(xn_s[...], wgu_s[c], preferred_element_type=F32)
        g, u = gu[:, 0:WEIGHT_TILE], gu[:, WEIGHT_TILE:]
        return g, (_silu(g) * u).astype(BF16)

    def down():
        y = jnp.dot(a_s[...], wd_s[...], preferred_element_type=F32)
        return dict(y=y, r=lax.rsqrt(jnp.mean(y * y, axis=-1, keepdims=True) + EPS))

    def ordered(r, rows, anchor):
        return r[rows] if anchor is None else _after(r[rows], anchor[rows, 0:1])

    def finish_stage(b, stage, s, st, anchor=None):
        rows = slice(s * slice_rows, (s + 1) * slice_rows)
        o = b % n_o
        if stage == "residual":
            h = xbuf[b % n_x, rows, :] + 0.5 * (st["y"][rows] * ordered(st["r"], rows, anchor) * post_ref[...])
            obuf[o, rows, :] = h
            if with_ple:
                hb_s[rows, :] = h.astype(BF16)
        elif stage == "embed_dots":
            gate = _sigmoid(jnp.dot(hb_s[...], wpg_s[...], preferred_element_type=F32))
            st["e"] = gate * _dot(pbuf[b % n_o], wpp_s[...])
            st["r2"] = lax.rsqrt(jnp.mean(st["e"] * st["e"], axis=-1, keepdims=True) + EPS)
        else:
            obuf[o, rows, :] = obuf[o, rows, :] + st["e"][rows] * ordered(st["r2"], rows, anchor) * pleg_ref[...]

    def weight_copies(c):
        slot, cols = c % wst_g.shape[0], slice(c * WEIGHT_TILE, (c + 1) * WEIGHT_TILE)
        return [pltpu.make_async_copy(wg_ref.at[:, cols], wst_g.at[slot], w_sem.at[0, slot]),
                pltpu.make_async_copy(wu_ref.at[:, cols], wst_u.at[slot], w_sem.at[1, slot]),
                pltpu.make_async_copy(wd_ref.at[cols, :], wst_d.at[slot], w_sem.at[2, slot])]

    def embed_weight_copies():
        return [pltpu.make_async_copy(wpg_ref, wpg_st, e_sem.at[0]), pltpu.make_async_copy(wpp_ref, wpp_st, e_sem.at[1])]

    n_w = wst_g.shape[0]
    start_x(0)
    for c in range(min(n_w, n_prep)):
        for copy in weight_copies(c):
            copy.start()
    if with_ple:
        start_either(0, p_copy(0), pp_hbm, ps_hbm)
        for copy in embed_weight_copies():
            copy.start()
    wait_x(0)
    if n_blocks > 1:
        start_x(1)
    xn_s[...] = _rms(xbuf[0], pre_ref[...]).astype(BF16)
    for c in range(n_prep):
        cols = slice(c * WEIGHT_TILE, (c + 1) * WEIGHT_TILE)
        for copy in weight_copies(c):
            copy.wait()
        wgu_s[c, :, 0:WEIGHT_TILE] = wst_g[c % n_w].astype(BF16)
        wgu_s[c, :, WEIGHT_TILE:] = wst_u[c % n_w].astype(BF16)
        wd_s[cols, :] = wst_d[c % n_w].astype(BF16)
        if c + n_w < n_prep:
            for copy in weight_copies(c + n_w):
                copy.start()
        _, a_s[:, cols] = gate_up_chunk(c)
    if with_ple:
        for copy in embed_weight_copies():
            copy.wait()
        wpg_s[...] = wpg_st[...].astype(BF16)
        wpp_s[...] = wpp_st[...].astype(BF16)

    def run_blocks():
        def step(b, carry):
            wait_x(b + 1)
            if with_ple:
                p_copy(b)(pp_hbm.at[pl.ds(0, tm)]).wait()
                start_either(b + 1, p_copy(b + 1), pp_hbm, ps_hbm)

            @pl.when(b + 2 < n_blocks)
            def _():
                start_x(b + 2)

            @pl.when(b >= n_o)
            def _():
                wait_out(b - n_o)

            st = down()
            x = _after(xbuf[(b + 1) % n_x], st["y"][:, 0:1])
            xn_s[...] = _rms(x, pre_ref[...]).astype(BF16)
            for c in range(n_prep):
                g, a_s[:, c * WEIGHT_TILE:(c + 1) * WEIGHT_TILE] = gate_up_chunk(c)
                k = c - FINISH_FIRST_CHUNK
                if 0 <= k < len(stages):
                    finish_stage(b, *stages[k], st, anchor=g)
            start_out(b)
            return carry

        lax.fori_loop(0, n_blocks - 1, step, 0)
        last = n_blocks - 1
        if with_ple:
            p_copy(last)(pp_hbm.at[pl.ds(0, tm)]).wait()
        if last >= n_o:
            wait_out(last - n_o)
        st = down()
        for stage, s in stages:
            finish_stage(last, stage, s, st)
        start_out(last)
        for b in range(max(last - n_o + 1, 0), last + 1):
            wait_out(b)

    run_blocks()


def _ffn_loop(x, pre_g, post_g, wg, wu, wd, ple=None, *, n_prompt, n_sample, name):
    tm = TOKEN_BLOCK
    d, d_ff = wg.shape
    n_prep = d_ff // WEIGHT_TILE
    n_blocks = n_prompt + n_sample
    assert n_prompt >= 1
    anywhere = pl.BlockSpec(memory_space=pl.ANY)
    weights = [_resident(pre_g), _resident(post_g), anywhere, anywhere, anywhere]
    scratch = [pltpu.VMEM((n_prep, d, 2 * WEIGHT_TILE), BF16), pltpu.VMEM((d_ff, d), BF16),
               pltpu.VMEM((tm, d), BF16), pltpu.VMEM((tm, d_ff), BF16),
               pltpu.VMEM((INPUT_RING, tm, d), F32), pltpu.VMEM((DOUBLE_BUFFER, tm, d), F32),
               pltpu.SemaphoreType.DMA((INPUT_RING,)), pltpu.SemaphoreType.DMA((DOUBLE_BUFFER,)),
               pltpu.VMEM((WEIGHT_RING, d, WEIGHT_TILE), F32), pltpu.VMEM((WEIGHT_RING, d, WEIGHT_TILE), F32),
               pltpu.VMEM((WEIGHT_RING, WEIGHT_TILE, d), F32), pltpu.SemaphoreType.DMA((3, WEIGHT_RING))]
    if ple is None:
        in_specs = [anywhere, anywhere] + weights
        args = [x[0], x[1], pre_g, post_g, wg, wu, wd]
        out_specs = anywhere
        out_shape = jax.ShapeDtypeStruct((n_blocks * tm, d), F32)
    else:
        pp, ps, wpg, wpp, ple_g = ple
        in_specs = [anywhere, anywhere, anywhere] + weights + [anywhere, anywhere, _resident(ple_g)]
        args = [x, pp, ps, pre_g, post_g, wg, wu, wd, wpg, wpp, ple_g]
        out_specs = [anywhere, anywhere]
        out_shape = [jax.ShapeDtypeStruct((n_prompt * tm, d), F32), jax.ShapeDtypeStruct((n_sample * tm, d), F32)]
        scratch += [pltpu.VMEM(wpg.shape, BF16), pltpu.VMEM(wpp.shape, BF16), pltpu.VMEM((tm, d), BF16),
                    pltpu.VMEM((DOUBLE_BUFFER, tm, pp.shape[1]), F32), pltpu.SemaphoreType.DMA((DOUBLE_BUFFER,)),
                    pltpu.VMEM(wpg.shape, F32), pltpu.VMEM(wpp.shape, F32), pltpu.SemaphoreType.DMA((2,))]
    return pl.pallas_call(
        functools.partial(_ffn_loop_kernel, n_prep=n_prep, n_prompt=n_prompt, n_blocks=n_blocks,
                          with_ple=ple is not None),
        grid=(1,),
        in_specs=in_specs,
        out_specs=out_specs,
        out_shape=out_shape,
        scratch_shapes=scratch,
        compiler_params=_params(),
        name=name,
    )(*args)


def _head_rms(v, ones_slab, g):
    vv = (v * v).astype(BF16)
    ms = jnp.concatenate(
        [jnp.dot(vv[:, k * V7X_MXU_COLS:(k + 1) * V7X_MXU_COLS], ones_slab, preferred_element_type=F32)
         for k in range(CHUNK_DIM // V7X_MXU_COLS)], axis=1)
    return v * lax.rsqrt(ms + EPS) * g


def _mix_kernel(hc_ref, hv_ref, state_ref, pre_ref, post_ref, win_ref, wout_ref, convw_ref, vg_ref,
                ws_ref, b_ref, ga_ref, gb_ref,
                o_ref, convp_ref, vp_ref, convs_ref, vs_ref,
                win_s, wout_s, n_s, yab_s, carry_ref, vn_s, vt_s, ones_ref, wslab_s, bias_s,
                win_st, wout_st, w_sem,
                *, n_prep, n_prompt, n_blocks, blocks_per_seq, seq_len):
    tm = hc_ref.shape[0]
    i = pl.program_id(0)
    j = i - n_prep

    @pl.when(i == 0)
    def _():
        width = win_s.shape[2]
        pieces = [pltpu.make_async_copy(win_ref.at[:, k * width:(k + 1) * width],
                                        win_st.at[:, k * width:(k + 1) * width], w_sem.at[k])
                  for k in range(MIX_SEGMENTS)]
        pieces.append(pltpu.make_async_copy(wout_ref, wout_st, w_sem.at[MIX_SEGMENTS]))
        for copy in pieces:
            copy.start()
        for k in range(MIX_SEGMENTS):
            pieces[k].wait()
            win_s[k] = win_st[:, k * width:(k + 1) * width].astype(BF16)
        pieces[MIX_SEGMENTS].wait()
        wout_s[...] = wout_st[...].astype(BF16)

    @pl.when(i == 0)
    def _():
        yab_s[...] = jnp.zeros_like(yab_s)
        carry_ref[...] = jnp.zeros_like(carry_ref)
        row_head = lax.broadcasted_iota(jnp.int32, ones_ref.shape, 0) // HEAD_DIM_B
        col_head = lax.broadcasted_iota(jnp.int32, ones_ref.shape, 1) // HEAD_DIM_B
        ones_ref[...] = jnp.where(row_head == col_head, 1.0 / HEAD_DIM_B, 0.0).astype(BF16)

    assert MIX_FINISH_SLICES <= MIX_SEGMENTS
    slice_rows = tm // MIX_FINISH_SLICES

    def out_proj_prev():
        y = jnp.dot(yab_s[...], wout_s[...], preferred_element_type=F32)
        return dict(y=y, r=lax.rsqrt(jnp.mean(y * y, axis=-1, keepdims=True) + EPS), done=0)

    def finish_slice(prev, anchor=None):
        rows = slice(prev["done"] * slice_rows, (prev["done"] + 1) * slice_rows)
        prev["done"] += 1
        r = prev["r"][rows] if anchor is None else _after(prev["r"][rows], anchor[rows, 0:1])
        o_ref[rows, :] = hv_ref[rows, :] + prev["y"][rows] * r * post_ref[...]

    def seg(k, prev):
        out = jnp.dot(n_s[...], win_s[k], preferred_element_type=F32)
        if prev["done"] < MIX_FINISH_SLICES:
            finish_slice(prev, anchor=out)
        return out

    def conv(z, z1, z2):
        return convw_ref[0:1, :] * z2 + convw_ref[1:2, :] * z1 + convw_ref[2:3, :] * z

    def per_head(piece, rows):
        return jnp.concatenate([jnp.broadcast_to(piece(h), (rows, HEAD_DIM_B)) for h in range(HEADS_B)], axis=1)

    def v_gain():
        return jnp.concatenate([vg_ref[h:h + 1, :] for h in range(HEADS_B)], axis=1)

    def chunk_bias(rows):
        b_t = b_ref[...].T
        return per_head(lambda h: b_t[0:rows, h:h + 1], rows)

    @pl.when(i == 0)
    def _():
        t_idx = lax.broadcasted_iota(jnp.int32, (CHUNK, HEADS_PER_SLAB * CHUNK), 0)
        s_idx = lax.broadcasted_iota(jnp.int32, (CHUNK, HEADS_PER_SLAB * CHUNK), 1) % CHUNK
        for k in range(N_SLABS):
            w_k = jnp.concatenate([ws_ref[k * HEADS_PER_SLAB + hd] for hd in range(HEADS_PER_SLAB)], axis=1)
            wslab_s[k] = jnp.where(t_idx >= s_idx, w_k, 0.0).astype(BF16)
        bias_s[...] = chunk_bias(CHUNK)

    def group_norms(conv_out, mixed, prev):
        yab_s[:, 0:CONV_DIM] = _rms(seg(0, prev) * conv_out, ga_ref[...]).astype(BF16)
        yab_s[:, CONV_DIM:] = _rms(seg(3, prev) * mixed, gb_ref[...]).astype(BF16)

    def prompt_block():
        prev = out_proj_prev()
        n_s[...] = _rms(hc_ref[...], pre_ref[...]).astype(BF16)

        v_n = _head_rms(seg(4, prev), ones_ref[...], v_gain())
        vp_ref[0] = v_n[tm - CHUNK:, :].T
        lane_head = lax.broadcasted_iota(jnp.int32, (CHUNK, V7X_MXU_COLS), 1) // HEAD_DIM_B
        chunks = []
        for c in range(tm // CHUNK):
            slabs = []
            for k in range(N_SLABS):
                cols = slice(k * V7X_MXU_COLS, (k + 1) * V7X_MXU_COLS)
                v_ck = v_n[c * CHUNK:(c + 1) * CHUNK, cols].astype(BF16)
                bd = jnp.concatenate(
                    [jnp.where(lane_head == hd, v_ck, jnp.zeros((), BF16)) for hd in range(HEADS_PER_SLAB)], axis=0)
                slabs.append(jnp.dot(wslab_s[k], bd, preferred_element_type=F32))
            chunks.append(jnp.concatenate(slabs, axis=1) + bias_s[...])
        mixed = jnp.concatenate(chunks, axis=0)

        z = seg(1, prev) * seg(2, prev)
        row = lax.broadcasted_iota(jnp.int32, z.shape, 0)
        carry = jnp.where(j % blocks_per_seq == 0, 0.0, carry_ref[...])
        prev1 = carry[1:2, :]
        prev2 = carry[0:1, :]
        z1 = jnp.where(row == 0, prev1, pltpu.roll(z, 1, axis=0))
        z2 = jnp.where(row == 0, prev2, jnp.where(row == 1, prev1, pltpu.roll(z, 2, axis=0)))
        tail = z[tm - (CONV_K - 1):, :]
        carry_ref[...] = tail
        convp_ref[0] = tail
        group_norms(conv(z, z1, z2), mixed, prev)

    def sample_block():
        prev = out_proj_prev()
        n_s[...] = _rms(hc_ref[...], pre_ref[...]).astype(BF16)

        n_seq = tm // seq_len
        per_seq = lambda a: a.reshape(n_seq, seq_len, a.shape[-1])
        every_row = lambda a: jnp.broadcast_to(a, (n_seq, seq_len, a.shape[-1]))
        v_n = _head_rms(seg(4, prev), ones_ref[...], v_gain())
        first_seq = pl.multiple_of((j - n_prompt) * n_seq, n_seq)
        for c in range(CHUNK_DIM // V7X_LANES):
            lanes = slice(c * V7X_LANES, (c + 1) * V7X_LANES)
            vn_s[c] = v_n[:, lanes]
            for t in range(seq_len):
                vt_s[t, pl.ds(first_seq, n_seq), lanes] = vn_s[c, pl.ds(t, n_seq, stride=seq_len), :]

        @pl.when(j == n_blocks - 1)
        def _():
            for t in range(seq_len):
                vs_ref[t] = vt_s[t].T
        v_seq = per_seq(v_n)
        t_row = lax.broadcasted_iota(jnp.int32, (seq_len, CHUNK_DIM), 0)
        mixed = every_row(chunk_bias(seq_len)[None])
        for s in range(seq_len):
            w_from_s = per_head(lambda h: ws_ref[h, 0:seq_len, s:s + 1], seq_len)
            mixed = mixed + jnp.where(t_row >= s, w_from_s, 0.0)[None] * every_row(v_seq[:, s:s + 1, :])
        mixed = mixed.reshape(tm, CHUNK_DIM)

        z = seg(1, prev) * seg(2, prev)
        t = lax.broadcasted_iota(jnp.int32, z.shape, 0) % seq_len
        st = state_ref[...]
        prev1 = every_row(st[:, 1:2, :]).reshape(tm, CONV_DIM)
        prev2 = every_row(st[:, 0:1, :]).reshape(tm, CONV_DIM)
        z1 = jnp.where(t == 0, prev1, pltpu.roll(z, 1, axis=0))
        z2 = jnp.where(t == 0, prev2, jnp.where(t == 1, prev1, pltpu.roll(z, 2, axis=0)))
        convs_ref[...] = per_seq(z)[:, seq_len - (CONV_K - 1):, :]
        group_norms(conv(z, z1, z2), mixed, prev)

    @pl.when(jnp.logical_and(j >= 0, j < n_prompt))
    def _():
        prompt_block()

    @pl.when(jnp.logical_and(j >= n_prompt, j < n_blocks))
    def _():
        sample_block()

    @pl.when(j == n_blocks)
    def _():
        prev = out_proj_prev()
        for _ in range(MIX_FINISH_SLICES):
            finish_slice(prev)


def _resident(a):
    return pl.BlockSpec(a.shape, lambda i: (0,) * a.ndim, pipeline_mode=pl.Buffered(1))


def _clamped(lo, n):
    return lambda i: jnp.minimum(jnp.maximum(i - lo, 0), n - 1)


def _stream(tm, d, lo, n):
    blk = _clamped(lo, n)
    return pl.BlockSpec((tm, d), lambda i: (blk(i), 0))


def _row_pieces(w, rows):
    blk = _clamped(0, w.shape[0] // rows)
    return pl.BlockSpec((rows, w.shape[1]), lambda i: (blk(i), 0))


def _col_pieces(w, cols):
    blk = _clamped(0, w.shape[1] // cols)
    return pl.BlockSpec((w.shape[0], cols), lambda i: (0, blk(i)))


def _params():
    return pltpu.CompilerParams(dimension_semantics=("arbitrary",), vmem_limit_bytes=V7X_VMEM_LIMIT_BYTES)


def _mix(h, state, batch_p, seq_p, batch_s, seq_s, mw):
    tm = MIX_TOKEN_BLOCK
    d = h.shape[1]
    n_prep = 1
    assert seq_p % tm == 0 and (batch_s * seq_s) % tm == 0 and tm % seq_s == 0
    n_prompt, n_sample = batch_p * seq_p // tm, batch_s * seq_s // tm
    n_blocks = n_prompt + n_sample
    blocks_per_seq = seq_p // tm
    seqs_per_block = tm // seq_s
    assert mw["w_in"].shape[1] == MIX_SEGMENTS * CONV_DIM
    anywhere = pl.BlockSpec(memory_space=pl.ANY)
    small = [mw["v_g"], mw["w_s"], mw["b_s"], mw["g_a"], mw["g_b"]]
    conv_w_spec = pl.BlockSpec((None,) + mw["conv_w"].shape[1:], lambda i: (0, 0, 0), pipeline_mode=pl.Buffered(1))
    lo_s = n_prep + n_prompt
    prompt_seq = _clamped(0, batch_p)
    per_prompt_seq = lambda i: (prompt_seq((i - n_prep) // blocks_per_seq), 0, 0)
    sample_blk = _clamped(lo_s, n_sample)
    per_sample_blk = pl.BlockSpec((seqs_per_block, CONV_K - 1, CONV_DIM), lambda i: (sample_blk(i), 0, 0))
    return pl.pallas_call(
        functools.partial(_mix_kernel, n_prep=n_prep, n_prompt=n_prompt, n_blocks=n_blocks,
                          blocks_per_seq=blocks_per_seq, seq_len=seq_s),
        grid=(n_prep + n_blocks + 1,),
        in_specs=[_stream(tm, d, n_prep, n_blocks), _stream(tm, d, n_prep + 1, n_blocks),
                  per_sample_blk, _resident(mw["pre"]), _resident(mw["post"]),
                  anywhere, anywhere, conv_w_spec]
                 + [_resident(a) for a in small],
        out_specs=[_stream(tm, d, n_prep + 1, n_blocks),
                   pl.BlockSpec((1, CONV_K - 1, CONV_DIM), per_prompt_seq),
                   pl.BlockSpec((1, CHUNK_DIM, CHUNK), per_prompt_seq),
                   per_sample_blk,
                   pl.BlockSpec((seq_s, CHUNK_DIM, batch_s), lambda i: (0, 0, 0))],
        out_shape=[jax.ShapeDtypeStruct(h.shape, F32),
                   jax.ShapeDtypeStruct((batch_p, CONV_K - 1, CONV_DIM), F32),
                   jax.ShapeDtypeStruct((batch_p, CHUNK_DIM, CHUNK), F32),
                   jax.ShapeDtypeStruct((batch_s, CONV_K - 1, CONV_DIM), F32),
                   jax.ShapeDtypeStruct((seq_s, CHUNK_DIM, batch_s), F32)],
        scratch_shapes=[pltpu.VMEM((MIX_SEGMENTS, d, CONV_DIM), BF16), pltpu.VMEM(mw["w_out"].shape, BF16),
                        pltpu.VMEM((tm, d), BF16), pltpu.VMEM((tm, CONV_DIM + CHUNK_DIM), BF16),
                        pltpu.VMEM((CONV_K - 1, CONV_DIM), F32),
                        pltpu.VMEM((CHUNK_DIM // V7X_LANES, tm, V7X_LANES), F32),
                        pltpu.VMEM((seq_s, batch_s, CHUNK_DIM), F32),
                        pltpu.VMEM((V7X_MXU_COLS, V7X_MXU_COLS), BF16),
                        pltpu.VMEM((N_SLABS, CHUNK, HEADS_PER_SLAB * CHUNK), BF16),
                        pltpu.VMEM((CHUNK, CHUNK_DIM), F32),
                        pltpu.VMEM(mw["w_in"].shape, F32), pltpu.VMEM(mw["w_out"].shape, F32),
                        pltpu.SemaphoreType.DMA((MIX_SEGMENTS + 1,))],
        compiler_params=_params(),
        name="mix",
    )(h, h, state, mw["pre"], mw["post"], mw["w_in"], mw["w_out"], mw["conv_w"], *small)


def _row(g):
    return g.reshape(1, -1).astype(F32)


def _mix_weights(mix_pre_g, mix_post_g, w_in, conv_w, v_norm_g, w_s, b_s, out_g_a, out_g_b, w_out):
    return dict(pre=_row(mix_pre_g), post=_row(mix_post_g), w_in=w_in, w_out=w_out,
                conv_w=conv_w.astype(F32), v_g=v_norm_g.astype(F32), w_s=w_s.astype(F32),
                b_s=b_s.astype(F32), g_a=_row(out_g_a), g_b=_row(out_g_b))


def kernel(x_prompt, x_sample, p_prompt, p_sample, state_conv, ffn1_pre_g, ffn1_post_g, ffn1_w_gate, ffn1_w_up, ffn1_w_down, mix_pre_g, mix_post_g, w_in, conv_w, v_norm_g, w_s, b_s, out_g_a, out_g_b, w_out, ffn2_pre_g, ffn2_post_g, ffn2_w_gate, ffn2_w_up, ffn2_w_down, ple_w_gate, ple_w_proj, ple_post_g):
    depth = ffn1_pre_g.shape[0]
    bp, lp, d = x_prompt.shape
    bs, ls, _ = x_sample.shape
    tm = TOKEN_BLOCK
    assert lp % tm == 0 and (bs * ls) % tm == 0 and tm % ls == 0
    assert CONV_K - 1 <= ls <= V7X_SUBLANES and (lp - 1) % CHUNK + 1 == CHUNK
    n_prompt, n_sample = bp * lp // tm, bs * ls // tm
    x = (x_prompt.reshape(bp * lp, d), x_sample.reshape(bs * ls, d))
    conv_p, conv_s, v_p, v_s = [], [], [], []
    for i in range(depth):
        mw = _mix_weights(mix_pre_g[i], mix_post_g[i], w_in[i], conv_w[i:i + 1], v_norm_g[i], w_s[i], b_s[i],
                          out_g_a[i], out_g_b[i], w_out[i])
        ple = (p_prompt[i].reshape(bp * lp, -1), p_sample[i].reshape(bs * ls, -1),
               ple_w_gate[i], ple_w_proj[i], _row(ple_post_g[i]))

        blocks = dict(n_prompt=n_prompt, n_sample=n_sample)
        h = _ffn_loop(x, _row(ffn1_pre_g[i]), _row(ffn1_post_g[i]),
                      ffn1_w_gate[i], ffn1_w_up[i], ffn1_w_down[i], name="ffn1", **blocks)
        h, cp, vp, cs, vs = _mix(h, state_conv[i].astype(F32), bp, lp, bs, ls, mw)
        x = _ffn_loop(h, _row(ffn2_pre_g[i]), _row(ffn2_post_g[i]),
                      ffn2_w_gate[i], ffn2_w_up[i], ffn2_w_down[i], ple, name="ffn2_ple", **blocks)

        conv_p.append(cp)
        conv_s.append(cs)
        v_p.append(vp.transpose(0, 2, 1).reshape(bp, CHUNK, HEADS_B, HEAD_DIM_B))
        v_s.append(vs.transpose(2, 0, 1).reshape(bs, ls, HEADS_B, HEAD_DIM_B))
    layers = lambda parts: parts[0][None] if depth == 1 else jnp.stack(parts)
    return (x[0].reshape(bp, lp, d), x[1].reshape(bs, ls, d),
            layers(conv_p), layers(conv_s), layers(v_p), layers(v_s))
```

```python
import functools

import jax
import jax.numpy as jnp
from jax import lax
from jax.experimental import pallas as pl
from jax.experimental.pallas import tpu as pltpu

CONV_K = 3
HEADS_B = 8
HEAD_DIM_B = 64
CHUNK_DIM = HEADS_B * HEAD_DIM_B
CONV_DIM = CHUNK_DIM
CHUNK = 128
EPS = 1e-6

V7X_SUBLANES = 8
V7X_LANES = 128
V7X_MXU_COLS = 256
V7X_VMEM_LIMIT_BYTES = 60 * 1024 * 1024

TOKEN_BLOCK = 512
INPUT_RING = 3
WEIGHT_RING = 4
DOUBLE_BUFFER = 2
MIX_TOKEN_BLOCK = 512
WEIGHT_TILE = V7X_MXU_COLS
HEADS_PER_SLAB = V7X_MXU_COLS // HEAD_DIM_B
N_SLABS = CHUNK_DIM // V7X_MXU_COLS
MIX_SEGMENTS = 5
FINISH_SLICES = 8
FINISH_FIRST_CHUNK = 1
MIX_FINISH_SLICES = 4

BF16 = jnp.bfloat16
F32 = jnp.float32


def _rms(x, g):
    return x * lax.rsqrt(jnp.mean(x * x, axis=-1, keepdims=True) + EPS) * g


def _dot(a, b):
    return jnp.dot(a.astype(BF16), b, preferred_element_type=F32)


def _sigmoid(x):
    return 0.5 + 0.5 * jnp.tanh(0.5 * x)


def _silu(x):
    half = 0.5 * x
    return half + half * jnp.tanh(half)


def _after(value, anchor):
    bits = lax.bitcast_convert_type(anchor, jnp.int32)
    zero = lax.shift_right_logical(lax.shift_right_logical(bits, 16), 16)
    return value + zero.astype(value.dtype)


def _ffn_loop_kernel(*refs, n_prep, n_prompt, n_blocks, with_ple):
    if with_ple:
        (x_hbm, pp_hbm, ps_hbm, pre_ref, post_ref, wg_ref, wu_ref, wd_ref, wpg_ref, wpp_ref, pleg_ref,
         op_hbm, os_hbm, wgu_s, wd_s, xn_s, a_s, xbuf, obuf, in_sem, out_sem, wst_g, wst_u, wst_d, w_sem,
         wpg_s, wpp_s, hb_s, pbuf, p_sem, wpg_st, wpp_st, e_sem) = refs
    else:
        (xp_hbm, xs_hbm, pre_ref, post_ref, wg_ref, wu_ref, wd_ref, o_hbm,
         wgu_s, wd_s, xn_s, a_s, xbuf, obuf, in_sem, out_sem, wst_g, wst_u, wst_d, w_sem) = refs
    tm = xn_s.shape[0]
    n_x, n_o = xbuf.shape[0], obuf.shape[0]
    n_slices = FINISH_SLICES // 2 if with_ple else FINISH_SLICES
    slice_rows = tm // n_slices
    stages = [("residual", s) for s in range(n_slices)]
    if with_ple:
        stages += [("embed_dots", 0)] + [("embed_add", s) for s in range(n_slices)]
    assert FINISH_FIRST_CHUNK + len(stages) <= n_prep

    def rows_of(b, prompt_hbm, sample_hbm):
        if isinstance(b, int):
            return (prompt_hbm.at[pl.ds(b * tm, tm)] if b < n_prompt else None,
                    sample_hbm.at[pl.ds((b - n_prompt) * tm, tm)] if b >= n_prompt else None)
        return (prompt_hbm.at[pl.ds(pl.multiple_of(b * tm, tm), tm)],
                sample_hbm.at[pl.ds(pl.multiple_of((b - n_prompt) * tm, tm), tm)])

    def start_either(b, make_copy, prompt_hbm, sample_hbm):
        in_prompt, in_sample = rows_of(b, prompt_hbm, sample_hbm)
        if isinstance(b, int):
            make_copy(in_prompt if in_sample is None else in_sample).start()
            return

        @pl.when(b < n_prompt)
        def _():
            make_copy(in_prompt).start()

        @pl.when(b >= n_prompt)
        def _():
            make_copy(in_sample).start()

    def x_copy(b):
        return lambda src: pltpu.make_async_copy(src, xbuf.at[b % n_x], in_sem.at[b % n_x])

    def start_x(b):
        if with_ple:
            row0 = b * tm if isinstance(b, int) else pl.multiple_of(b * tm, tm)
            x_copy(b)(x_hbm.at[pl.ds(row0, tm)]).start()
        else:
            start_either(b, x_copy(b), xp_hbm, xs_hbm)

    def wait_x(b):
        x_copy(b)((x_hbm if with_ple else xp_hbm).at[pl.ds(0, tm)]).wait()

    def p_copy(b):
        return lambda src: pltpu.make_async_copy(src, pbuf.at[b % n_o], p_sem.at[b % n_o])

    def out_copy(b):
        return lambda dst: pltpu.make_async_copy(obuf.at[b % n_o], dst, out_sem.at[b % n_o])

    def start_out(b):
        if with_ple:
            start_either(b, out_copy(b), op_hbm, os_hbm)
        else:
            row0 = b * tm if isinstance(b, int) else pl.multiple_of(b * tm, tm)
            out_copy(b)(o_hbm.at[pl.ds(row0, tm)]).start()

    def wait_out(b):
        out_copy(b)((op_hbm if with_ple else o_hbm).at[pl.ds(0, tm)]).wait()

    def gate_up_chunk(c):
        gu = jnp.dot(xn_s[...], wgu_s[c], preferred_element_type=F32)
        g, u = gu[:, 0:WEIGHT_TILE], gu[:, WEIGHT_TILE:]
        return g, (_silu(g) * u).astype(BF16)

    def down():
        y = jnp.dot(a_s[...], wd_s[...], preferred_element_type=F32)
        return dict(y=y, r=lax.rsqrt(jnp.mean(y * y, axis=-1, keepdims=True) + EPS))

    def ordered(r, rows, anchor):
        return r[rows] if anchor is None else _after(r[rows], anchor[rows, 0:1])

    def finish_stage(b, stage, s, st, anchor=None):
        rows = slice(s * slice_rows, (s + 1) * slice_rows)
        o = b % n_o
        if stage == "residual":
            h = xbuf[b % n_x, rows, :] + 0.5 * (st["y"][rows] * ordered(st["r"], rows, anchor) * post_ref[...])
            obuf[o, rows, :] = h
            if with_ple:
                hb_s[rows, :] = h.astype(BF16)
        elif stage == "embed_dots":
            gate = _sigmoid(jnp.dot(hb_s[...], wpg_s[...], preferred_element_type=F32))
            st["e"] = gate * _dot(pbuf[b % n_o], wpp_s[...])
            st["r2"] = lax.rsqrt(jnp.mean(st["e"] * st["e"], axis=-1, keepdims=True) + EPS)
        else:
            obuf[o, rows, :] = obuf[o, rows, :] + st["e"][rows] * ordered(st["r2"], rows, anchor) * pleg_ref[...]

    def weight_copies(c):
        slot, cols = c % wst_g.shape[0], slice(c * WEIGHT_TILE, (c + 1) * WEIGHT_TILE)
        return [pltpu.make_async_copy(wg_ref.at[:, cols], wst_g.at[slot], w_sem.at[0, slot]),
                pltpu.make_async_copy(wu_ref.at[:, cols], wst_u.at[slot], w_sem.at[1, slot]),
                pltpu.make_async_copy(wd_ref.at[cols, :], wst_d.at[slot], w_sem.at[2, slot])]

    def embed_weight_copies():
        return [pltpu.make_async_copy(wpg_ref, wpg_st, e_sem.at[0]), pltpu.make_async_copy(wpp_ref, wpp_st, e_sem.at[1])]

    n_w = wst_g.shape[0]
    start_x(0)
    for c in range(min(n_w, n_prep)):
        for copy in weight_copies(c):
            copy.start()
    if with_ple:
        start_either(0, p_copy(0), pp_hbm, ps_hbm)
        for copy in embed_weight_copies():
            copy.start()
    wait_x(0)
    if n_blocks > 1:
        start_x(1)
    xn_s[...] = _rms(xbuf[0], pre_ref[...]).astype(BF16)
    for c in range(n_prep):
        cols = slice(c * WEIGHT_TILE, (c + 1) * WEIGHT_TILE)
        for copy in weight_copies(c):
            copy.wait()
        wgu_s[c, :, 0:WEIGHT_TILE] = wst_g[c % n_w].astype(BF16)
        wgu_s[c, :, WEIGHT_TILE:] = wst_u[c % n_w].astype(BF16)
        wd_s[cols, :] = wst_d[c % n_w].astype(BF16)
        if c + n_w < n_prep:
            for copy in weight_copies(c + n_w):
                copy.start()
        _, a_s[:, cols] = gate_up_chunk(c)
    if with_ple:
        for copy in embed_weight_copies():
            copy.wait()
        wpg_s[...] = wpg_st[...].astype(BF16)
        wpp_s[...] = wpp_st[...].astype(BF16)

    def run_blocks():
        def step(b, carry):
            wait_x(b + 1)
            if with_ple:
                p_copy(b)(pp_hbm.at[pl.ds(0, tm)]).wait()
                start_either(b + 1, p_copy(b + 1), pp_hbm, ps_hbm)

            @pl.when(b + 2 < n_blocks)
            def _():
                start_x(b + 2)

            @pl.when(b >= n_o)
            def _():
                wait_out(b - n_o)

            st = down()
            x = _after(xbuf[(b + 1) % n_x], st["y"][:, 0:1])
            xn_s[...] = _rms(x, pre_ref[...]).astype(BF16)
            for c in range(n_prep):
                g, a_s[:, c * WEIGHT_TILE:(c + 1) * WEIGHT_TILE] = gate_up_chunk(c)
                k = c - FINISH_FIRST_CHUNK
                if 0 <= k < len(stages):
                    finish_stage(b, *stages[k], st, anchor=g)
            start_out(b)
            return carry

        lax.fori_loop(0, n_blocks - 1, step, 0)
        last = n_blocks - 1
        if with_ple:
            p_copy(last)(pp_hbm.at[pl.ds(0, tm)]).wait()
        if last >= n_o:
            wait_out(last - n_o)
        st = down()
        for stage, s in stages:
            finish_stage(last, stage, s, st)
        start_out(last)
        for b in range(max(last - n_o + 1, 0), last + 1):
            wait_out(b)

    run_blocks()


def _ffn_loop(x, pre_g, post_g, wg, wu, wd, ple=None, *, n_prompt, n_sample, name):
    tm = TOKEN_BLOCK
    d, d_ff = wg.shape
    n_prep = d_ff // WEIGHT_TILE
    n_blocks = n_prompt + n_sample
    assert n_prompt >= 1
    anywhere = pl.BlockSpec(memory_space=pl.ANY)
    weights = [_resident(pre_g), _resident(post_g), anywhere, anywhere, anywhere]
    scratch = [pltpu.VMEM((n_prep, d, 2 * WEIGHT_TILE), BF16), pltpu.VMEM((d_ff, d), BF16),
               pltpu.VMEM((tm, d), BF16), pltpu.VMEM((tm, d_ff), BF16),
               pltpu.VMEM((INPUT_RING, tm, d), F32), pltpu.VMEM((DOUBLE_BUFFER, tm, d), F32),
               pltpu.SemaphoreType.DMA((INPUT_RING,)), pltpu.SemaphoreType.DMA((DOUBLE_BUFFER,)),
               pltpu.VMEM((WEIGHT_RING, d, WEIGHT_TILE), F32), pltpu.VMEM((WEIGHT_RING, d, WEIGHT_TILE), F32),
               pltpu.VMEM((WEIGHT_RING, WEIGHT_TILE, d), F32), pltpu.SemaphoreType.DMA((3, WEIGHT_RING))]
    if ple is None:
        in_specs = [anywhere, anywhere] + weights
        args = [x[0], x[1], pre_g, post_g, wg, wu, wd]
        out_specs = anywhere
        out_shape = jax.ShapeDtypeStruct((n_blocks * tm, d), F32)
    else:
        pp, ps, wpg, wpp, ple_g = ple
        in_specs = [anywhere, anywhere, anywhere] + weights + [anywhere, anywhere, _resident(ple_g)]
        args = [x, pp, ps, pre_g, post_g, wg, wu, wd, wpg, wpp, ple_g]
        out_specs = [anywhere, anywhere]
        out_shape = [jax.ShapeDtypeStruct((n_prompt * tm, d), F32), jax.ShapeDtypeStruct((n_sample * tm, d), F32)]
        scratch += [pltpu.VMEM(wpg.shape, BF16), pltpu.VMEM(wpp.shape, BF16), pltpu.VMEM((tm, d), BF16),
                    pltpu.VMEM((DOUBLE_BUFFER, tm, pp.shape[1]), F32), pltpu.SemaphoreType.DMA((DOUBLE_BUFFER,)),
                    pltpu.VMEM(wpg.shape, F32), pltpu.VMEM(wpp.shape, F32), pltpu.SemaphoreType.DMA((2,))]
    return pl.pallas_call(
        functools.partial(_ffn_loop_kernel, n_prep=n_prep, n_prompt=n_prompt, n_blocks=n_blocks,
                          with_ple=ple is not None),
        grid=(1,),
        in_specs=in_specs,
        out_specs=out_specs,
        out_shape=out_shape,
        scratch_shapes=scratch,
        compiler_params=_params(),
        name=name,
    )(*args)


def _head_rms(v, ones_slab, g):
    vv = (v * v).astype(BF16)
    ms = jnp.concatenate(
        [jnp.dot(vv[:, k * V7X_MXU_COLS:(k + 1) * V7X_MXU_COLS], ones_slab, preferred_element_type=F32)
         for k in range(CHUNK_DIM // V7X_MXU_COLS)], axis=1)
    return v * lax.rsqrt(ms + EPS) * g


def _mix_kernel(hc_ref, hv_ref, state_ref, pre_ref, post_ref, win_ref, wout_ref, convw_ref, vg_ref,
                ws_ref, b_ref, ga_ref, gb_ref,
                o_ref, convp_ref, vp_ref, convs_ref, vs_ref,
                win_s, wout_s, n_s, yab_s, carry_ref, vn_s, vt_s, ones_ref, wslab_s, bias_s,
                win_st, wout_st, w_sem,
                *, n_prep, n_prompt, n_blocks, blocks_per_seq, seq_len):
    tm = hc_ref.shape[0]
    i = pl.program_id(0)
    j = i - n_prep

    @pl.when(i == 0)
    def _():
        width = win_s.shape[2]
        pieces = [pltpu.make_async_copy(win_ref.at[:, k * width:(k + 1) * width],
                                        win_st.at[:, k * width:(k + 1) * width], w_sem.at[k])
                  for k in range(MIX_SEGMENTS)]
        pieces.append(pltpu.make_async_copy(wout_ref, wout_st, w_sem.at[MIX_SEGMENTS]))
        for copy in pieces:
            copy.start()
        for k in range(MIX_SEGMENTS):
            pieces[k].wait()
            win_s[k] = win_st[:, k * width:(k + 1) * width].astype(BF16)
        pieces[MIX_SEGMENTS].wait()
        wout_s[...] = wout_st[...].astype(BF16)

    @pl.when(i == 0)
    def _():
        yab_s[...] = jnp.zeros_like(yab_s)
        carry_ref[...] = jnp.zeros_like(carry_ref)
        row_head = lax.broadcasted_iota(jnp.int32, ones_ref.shape, 0) // HEAD_DIM_B
        col_head = lax.broadcasted_iota(jnp.int32, ones_ref.shape, 1) // HEAD_DIM_B
        ones_ref[...] = jnp.where(row_head == col_head, 1.0 / HEAD_DIM_B, 0.0).astype(BF16)

    assert MIX_FINISH_SLICES <= MIX_SEGMENTS
    slice_rows = tm // MIX_FINISH_SLICES

    def out_proj_prev():
        y = jnp.dot(yab_s[...], wout_s[...], preferred_element_type=F32)
        return dict(y=y, r=lax.rsqrt(jnp.mean(y * y, axis=-1, keepdims=True) + EPS), done=0)

    def finish_slice(prev, anchor=None):
        rows = slice(prev["done"] * slice_rows, (prev["done"] + 1) * slice_rows)
        prev["done"] += 1
        r = prev["r"][rows] if anchor is None else _after(prev["r"][rows], anchor[rows, 0:1])
        o_ref[rows, :] = hv_ref[rows, :] + prev["y"][rows] * r * post_ref[...]

    def seg(k, prev):
        out = jnp.dot(n_s[...], win_s[k], preferred_element_type=F32)
        if prev["done"] < MIX_FINISH_SLICES:
            finish_slice(prev, anchor=out)
        return out

    def conv(z, z1, z2):
        return convw_ref[0:1, :] * z2 + convw_ref[1:2, :] * z1 + convw_ref[2:3, :] * z

    def per_head(piece, rows):
        return jnp.concatenate([jnp.broadcast_to(piece(h), (rows, HEAD_DIM_B)) for h in range(HEADS_B)], axis=1)

    def v_gain():
        return jnp.concatenate([vg_ref[h:h + 1, :] for h in range(HEADS_B)], axis=1)

    def chunk_bias(rows):
        b_t = b_ref[...].T
        return per_head(lambda h: b_t[0:rows, h:h + 1], rows)

    @pl.when(i == 0)
    def _():
        t_idx = lax.broadcasted_iota(jnp.int32, (CHUNK, HEADS_PER_SLAB * CHUNK), 0)
        s_idx = lax.broadcasted_iota(jnp.int32, (CHUNK, HEADS_PER_SLAB * CHUNK), 1) % CHUNK
        for k in range(N_SLABS):
            w_k = jnp.concatenate([ws_ref[k * HEADS_PER_SLAB + hd] for hd in range(HEADS_PER_SLAB)], axis=1)
            wslab_s[k] = jnp.where(t_idx >= s_idx, w_k, 0.0).astype(BF16)
        bias_s[...] = chunk_bias(CHUNK)

    def group_norms(conv_out, mixed, prev):
        yab_s[:, 0:CONV_DIM] = _rms(seg(0, prev) * conv_out, ga_ref[...]).astype(BF16)
        yab_s[:, CONV_DIM:] = _rms(seg(3, prev) * mixed, gb_ref[...]).astype(BF16)

    def prompt_block():
        prev = out_proj_prev()
        n_s[...] = _rms(hc_ref[...], pre_ref[...]).astype(BF16)

        v_n = _head_rms(seg(4, prev), ones_ref[...], v_gain())
        vp_ref[0] = v_n[tm - CHUNK:, :].T
        lane_head = lax.broadcasted_iota(jnp.int32, (CHUNK, V7X_MXU_COLS), 1) // HEAD_DIM_B
        chunks = []
        for c in range(tm // CHUNK):
            slabs = []
            for k in range(N_SLABS):
                cols = slice(k * V7X_MXU_COLS, (k + 1) * V7X_MXU_COLS)
                v_ck = v_n[c * CHUNK:(c + 1) * CHUNK, cols].astype(BF16)
                bd = jnp.concatenate(
                    [jnp.where(lane_head == hd, v_ck, jnp.zeros((), BF16)) for hd in range(HEADS_PER_SLAB)], axis=0)
                slabs.append(jnp.dot(wslab_s[k], bd, preferred_element_type=F32))
            chunks.append(jnp.concatenate(slabs, axis=1) + bias_s[...])
        mixed = jnp.concatenate(chunks, axis=0)

        z = seg(1, prev) * seg(2, prev)
        row = lax.broadcasted_iota(jnp.int32, z.shape, 0)
        carry = jnp.where(j % blocks_per_seq == 0, 0.0, carry_ref[...])
        prev1 = carry[1:2, :]
        prev2 = carry[0:1, :]
        z1 = jnp.where(row == 0, prev1, pltpu.roll(z, 1, axis=0))
        z2 = jnp.where(row == 0, prev2, jnp.where(row == 1, prev1, pltpu.roll(z, 2, axis=0)))
        tail = z[tm - (CONV_K - 1):, :]
        carry_ref[...] = tail
        convp_ref[0] = tail
        group_norms(conv(z, z1, z2), mixed, prev)

    def sample_block():
        prev = out_proj_prev()
        n_s[...] = _rms(hc_ref[...], pre_ref[...]).astype(BF16)

        n_seq = tm // seq_len
        per_seq = lambda a: a.reshape(n_seq, seq_len, a.shape[-1])
        every_row = lambda a: jnp.broadcast_to(a, (n_seq, seq_len, a.shape[-1]))
        v_n = _head_rms(seg(4, prev), ones_ref[...], v_gain())
        first_seq = pl.multiple_of((j - n_prompt) * n_seq, n_seq)
        for c in range(CHUNK_DIM // V7X_LANES):
            lanes = slice(c * V7X_LANES, (c + 1) * V7X_LANES)
            vn_s[c] = v_n[:, lanes]
            for t in range(seq_len):
                vt_s[t, pl.ds(first_seq, n_seq), lanes] = vn_s[c, pl.ds(t, n_seq, stride=seq_len), :]

        @pl.when(j == n_blocks - 1)
        def _():
            for t in range(seq_len):
                vs_ref[t] = vt_s[t].T
        v_seq = per_seq(v_n)
        t_row = lax.broadcasted_iota(jnp.int32, (seq_len, CHUNK_DIM), 0)
        mixed = every_row(chunk_bias(seq_len)[None])
        for s in range(seq_len):
            w_from_s = per_head(lambda h: ws_ref[h, 0:seq_len, s:s + 1], seq_len)
            mixed = mixed + jnp.where(t_row >= s, w_from_s, 0.0)[None] * every_row(v_seq[:, s:s + 1, :])
        mixed = mixed.reshape(tm, CHUNK_DIM)

        z = seg(1, prev) * seg(2, prev)
        t = lax.broadcasted_iota(jnp.int32, z.shape, 0) % seq_len
        st = state_ref[...]
        prev1 = every_row(st[:, 1:2, :]).reshape(tm, CONV_DIM)
        prev2 = every_row(st[:, 0:1, :]).reshape(tm, CONV_DIM)
        z1 = jnp.where(t == 0, prev1, pltpu.roll(z, 1, axis=0))
        z2 = jnp.where(t == 0, prev2, jnp.where(t == 1, prev1, pltpu.roll(z, 2, axis=0)))
        convs_ref[...] = per_seq(z)[:, seq_len - (CONV_K - 1):, :]
        group_norms(conv(z, z1, z2), mixed, prev)

    @pl.when(jnp.logical_and(j >= 0, j < n_prompt))
    def _():
        prompt_block()

    @pl.when(jnp.logical_and(j >= n_prompt, j < n_blocks))
    def _():
        sample_block()

    @pl.when(j == n_blocks)
    def _():
        prev = out_proj_prev()
        for _ in range(MIX_FINISH_SLICES):
            finish_slice(prev)


def _resident(a):
    return pl.BlockSpec(a.shape, lambda i: (0,) * a.ndim, pipeline_mode=pl.Buffered(1))


def _clamped(lo, n):
    return lambda i: jnp.minimum(jnp.maximum(i - lo, 0), n - 1)


def _stream(tm, d, lo, n):
    blk = _clamped(lo, n)
    return pl.BlockSpec((tm, d), lambda i: (blk(i), 0))


def _params():
    return pltpu.CompilerParams(dimension_semantics=("arbitrary",), vmem_limit_bytes=V7X_VMEM_LIMIT_BYTES)


def _mix(h, state, batch_p, seq_p, batch_s, seq_s, mw):
    tm = MIX_TOKEN_BLOCK
    d = h.shape[1]
    n_prep = 1
    assert seq_p % tm == 0 and (batch_s * seq_s) % tm == 0 and tm % seq_s == 0
    n_prompt, n_sample = batch_p * seq_p // tm, batch_s * seq_s // tm
    n_blocks = n_prompt + n_sample
    blocks_per_seq = seq_p // tm
    seqs_per_block = tm // seq_s
    assert mw["w_in"].shape[1] == MIX_SEGMENTS * CONV_DIM
    anywhere = pl.BlockSpec(memory_space=pl.ANY)
    small = [mw["v_g"], mw["w_s"], mw["b_s"], mw["g_a"], mw["g_b"]]
    conv_w_spec = pl.BlockSpec((None,) + mw["conv_w"].shape[1:], lambda i: (0, 0, 0), pipeline_mode=pl.Buffered(1))
    lo_s = n_prep + n_prompt
    prompt_seq = _clamped(0, batch_p)
    per_prompt_seq = lambda i: (prompt_seq((i - n_prep) // blocks_per_seq), 0, 0)
    sample_blk = _clamped(lo_s, n_sample)
    per_sample_blk = pl.BlockSpec((seqs_per_block, CONV_K - 1, CONV_DIM), lambda i: (sample_blk(i), 0, 0))
    return pl.pallas_call(
        functools.partial(_mix_kernel, n_prep=n_prep, n_prompt=n_prompt, n_blocks=n_blocks,
                          blocks_per_seq=blocks_per_seq, seq_len=seq_s),
        grid=(n_prep + n_blocks + 1,),
        in_specs=[_stream(tm, d, n_prep, n_blocks), _stream(tm, d, n_prep + 1, n_blocks),
                  per_sample_blk, _resident(mw["pre"]), _resident(mw["post"]),
                  anywhere, anywhere, conv_w_spec]
                 + [_resident(a) for a in small],
        out_specs=[_stream(tm, d, n_prep + 1, n_blocks),
                   pl.BlockSpec((1, CONV_K - 1, CONV_DIM), per_prompt_seq),
                   pl.BlockSpec((1, CHUNK_DIM, CHUNK), per_prompt_seq),
                   per_sample_blk,
                   pl.BlockSpec((seq_s, CHUNK_DIM, batch_s), lambda i: (0, 0, 0))],
        out_shape=[jax.ShapeDtypeStruct(h.shape, F32),
                   jax.ShapeDtypeStruct((batch_p, CONV_K - 1, CONV_DIM), F32),
                   jax.ShapeDtypeStruct((batch_p, CHUNK_DIM, CHUNK), F32),
                   jax.ShapeDtypeStruct((batch_s, CONV_K - 1, CONV_DIM), F32),
                   jax.ShapeDtypeStruct((seq_s, CHUNK_DIM, batch_s), F32)],
        scratch_shapes=[pltpu.VMEM((MIX_SEGMENTS, d, CONV_DIM), BF16), pltpu.VMEM(mw["w_out"].shape, BF16),
                        pltpu.VMEM((tm, d), BF16), pltpu.VMEM((tm, CONV_DIM + CHUNK_DIM), BF16),
                        pltpu.VMEM((CONV_K - 1, CONV_DIM), F32),
                        pltpu.VMEM((CHUNK_DIM // V7X_LANES, tm, V7X_LANES), F32),
                        pltpu.VMEM((seq_s, batch_s, CHUNK_DIM), F32),
                        pltpu.VMEM((V7X_MXU_COLS, V7X_MXU_COLS), BF16),
                        pltpu.VMEM((N_SLABS, CHUNK, HEADS_PER_SLAB * CHUNK), BF16),
                        pltpu.VMEM((CHUNK, CHUNK_DIM), F32),
                        pltpu.VMEM(mw["w_in"].shape, F32), pltpu.VMEM(mw["w_out"].shape, F32),
                        pltpu.SemaphoreType.DMA((MIX_SEGMENTS + 1,))],
        compiler_params=_params(),
        name="mix",
    )(h, h, state, mw["pre"], mw["post"], mw["w_in"], mw["w_out"], mw["conv_w"], *small)


def _row(g):
    return g.reshape(1, -1).astype(F32)


def _mix_weights(mix_pre_g, mix_post_g, w_in, conv_w, v_norm_g, w_s, b_s, out_g_a, out_g_b, w_out):
    return dict(pre=_row(mix_pre_g), post=_row(mix_post_g), w_in=w_in, w_out=w_out,
                conv_w=conv_w.astype(F32), v_g=v_norm_g.astype(F32), w_s=w_s.astype(F32),
                b_s=b_s.astype(F32), g_a=_row(out_g_a), g_b=_row(out_g_b))


def kernel(x_prompt, x_sample, p_prompt, p_sample, state_conv, ffn1_pre_g, ffn1_post_g, ffn1_w_gate, ffn1_w_up, ffn1_w_down, mix_pre_g, mix_post_g, w_in, conv_w, v_norm_g, w_s, b_s, out_g_a, out_g_b, w_out, ffn2_pre_g, ffn2_post_g, ffn2_w_gate, ffn2_w_up, ffn2_w_down, ple_w_gate, ple_w_proj, ple_post_g):
    depth = ffn1_pre_g.shape[0]
    bp, lp, d = x_prompt.shape
    bs, ls, _ = x_sample.shape
    tm = TOKEN_BLOCK
    assert lp % tm == 0 and (bs * ls) % tm == 0 and tm % ls == 0
    assert CONV_K - 1 <= ls <= V7X_SUBLANES and (lp - 1) % CHUNK + 1 == CHUNK
    n_prompt, n_sample = bp * lp // tm, bs * ls // tm
    x = (x_prompt.reshape(bp * lp, d), x_sample.reshape(bs * ls, d))
    conv_p, conv_s, v_p, v_s = [], [], [], []
    for i in range(depth):
        mw = _mix_weights(mix_pre_g[i], mix_post_g[i], w_in[i], conv_w[i:i + 1], v_norm_g[i], w_s[i], b_s[i],
                          out_g_a[i], out_g_b[i], w_out[i])
        ple = (p_prompt[i].reshape(bp * lp, -1), p_sample[i].reshape(bs * ls, -1),
               ple_w_gate[i], ple_w_proj[i], _row(ple_post_g[i]))

        blocks = dict(n_prompt=n_prompt, n_sample=n_sample)
        h = _ffn_loop(x, _row(ffn1_pre_g[i]), _row(ffn1_post_g[i]),
                      ffn1_w_gate[i], ffn1_w_up[i], ffn1_w_down[i], name="ffn1", **blocks)
        h, cp, vp, cs, vs = _mix(h, state_conv[i].astype(F32), bp, lp, bs, ls, mw)
        x = _ffn_loop(h, _row(ffn2_pre_g[i]), _row(ffn2_post_g[i]),
                      ffn2_w_gate[i], ffn2_w_up[i], ffn2_w_down[i], ple, name="ffn2_ple", **blocks)

        conv_p.append(cp)
        conv_s.append(cs)
        v_p.append(vp.transpose(0, 2, 1).reshape(bp, CHUNK, HEADS_B, HEAD_DIM_B))
        v_s.append(vs.transpose(2, 0, 1).reshape(bs, ls, HEADS_B, HEAD_DIM_B))
    layers = lambda parts: parts[0][None] if depth == 1 else jnp.stack(parts)
    return (x[0].reshape(bp, lp, d), x[1].reshape(bs, ls, d),
            layers(conv_p), layers(conv_s), layers(v_p), layers(v_s))
```

```python
import functools

import jax
import jax.numpy as jnp
from jax import lax
from jax.experimental import pallas as pl
from jax.experimental.pallas import tpu as pltpu

CONV_K = 3
HEADS_B = 8
HEAD_DIM_B = 64
CHUNK_DIM = HEADS_B * HEAD_DIM_B
CONV_DIM = CHUNK_DIM
CHUNK = 128
EPS = 1e-6

V7X_SUBLANES = 8
V7X_LANES = 128
V7X_MXU_COLS = 256
V7X_VMEM_LIMIT_BYTES = 60 * 1024 * 1024

TOKEN_BLOCK = 512
INPUT_RING = 3
WEIGHT_RING = 4
DOUBLE_BUFFER = 2
MIX_TOKEN_BLOCK = 512
WEIGHT_TILE = V7X_MXU_COLS
HEADS_PER_SLAB = V7X_MXU_COLS // HEAD_DIM_B
N_SLABS = CHUNK_DIM // V7X_MXU_COLS
MIX_SEGMENTS = 5
FINISH_SLICES = 8
FINISH_FIRST_CHUNK = 1
MIX_FINISH_SLICES = 4

BF16 = jnp.bfloat16
F32 = jnp.float32


def _rms(x, g):
    return x * lax.rsqrt(jnp.mean(x * x, axis=-1, keepdims=True) + EPS) * g


def _dot(a, b):
    return jnp.dot(a.astype(BF16), b, preferred_element_type=F32)


def _sigmoid(x):
    return 0.5 + 0.5 * jnp.tanh(0.5 * x)


def _silu(x):
    half = 0.5 * x
    return half + half * jnp.tanh(half)


def _after(value, anchor):
    bits = lax.bitcast_convert_type(anchor, jnp.int32)
    zero = lax.shift_right_logical(lax.shift_right_logical(bits, 16), 16)
    return value + zero.astype(value.dtype)


def _ffn_loop_kernel(*refs, n_prep, n_prompt, n_blocks, with_ple):
    if with_ple:
        (x_hbm, pp_hbm, ps_hbm, pre_ref, post_ref, wg_ref, wu_ref, wd_ref, wpg_ref, wpp_ref, pleg_ref,
         op_hbm, os_hbm, wgu_s, wd_s, xn_s, a_s, xbuf, obuf, in_sem, out_sem, wst_g, wst_u, wst_d, w_sem,
         wpg_s, wpp_s, hb_s, pbuf, p_sem, wpg_st, wpp_st, e_sem) = refs
    else:
        (xp_hbm, xs_hbm, pre_ref, post_ref, wg_ref, wu_ref, wd_ref, o_hbm,
         wgu_s, wd_s, xn_s, a_s, xbuf, obuf, in_sem, out_sem, wst_g, wst_u, wst_d, w_sem) = refs
    tm = xn_s.shape[0]
    n_x, n_o = xbuf.shape[0], obuf.shape[0]
    n_slices = FINISH_SLICES // 2 if with_ple else FINISH_SLICES
    slice_rows = tm // n_slices
    stages = [("residual", s) for s in range(n_slices)]
    if with_ple:
        stages += [("embed_dots", 0)] + [("embed_add", s) for s in range(n_slices)]
    assert FINISH_FIRST_CHUNK + len(stages) <= n_prep

    def rows_of(b, prompt_hbm, sample_hbm):
        if isinstance(b, int):
            return (prompt_hbm.at[pl.ds(b * tm, tm)] if b < n_prompt else None,
                    sample_hbm.at[pl.ds((b - n_prompt) * tm, tm)] if b >= n_prompt else None)
        return (prompt_hbm.at[pl.ds(pl.multiple_of(b * tm, tm), tm)],
                sample_hbm.at[pl.ds(pl.multiple_of((b - n_prompt) * tm, tm), tm)])

    def start_either(b, make_copy, prompt_hbm, sample_hbm):
        in_prompt, in_sample = rows_of(b, prompt_hbm, sample_hbm)
        if isinstance(b, int):
            make_copy(in_prompt if in_sample is None else in_sample).start()
            return

        @pl.when(b < n_prompt)
        def _():
            make_copy(in_prompt).start()

        @pl.when(b >= n_prompt)
        def _():
            make_copy(in_sample).start()

    def x_copy(b):
        return lambda src: pltpu.make_async_copy(src, xbuf.at[b % n_x], in_sem.at[b % n_x])

    def start_x(b):
        if with_ple:
            row0 = b * tm if isinstance(b, int) else pl.multiple_of(b * tm, tm)
            x_copy(b)(x_hbm.at[pl.ds(row0, tm)]).start()
        else:
            start_either(b, x_copy(b), xp_hbm, xs_hbm)

    def wait_x(b):
        x_copy(b)((x_hbm if with_ple else xp_hbm).at[pl.ds(0, tm)]).wait()

    def p_copy(b):
        return lambda src: pltpu.make_async_copy(src, pbuf.at[b % n_o], p_sem.at[b % n_o])

    def out_copy(b):
        return lambda dst: pltpu.make_async_copy(obuf.at[b % n_o], dst, out_sem.at[b % n_o])

    def start_out(b):
        if with_ple:
            start_either(b, out_copy(b), op_hbm, os_hbm)
        else:
            row0 = b * tm if isinstance(b, int) else pl.multiple_of(b * tm, tm)
            out_copy(b)(o_hbm.at[pl.ds(row0, tm)]).start()

    def wait_out(b):
        out_copy(b)((op_hbm if with_ple else o_hbm).at[pl.ds(0, tm)]).wait()

    def gate_up_chunk(c):
        gu = jnp.dot(xn_s[...], wgu_s[c], preferred_element_type=F32)
        g, u = gu[:, 0:WEIGHT_TILE], gu[:, WEIGHT_TILE:]
        return g, (_silu(g) * u).astype(BF16)

    def down():
        y = jnp.dot(a_s[...], wd_s[...], preferred_element_type=F32)
        return dict(y=y, r=lax.rsqrt(jnp.mean(y * y, axis=-1, keepdims=True) + EPS))

    def ordered(r, rows, anchor):
        return r[rows] if anchor is None else _after(r[rows], anchor[rows, 0:1])

    def finish_stage(b, stage, s, st, anchor=None):
        rows = slice(s * slice_rows, (s + 1) * slice_rows)
        o = b % n_o
        if stage == "residual":
            h = xbuf[b % n_x, rows, :] + 0.5 * (st["y"][rows] * ordered(st["r"], rows, anchor) * post_ref[...])
            obuf[o, rows, :] = h
            if with_ple:
                hb_s[rows, :] = h.astype(BF16)
        elif stage == "embed_dots":
            gate = _sigmoid(jnp.dot(hb_s[...], wpg_s[...], preferred_element_type=F32))
            st["e"] = gate * _dot(pbuf[b % n_o], wpp_s[...])
            st["r2"] = lax.rsqrt(jnp.mean(st["e"] * st["e"], axis=-1, keepdims=True) + EPS)
        else:
            obuf[o, rows, :] = obuf[o, rows, :] + st["e"][rows] * ordered(st["r2"], rows, anchor) * pleg_ref[...]

    def weight_copies(c):
        slot, cols = c % wst_g.shape[0], slice(c * WEIGHT_TILE, (c + 1) * WEIGHT_TILE)
        return [pltpu.make_async_copy(wg_ref.at[:, cols], wst_g.at[slot], w_sem.at[0, slot]),
                pltpu.make_async_copy(wu_ref.at[:, cols], wst_u.at[slot], w_sem.at[1, slot]),
                pltpu.make_async_copy(wd_ref.at[cols, :], wst_d.at[slot], w_sem.at[2, slot])]

    def embed_weight_copies():
        return [pltpu.make_async_copy(wpg_ref, wpg_st, e_sem.at[0]), pltpu.make_async_copy(wpp_ref, wpp_st, e_sem.at[1])]

    n_w = wst_g.shape[0]
    start_x(0)
    for c in range(min(n_w, n_prep)):
        for copy in weight_copies(c):
            copy.start()
    if with_ple:
        start_either(0, p_copy(0), pp_hbm, ps_hbm)
        for copy in embed_weight_copies():
            copy.start()
    wait_x(0)
    if n_blocks > 1:
        start_x(1)
    xn_s[...] = _rms(xbuf[0], pre_ref[...]).astype(BF16)
    for c in range(n_prep):
        cols = slice(c * WEIGHT_TILE, (c + 1) * WEIGHT_TILE)
        for copy in weight_copies(c):
            copy.wait()
        wgu_s[c, :, 0:WEIGHT_TILE] = wst_g[c % n_w].astype(BF16)
        wgu_s[c, :, WEIGHT_TILE:] = wst_u[c % n_w].astype(BF16)
        wd_s[cols, :] = wst_d[c % n_w].astype(BF16)
        if c + n_w < n_prep:
            for copy in weight_copies(c + n_w):
                copy.start()
        _, a_s[:, cols] = gate_up_chunk(c)
    if with_ple:
        for copy in embed_weight_copies():
            copy.wait()
        wpg_s[...] = wpg_st[...].astype(BF16)
        wpp_s[...] = wpp_st[...].astype(BF16)

    def run_blocks():
        def step(b, carry):
            wait_x(b + 1)
            if with_ple:
                p_copy(b)(pp_hbm.at[pl.ds(0, tm)]).wait()
                start_either(b + 1, p_copy(b + 1), pp_hbm, ps_hbm)

            @pl.when(b + 2 < n_blocks)
            def _():
                start_x(b + 2)

            @pl.when(b >= n_o)
            def _():
                wait_out(b - n_o)

            st = down()
            x = _after(xbuf[(b + 1) % n_x], st["y"][:, 0:1])
            xn_s[...] = _rms(x, pre_ref[...]).astype(BF16)
            for c in range(n_prep):
                g, a_s[:, c * WEIGHT_TILE:(c + 1) * WEIGHT_TILE] = gate_up_chunk(c)
                k = c - FINISH_FIRST_CHUNK
                if 0 <= k < len(stages):
                    finish_stage(b, *stages[k], st, anchor=g)
            start_out(b)
            return carry

        lax.fori_loop(0, n_blocks - 1, step, 0)
        last = n_blocks - 1
        if with_ple:
            p_copy(last)(pp_hbm.at[pl.ds(0, tm)]).wait()
        if last >= n_o:
            wait_out(last - n_o)
        st = down()
        for stage, s in stages:
            finish_stage(last, stage, s, st)
        start_out(last)
        for b in range(max(last - n_o + 1, 0), last + 1):
            wait_out(b)

    run_blocks()


def _ffn_loop(x, pre_g, post_g, wg, wu, wd, ple=None, *, n_prompt, n_sample, name):
    tm = TOKEN_BLOCK
    d, d_ff = wg.shape
    n_prep = d_ff // WEIGHT_TILE
    n_blocks = n_prompt + n_sample
    assert n_prompt >= 1
    anywhere = pl.BlockSpec(memory_space=pl.ANY)
    weights = [_resident(pre_g), _resident(post_g), anywhere, anywhere, anywhere]
    scratch = [pltpu.VMEM((n_prep, d, 2 * WEIGHT_TILE), BF16), pltpu.VMEM((d_ff, d), BF16),
               pltpu.VMEM((tm, d), BF16), pltpu.VMEM((tm, d_ff), BF16),
               pltpu.VMEM((INPUT_RING, tm, d), F32), pltpu.VMEM((DOUBLE_BUFFER, tm, d), F32),
               pltpu.SemaphoreType.DMA((INPUT_RING,)), pltpu.SemaphoreType.DMA((DOUBLE_BUFFER,)),
               pltpu.VMEM((WEIGHT_RING, d, WEIGHT_TILE), F32), pltpu.VMEM((WEIGHT_RING, d, WEIGHT_TILE), F32),
               pltpu.VMEM((WEIGHT_RING, WEIGHT_TILE, d), F32), pltpu.SemaphoreType.DMA((3, WEIGHT_RING))]
    if ple is None:
        in_specs = [anywhere, anywhere] + weights
        args = [x[0], x[1], pre_g, post_g, wg, wu, wd]
        out_specs = anywhere
        out_shape = jax.ShapeDtypeStruct((n_blocks * tm, d), F32)
    else:
        pp, ps, wpg, wpp, ple_g = ple
        in_specs = [anywhere, anywhere, anywhere] + weights + [anywhere, anywhere, _resident(ple_g)]
        args = [x, pp, ps, pre_g, post_g, wg, wu, wd, wpg, wpp, ple_g]
        out_specs = [anywhere, anywhere]
        out_shape = [jax.ShapeDtypeStruct((n_prompt * tm, d), F32), jax.ShapeDtypeStruct((n_sample * tm, d), F32)]
        scratch += [pltpu.VMEM(wpg.shape, BF16), pltpu.VMEM(wpp.shape, BF16), pltpu.VMEM((tm, d), BF16),
                    pltpu.VMEM((DOUBLE_BUFFER, tm, pp.shape[1]), F32), pltpu.SemaphoreType.DMA((DOUBLE_BUFFER,)),
                    pltpu.VMEM(wpg.shape, F32), pltpu.VMEM(wpp.shape, F32), pltpu.SemaphoreType.DMA((2,))]
    return pl.pallas_call(
        functools.partial(_ffn_loop_kernel, n_prep=n_prep, n_prompt=n_prompt, n_blocks=n_blocks,
                          with_ple=ple is not None),
        grid=(1,),
        in_specs=in_specs,
        out_specs=out_specs,
        out_shape=out_shape,
        scratch_shapes=scratch,
        compiler_params=_params(),
        name=name,
    )(*args)


def _head_rms(v, ones_slab, g):
    vv = (v * v).astype(BF16)
    ms = jnp.concatenate(
        [jnp.dot(vv[:, k * V7X_MXU_COLS:(k + 1) * V7X_MXU_COLS], ones_slab, preferred_element_type=F32)
         for k in range(CHUNK_DIM // V7X_MXU_COLS)], axis=1)
    return v * lax.rsqrt(ms + EPS) * g


def _mix_kernel(hc_ref, hv_ref, state_ref, pre_ref, post_ref, win_ref, wout_ref, convw_ref, vg_ref,
                ws_ref, b_ref, ga_ref, gb_ref,
                o_ref, convp_ref, vp_ref, convs_ref, vs_ref,
                win_s, wout_s, n_s, yab_s, carry_ref, vn_s, vt_s, ones_ref, wslab_s, bias_s,
                win_st, wout_st, w_sem,
                *, n_prep, n_prompt, n_blocks, blocks_per_seq, seq_len):
    tm = hc_ref.shape[0]
    i = pl.program_id(0)
    j = i - n_prep

    width = win_s.shape[2]

    def weight_pieces():
        pieces = [pltpu.make_async_copy(win_ref.at[:, k * width:(k + 1) * width],
                                        win_st.at[:, k * width:(k + 1) * width], w_sem.at[k])
                  for k in range(MIX_SEGMENTS)]
        return pieces + [pltpu.make_async_copy(wout_ref, wout_st, w_sem.at[MIX_SEGMENTS])]

    @pl.when(i == 0)
    def _():
        for copy in weight_pieces():
            copy.start()
        yab_s[...] = jnp.zeros_like(yab_s)
        carry_ref[...] = jnp.zeros_like(carry_ref)
        row_head = lax.broadcasted_iota(jnp.int32, ones_ref.shape, 0) // HEAD_DIM_B
        col_head = lax.broadcasted_iota(jnp.int32, ones_ref.shape, 1) // HEAD_DIM_B
        ones_ref[...] = jnp.where(row_head == col_head, 1.0 / HEAD_DIM_B, 0.0).astype(BF16)

    assert MIX_FINISH_SLICES <= MIX_SEGMENTS
    slice_rows = tm // MIX_FINISH_SLICES

    def out_proj_prev():
        y = jnp.dot(yab_s[...], wout_s[...], preferred_element_type=F32)
        return dict(y=y, r=lax.rsqrt(jnp.mean(y * y, axis=-1, keepdims=True) + EPS), done=0)

    def finish_slice(prev, anchor=None):
        rows = slice(prev["done"] * slice_rows, (prev["done"] + 1) * slice_rows)
        prev["done"] += 1
        r = prev["r"][rows] if anchor is None else _after(prev["r"][rows], anchor[rows, 0:1])
        o_ref[rows, :] = hv_ref[rows, :] + prev["y"][rows] * r * post_ref[...]

    def seg(k, prev):
        out = jnp.dot(n_s[...], win_s[k], preferred_element_type=F32)
        if prev["done"] < MIX_FINISH_SLICES:
            finish_slice(prev, anchor=out)
        return out

    def conv(z, z1, z2):
        return convw_ref[0:1, :] * z2 + convw_ref[1:2, :] * z1 + convw_ref[2:3, :] * z

    def per_head(piece, rows):
        return jnp.concatenate([jnp.broadcast_to(piece(h), (rows, HEAD_DIM_B)) for h in range(HEADS_B)], axis=1)

    def v_gain():
        return jnp.concatenate([vg_ref[h:h + 1, :] for h in range(HEADS_B)], axis=1)

    def chunk_bias(rows):
        b_t = b_ref[...].T
        return per_head(lambda h: b_t[0:rows, h:h + 1], rows)

    @pl.when(i == 0)
    def _():
        t_idx = lax.broadcasted_iota(jnp.int32, (CHUNK, HEADS_PER_SLAB * CHUNK), 0)
        s_idx = lax.broadcasted_iota(jnp.int32, (CHUNK, HEADS_PER_SLAB * CHUNK), 1) % CHUNK
        for k in range(N_SLABS):
            w_k = jnp.concatenate([ws_ref[k * HEADS_PER_SLAB + hd] for hd in range(HEADS_PER_SLAB)], axis=1)
            wslab_s[k] = jnp.where(t_idx >= s_idx, w_k, 0.0).astype(BF16)
        bias_s[...] = chunk_bias(CHUNK)
        pieces = weight_pieces()
        for k in range(MIX_SEGMENTS):
            pieces[k].wait()
            win_s[k] = win_st[:, k * width:(k + 1) * width].astype(BF16)
        pieces[MIX_SEGMENTS].wait()
        wout_s[...] = wout_st[...].astype(BF16)

    def group_norms(conv_out, mixed, prev):
        yab_s[:, 0:CONV_DIM] = _rms(seg(0, prev) * conv_out, ga_ref[...]).astype(BF16)
        yab_s[:, CONV_DIM:] = _rms(seg(3, prev) * mixed, gb_ref[...]).astype(BF16)

    def prompt_block():
        prev = out_proj_prev()
        n_s[...] = _rms(hc_ref[...], pre_ref[...]).astype(BF16)

        v_n = _head_rms(seg(4, prev), ones_ref[...], v_gain())
        vp_ref[0] = v_n[tm - CHUNK:, :].T
        lane_head = lax.broadcasted_iota(jnp.int32, (CHUNK, V7X_MXU_COLS), 1) // HEAD_DIM_B
        chunks = []
        for c in range(tm // CHUNK):
            slabs = []
            for k in range(N_SLABS):
                cols = slice(k * V7X_MXU_COLS, (k + 1) * V7X_MXU_COLS)
                v_ck = v_n[c * CHUNK:(c + 1) * CHUNK, cols].astype(BF16)
                bd = jnp.concatenate(
                    [jnp.where(lane_head == hd, v_ck, jnp.zeros((), BF16)) for hd in range(HEADS_PER_SLAB)], axis=0)
                slabs.append(jnp.dot(wslab_s[k], bd, preferred_element_type=F32))
            chunks.append(jnp.concatenate(slabs, axis=1) + bias_s[...])
        mixed = jnp.concatenate(chunks, axis=0)

        z = seg(1, prev) * seg(2, prev)
        row = lax.broadcasted_iota(jnp.int32, z.shape, 0)
        carry = jnp.where(j % blocks_per_seq == 0, 0.0, carry_ref[...])
        prev1 = carry[1:2, :]
        prev2 = carry[0:1, :]
        z1 = jnp.where(row == 0, prev1, pltpu.roll(z, 1, axis=0))
        z2 = jnp.where(row == 0, prev2, jnp.where(row == 1, prev1, pltpu.roll(z, 2, axis=0)))
        tail = z[tm - (CONV_K - 1):, :]
        carry_ref[...] = tail
        convp_ref[0] = tail
        group_norms(conv(z, z1, z2), mixed, prev)

    def sample_block():
        prev = out_proj_prev()
        n_s[...] = _rms(hc_ref[...], pre_ref[...]).astype(BF16)

        n_seq = tm // seq_len
        per_seq = lambda a: a.reshape(n_seq, seq_len, a.shape[-1])
        every_row = lambda a: jnp.broadcast_to(a, (n_seq, seq_len, a.shape[-1]))
        v_n = _head_rms(seg(4, prev), ones_ref[...], v_gain())
        first_seq = pl.multiple_of((j - n_prompt) * n_seq, n_seq)
        for c in range(CHUNK_DIM // V7X_LANES):
            lanes = slice(c * V7X_LANES, (c + 1) * V7X_LANES)
            vn_s[c] = v_n[:, lanes]
            for t in range(seq_len):
                vt_s[t, pl.ds(first_seq, n_seq), lanes] = vn_s[c, pl.ds(t, n_seq, stride=seq_len), :]

        @pl.when(j == n_blocks - 1)
        def _():
            for t in range(seq_len):
                vs_ref[t] = vt_s[t].T
        v_seq = per_seq(v_n)
        t_row = lax.broadcasted_iota(jnp.int32, (seq_len, CHUNK_DIM), 0)
        mixed = every_row(chunk_bias(seq_len)[None])
        for s in range(seq_len):
            w_from_s = per_head(lambda h: ws_ref[h, 0:seq_len, s:s + 1], seq_len)
            mixed = mixed + jnp.where(t_row >= s, w_from_s, 0.0)[None] * every_row(v_seq[:, s:s + 1, :])
        mixed = mixed.reshape(tm, CHUNK_DIM)

        z = seg(1, prev) * seg(2, prev)
        t = lax.broadcasted_iota(jnp.int32, z.shape, 0) % seq_len
        st = state_ref[...]
        prev1 = every_row(st[:, 1:2, :]).reshape(tm, CONV_DIM)
        prev2 = every_row(st[:, 0:1, :]).reshape(tm, CONV_DIM)
        z1 = jnp.where(t == 0, prev1, pltpu.roll(z, 1, axis=0))
        z2 = jnp.where(t == 0, prev2, jnp.where(t == 1, prev1, pltpu.roll(z, 2, axis=0)))
        convs_ref[...] = per_seq(z)[:, seq_len - (CONV_K - 1):, :]
        group_norms(conv(z, z1, z2), mixed, prev)

    @pl.when(jnp.logical_and(j >= 0, j < n_prompt))
    def _():
        prompt_block()

    @pl.when(jnp.logical_and(j >= n_prompt, j < n_blocks))
    def _():
        sample_block()

    @pl.when(j == n_blocks)
    def _():
        prev = out_proj_prev()
        for _ in range(MIX_FINISH_SLICES):
            finish_slice(prev)


def _resident(a):
    return pl.BlockSpec(a.shape, lambda i: (0,) * a.ndim, pipeline_mode=pl.Buffered(1))


def _clamped(lo, n):
    return lambda i: jnp.minimum(jnp.maximum(i - lo, 0), n - 1)


def _stream(tm, d, lo, n):
    blk = _clamped(lo, n)
    return pl.BlockSpec((tm, d), lambda i: (blk(i), 0))


def _params():
    return pltpu.CompilerParams(dimension_semantics=("arbitrary",), vmem_limit_bytes=V7X_VMEM_LIMIT_BYTES)


def _mix(h, state, batch_p, seq_p, batch_s, seq_s, mw):
    tm = MIX_TOKEN_BLOCK
    d = h.shape[1]
    n_prep = 1
    assert seq_p % tm == 0 and (batch_s * seq_s) % tm == 0 and tm % seq_s == 0
    n_prompt, n_sample = batch_p * seq_p // tm, batch_s * seq_s // tm
    n_blocks = n_prompt + n_sample
    blocks_per_seq = seq_p // tm
    seqs_per_block = tm // seq_s
    assert mw["w_in"].shape[1] == MIX_SEGMENTS * CONV_DIM
    anywhere = pl.BlockSpec(memory_space=pl.ANY)
    small = [mw["v_g"], mw["w_s"], mw["b_s"], mw["g_a"], mw["g_b"]]
    conv_w_spec = pl.BlockSpec((None,) + mw["conv_w"].shape[1:], lambda i: (0, 0, 0), pipeline_mode=pl.Buffered(1))
    lo_s = n_prep + n_prompt
    prompt_seq = _clamped(0, batch_p)
    per_prompt_seq = lambda i: (prompt_seq((i - n_prep) // blocks_per_seq), 0, 0)
    sample_blk = _clamped(lo_s, n_sample)
    per_sample_blk = pl.BlockSpec((seqs_per_block, CONV_K - 1, CONV_DIM), lambda i: (sample_blk(i), 0, 0))
    return pl.pallas_call(
        functools.partial(_mix_kernel, n_prep=n_prep, n_prompt=n_prompt, n_blocks=n_blocks,
                          blocks_per_seq=blocks_per_seq, seq_len=seq_s),
        grid=(n_prep + n_blocks + 1,),
        in_specs=[_stream(tm, d, n_prep, n_blocks), _stream(tm, d, n_prep + 1, n_blocks),
                  per_sample_blk, _resident(mw["pre"]), _resident(mw["post"]),
                  anywhere, anywhere, conv_w_spec]
                 + [_resident(a) for a in small],
        out_specs=[_stream(tm, d, n_prep + 1, n_blocks),
                   pl.BlockSpec((1, CONV_K - 1, CONV_DIM), per_prompt_seq),
                   pl.BlockSpec((1, CHUNK_DIM, CHUNK), per_prompt_seq),
                   per_sample_blk,
                   pl.BlockSpec((seq_s, CHUNK_DIM, batch_s), lambda i: (0, 0, 0))],
        out_shape=[jax.ShapeDtypeStruct(h.shape, F32),
                   jax.ShapeDtypeStruct((batch_p, CONV_K - 1, CONV_DIM), F32),
                   jax.ShapeDtypeStruct((batch_p, CHUNK_DIM, CHUNK), F32),
                   jax.ShapeDtypeStruct((batch_s, CONV_K - 1, CONV_DIM), F32),
                   jax.ShapeDtypeStruct((seq_s, CHUNK_DIM, batch_s), F32)],
        scratch_shapes=[pltpu.VMEM((MIX_SEGMENTS, d, CONV_DIM), BF16), pltpu.VMEM(mw["w_out"].shape, BF16),
                        pltpu.VMEM((tm, d), BF16), pltpu.VMEM((tm, CONV_DIM + CHUNK_DIM), BF16),
                        pltpu.VMEM((CONV_K - 1, CONV_DIM), F32),
                        pltpu.VMEM((CHUNK_DIM // V7X_LANES, tm, V7X_LANES), F32),
                        pltpu.VMEM((seq_s, batch_s, CHUNK_DIM), F32),
                        pltpu.VMEM((V7X_MXU_COLS, V7X_MXU_COLS), BF16),
                        pltpu.VMEM((N_SLABS, CHUNK, HEADS_PER_SLAB * CHUNK), BF16),
                        pltpu.VMEM((CHUNK, CHUNK_DIM), F32),
                        pltpu.VMEM(mw["w_in"].shape, F32), pltpu.VMEM(mw["w_out"].shape, F32),
                        pltpu.SemaphoreType.DMA((MIX_SEGMENTS + 1,))],
        compiler_params=_params(),
        name="mix",
    )(h, h, state, mw["pre"], mw["post"], mw["w_in"], mw["w_out"], mw["conv_w"], *small)


def _row(g):
    return g.reshape(1, -1).astype(F32)


def _mix_weights(mix_pre_g, mix_post_g, w_in, conv_w, v_norm_g, w_s, b_s, out_g_a, out_g_b, w_out):
    return dict(pre=_row(mix_pre_g), post=_row(mix_post_g), w_in=w_in, w_out=w_out,
                conv_w=conv_w.astype(F32), v_g=v_norm_g.astype(F32), w_s=w_s.astype(F32),
                b_s=b_s.astype(F32), g_a=_row(out_g_a), g_b=_row(out_g_b))


def kernel(x_prompt, x_sample, p_prompt, p_sample, state_conv, ffn1_pre_g, ffn1_post_g, ffn1_w_gate, ffn1_w_up, ffn1_w_down, mix_pre_g, mix_post_g, w_in, conv_w, v_norm_g, w_s, b_s, out_g_a, out_g_b, w_out, ffn2_pre_g, ffn2_post_g, ffn2_w_gate, ffn2_w_up, ffn2_w_down, ple_w_gate, ple_w_proj, ple_post_g):
    depth = ffn1_pre_g.shape[0]
    bp, lp, d = x_prompt.shape
    bs, ls, _ = x_sample.shape
    tm = TOKEN_BLOCK
    assert lp % tm == 0 and (bs * ls) % tm == 0 and tm % ls == 0
    assert CONV_K - 1 <= ls <= V7X_SUBLANES and (lp - 1) % CHUNK + 1 == CHUNK
    n_prompt, n_sample = bp * lp // tm, bs * ls // tm
    x = (x_prompt.reshape(bp * lp, d), x_sample.reshape(bs * ls, d))
    conv_p, conv_s, v_p, v_s = [], [], [], []
    for i in range(depth):
        mw = _mix_weights(mix_pre_g[i], mix_post_g[i], w_in[i], conv_w[i:i + 1], v_norm_g[i], w_s[i], b_s[i],
                          out_g_a[i], out_g_b[i], w_out[i])
        ple = (p_prompt[i].reshape(bp * lp, -1), p_sample[i].reshape(bs * ls, -1),
               ple_w_gate[i], ple_w_proj[i], _row(ple_post_g[i]))

        blocks = dict(n_prompt=n_prompt, n_sample=n_sample)
        h = _ffn_loop(x, _row(ffn1_pre_g[i]), _row(ffn1_post_g[i]),
                      ffn1_w_gate[i], ffn1_w_up[i], ffn1_w_down[i], name="ffn1", **blocks)
        h, cp, vp, cs, vs = _mix(h, state_conv[i].astype(F32), bp, lp, bs, ls, mw)
        x = _ffn_loop(h, _row(ffn2_pre_g[i]), _row(ffn2_post_g[i]),
                      ffn2_w_gate[i], ffn2_w_up[i], ffn2_w_down[i], ple, name="ffn2_ple", **blocks)

        conv_p.append(cp)
        conv_s.append(cs)
        v_p.append(vp.transpose(0, 2, 1).reshape(bp, CHUNK, HEADS_B, HEAD_DIM_B))
        v_s.append(vs.transpose(2, 0, 1).reshape(bs, ls, HEADS_B, HEAD_DIM_B))
    layers = lambda parts: parts[0][None] if depth == 1 else jnp.stack(parts)
    return (x[0].reshape(bp, lp, d), x[1].reshape(bs, ls, d),
            layers(conv_p), layers(conv_s), layers(v_p), layers(v_s))
```

```python
import functools

import jax
import jax.numpy as jnp
from jax import lax
from jax.experimental import pallas as pl
from jax.experimental.pallas import tpu as pltpu

CONV_K = 3
HEADS_B = 8
HEAD_DIM_B = 64
CHUNK_DIM = HEADS_B * HEAD_DIM_B
CONV_DIM = CHUNK_DIM
CHUNK = 128
EPS = 1e-6

V7X_SUBLANES = 8
V7X_LANES = 128
V7X_MXU_COLS = 256
V7X_VMEM_LIMIT_BYTES = 60 * 1024 * 1024

TOKEN_BLOCK = 512
INPUT_RING = 3
WEIGHT_RING = 4
DOUBLE_BUFFER = 2
MIX_TOKEN_BLOCK = 512
WEIGHT_TILE = V7X_MXU_COLS
HEADS_PER_SLAB = V7X_MXU_COLS // HEAD_DIM_B
N_SLABS = CHUNK_DIM // V7X_MXU_COLS
MIX_SEGMENTS = 5
FINISH_SLICES = 8
FINISH_FIRST_CHUNK = 1
MIX_FINISH_SLICES = 4

BF16 = jnp.bfloat16
F32 = jnp.float32


def _rms(x, g):
    return x * lax.rsqrt(jnp.mean(x * x, axis=-1, keepdims=True) + EPS) * g


def _dot(a, b):
    return jnp.dot(a.astype(BF16), b, preferred_element_type=F32)


def _sigmoid(x):
    return 0.5 + 0.5 * jnp.tanh(0.5 * x)


def _silu(x):
    half = 0.5 * x
    return half + half * jnp.tanh(half)


def _after(value, anchor):
    bits = lax.bitcast_convert_type(anchor, jnp.int32)
    zero = lax.shift_right_logical(lax.shift_right_logical(bits, 16), 16)
    return value + zero.astype(value.dtype)


def _ffn_loop_kernel(*refs, n_prep, n_prompt, n_blocks, with_ple):
    if with_ple:
        (x_hbm, pp_hbm, ps_hbm, pre_ref, post_ref, wg_ref, wu_ref, wd_ref, wpg_ref, wpp_ref, pleg_ref,
         op_hbm, os_hbm, wgu_s, wd_s, xn_s, a_s, xbuf, obuf, in_sem, out_sem, wst_g, wst_u, wst_d, w_sem,
         wpg_s, wpp_s, hb_s, pbuf, p_sem, wpg_st, wpp_st, e_sem) = refs
    else:
        (xp_hbm, xs_hbm, pre_ref, post_ref, wg_ref, wu_ref, wd_ref, o_hbm,
         wgu_s, wd_s, xn_s, a_s, xbuf, obuf, in_sem, out_sem, wst_g, wst_u, wst_d, w_sem) = refs
    tm = xn_s.shape[0]
    n_x, n_o = xbuf.shape[0], obuf.shape[0]
    n_slices = FINISH_SLICES // 2 if with_ple else FINISH_SLICES
    slice_rows = tm // n_slices
    stages = [("residual", s) for s in range(n_slices)]
    if with_ple:
        stages += [("embed_dots", 0)] + [("embed_add", s) for s in range(n_slices)]
    assert FINISH_FIRST_CHUNK + len(stages) <= n_prep

    def rows_of(b, prompt_hbm, sample_hbm):
        if isinstance(b, int):
            return (prompt_hbm.at[pl.ds(b * tm, tm)] if b < n_prompt else None,
                    sample_hbm.at[pl.ds((b - n_prompt) * tm, tm)] if b >= n_prompt else None)
        return (prompt_hbm.at[pl.ds(pl.multiple_of(b * tm, tm), tm)],
                sample_hbm.at[pl.ds(pl.multiple_of((b - n_prompt) * tm, tm), tm)])

    def start_either(b, make_copy, prompt_hbm, sample_hbm):
        in_prompt, in_sample = rows_of(b, prompt_hbm, sample_hbm)
        if isinstance(b, int):
            make_copy(in_prompt if in_sample is None else in_sample).start()
            return

        @pl.when(b < n_prompt)
        def _():
            make_copy(in_prompt).start()

        @pl.when(b >= n_prompt)
        def _():
            make_copy(in_sample).start()

    def x_copy(b):
        return lambda src: pltpu.make_async_copy(src, xbuf.at[b % n_x], in_sem.at[b % n_x])

    def start_x(b):
        if with_ple:
            row0 = b * tm if isinstance(b, int) else pl.multiple_of(b * tm, tm)
            x_copy(b)(x_hbm.at[pl.ds(row0, tm)]).start()
        else:
            start_either(b, x_copy(b), xp_hbm, xs_hbm)

    def wait_x(b):
        x_copy(b)((x_hbm if with_ple else xp_hbm).at[pl.ds(0, tm)]).wait()

    def p_copy(b):
        return lambda src: pltpu.make_async_copy(src, pbuf.at[b % n_o], p_sem.at[b % n_o])

    def out_copy(b):
        return lambda dst: pltpu.make_async_copy(obuf.at[b % n_o], dst, out_sem.at[b % n_o])

    def start_out(b):
        if with_ple:
            start_either(b, out_copy(b), op_hbm, os_hbm)
        else:
            row0 = b * tm if isinstance(b, int) else pl.multiple_of(b * tm, tm)
            out_copy(b)(o_hbm.at[pl.ds(row0, tm)]).start()

    def wait_out(b):
        out_copy(b)((op_hbm if with_ple else o_hbm).at[pl.ds(0, tm)]).wait()

    def gate_up_chunk(c):
        gu = jnp.dot(xn_s[...], wgu_s[c], preferred_element_type=F32)
        g, u = gu[:, 0:WEIGHT_TILE], gu[:, WEIGHT_TILE:]
        return g, (_silu(g) * u).astype(BF16)

    def down():
        y = jnp.dot(a_s[...], wd_s[...], preferred_element_type=F32)
        return dict(y=y, r=lax.rsqrt(jnp.mean(y * y, axis=-1, keepdims=True) + EPS))

    def ordered(r, rows, anchor):
        return r[rows] if anchor is None else _after(r[rows], anchor[rows, 0:1])

    def finish_stage(b, stage, s, st, anchor=None):
        rows = slice(s * slice_rows, (s + 1) * slice_rows)
        o = b % n_o
        if stage == "residual":
            h = xbuf[b % n_x, rows, :] + 0.5 * (st["y"][rows] * ordered(st["r"], rows, anchor) * post_ref[...])
            obuf[o, rows, :] = h
            if with_ple:
                hb_s[rows, :] = h.astype(BF16)
        elif stage == "embed_dots":
            gate = _sigmoid(jnp.dot(hb_s[...], wpg_s[...], preferred_element_type=F32))
            st["e"] = gate * _dot(pbuf[b % n_o], wpp_s[...])
            st["r2"] = lax.rsqrt(jnp.mean(st["e"] * st["e"], axis=-1, keepdims=True) + EPS)
        else:
            obuf[o, rows, :] = obuf[o, rows, :] + st["e"][rows] * ordered(st["r2"], rows, anchor) * pleg_ref[...]

    def weight_copies(c):
        slot, cols = c % wst_g.shape[0], slice(c * WEIGHT_TILE, (c + 1) * WEIGHT_TILE)
        return [pltpu.make_async_copy(wg_ref.at[:, cols], wst_g.at[slot], w_sem.at[0, slot]),
                pltpu.make_async_copy(wu_ref.at[:, cols], wst_u.at[slot], w_sem.at[1, slot]),
                pltpu.make_async_copy(wd_ref.at[cols, :], wst_d.at[slot], w_sem.at[2, slot])]

    def embed_weight_copies():
        return [pltpu.make_async_copy(wpg_ref, wpg_st, e_sem.at[0]), pltpu.make_async_copy(wpp_ref, wpp_st, e_sem.at[1])]

    n_w = wst_g.shape[0]

    def start_alternating(copies, first=0):
        for k, copy in enumerate(copies):
            copy.start(priority=(first + k) % 2)

    start_x(0)
    for c in range(min(n_w, n_prep)):
        start_alternating(weight_copies(c), c)
    if with_ple:
        start_either(0, p_copy(0), pp_hbm, ps_hbm)
        start_alternating(embed_weight_copies())
    wait_x(0)
    if n_blocks > 1:
        start_x(1)
    xn_s[...] = _rms(xbuf[0], pre_ref[...]).astype(BF16)
    for c in range(n_prep):
        cols = slice(c * WEIGHT_TILE, (c + 1) * WEIGHT_TILE)
        for copy in weight_copies(c):
            copy.wait()
        wgu_s[c, :, 0:WEIGHT_TILE] = wst_g[c % n_w].astype(BF16)
        wgu_s[c, :, WEIGHT_TILE:] = wst_u[c % n_w].astype(BF16)
        wd_s[cols, :] = wst_d[c % n_w].astype(BF16)
        if c + n_w < n_prep:
            start_alternating(weight_copies(c + n_w), c + n_w)
        _, a_s[:, cols] = gate_up_chunk(c)
    if with_ple:
        for copy in embed_weight_copies():
            copy.wait()
        wpg_s[...] = wpg_st[...].astype(BF16)
        wpp_s[...] = wpp_st[...].astype(BF16)

    def run_blocks():
        def step(b, carry):
            wait_x(b + 1)
            if with_ple:
                p_copy(b)(pp_hbm.at[pl.ds(0, tm)]).wait()
                start_either(b + 1, p_copy(b + 1), pp_hbm, ps_hbm)

            @pl.when(b + 2 < n_blocks)
            def _():
                start_x(b + 2)

            @pl.when(b >= n_o)
            def _():
                wait_out(b - n_o)

            st = down()
            x = _after(xbuf[(b + 1) % n_x], st["y"][:, 0:1])
            xn_s[...] = _rms(x, pre_ref[...]).astype(BF16)
            for c in range(n_prep):
                g, a_s[:, c * WEIGHT_TILE:(c + 1) * WEIGHT_TILE] = gate_up_chunk(c)
                k = c - FINISH_FIRST_CHUNK
                if 0 <= k < len(stages):
                    finish_stage(b, *stages[k], st, anchor=g)
            start_out(b)
            return carry

        lax.fori_loop(0, n_blocks - 1, step, 0)
        last = n_blocks - 1
        if with_ple:
            p_copy(last)(pp_hbm.at[pl.ds(0, tm)]).wait()
        if last >= n_o:
            wait_out(last - n_o)
        st = down()
        for stage, s in stages:
            finish_stage(last, stage, s, st)
        start_out(last)
        for b in range(max(last - n_o + 1, 0), last + 1):
            wait_out(b)

    run_blocks()


def _ffn_loop(x, pre_g, post_g, wg, wu, wd, ple=None, *, n_prompt, n_sample, name):
    tm = TOKEN_BLOCK
    d, d_ff = wg.shape
    n_prep = d_ff // WEIGHT_TILE
    n_blocks = n_prompt + n_sample
    assert n_prompt >= 1
    anywhere = pl.BlockSpec(memory_space=pl.ANY)
    weights = [_resident(pre_g), _resident(post_g), anywhere, anywhere, anywhere]
    scratch = [pltpu.VMEM((n_prep, d, 2 * WEIGHT_TILE), BF16), pltpu.VMEM((d_ff, d), BF16),
               pltpu.VMEM((tm, d), BF16), pltpu.VMEM((tm, d_ff), BF16),
               pltpu.VMEM((INPUT_RING, tm, d), F32), pltpu.VMEM((DOUBLE_BUFFER, tm, d), F32),
               pltpu.SemaphoreType.DMA((INPUT_RING,)), pltpu.SemaphoreType.DMA((DOUBLE_BUFFER,)),
               pltpu.VMEM((WEIGHT_RING, d, WEIGHT_TILE), F32), pltpu.VMEM((WEIGHT_RING, d, WEIGHT_TILE), F32),
               pltpu.VMEM((WEIGHT_RING, WEIGHT_TILE, d), F32), pltpu.SemaphoreType.DMA((3, WEIGHT_RING))]
    if ple is None:
        in_specs = [anywhere, anywhere] + weights
        args = [x[0], x[1], pre_g, post_g, wg, wu, wd]
        out_specs = anywhere
        out_shape = jax.ShapeDtypeStruct((n_blocks * tm, d), F32)
    else:
        pp, ps, wpg, wpp, ple_g = ple
        in_specs = [anywhere, anywhere, anywhere] + weights + [anywhere, anywhere, _resident(ple_g)]
        args = [x, pp, ps, pre_g, post_g, wg, wu, wd, wpg, wpp, ple_g]
        out_specs = [anywhere, anywhere]
        out_shape = [jax.ShapeDtypeStruct((n_prompt * tm, d), F32), jax.ShapeDtypeStruct((n_sample * tm, d), F32)]
        scratch += [pltpu.VMEM(wpg.shape, BF16), pltpu.VMEM(wpp.shape, BF16), pltpu.VMEM((tm, d), BF16),
                    pltpu.VMEM((DOUBLE_BUFFER, tm, pp.shape[1]), F32), pltpu.SemaphoreType.DMA((DOUBLE_BUFFER,)),
                    pltpu.VMEM(wpg.shape, F32), pltpu.VMEM(wpp.shape, F32), pltpu.SemaphoreType.DMA((2,))]
    return pl.pallas_call(
        functools.partial(_ffn_loop_kernel, n_prep=n_prep, n_prompt=n_prompt, n_blocks=n_blocks,
                          with_ple=ple is not None),
        grid=(1,),
        in_specs=in_specs,
        out_specs=out_specs,
        out_shape=out_shape,
        scratch_shapes=scratch,
        compiler_params=_params(),
        name=name,
    )(*args)


def _head_rms(v, ones_slab, g):
    vv = (v * v).astype(BF16)
    ms = jnp.concatenate(
        [jnp.dot(vv[:, k * V7X_MXU_COLS:(k + 1) * V7X_MXU_COLS], ones_slab, preferred_element_type=F32)
         for k in range(CHUNK_DIM // V7X_MXU_COLS)], axis=1)
    return v * lax.rsqrt(ms + EPS) * g


def _mix_kernel(hc_ref, hv_ref, state_ref, pre_ref, post_ref, win_ref, wout_ref, convw_ref, vg_ref,
                ws_ref, b_ref, ga_ref, gb_ref,
                o_ref, convp_ref, vp_ref, convs_ref, vs_ref,
                win_s, wout_s, n_s, yab_s, carry_ref, vn_s, vt_s, ones_ref, wslab_s, bias_s,
                win_st, wout_st, w_sem,
                *, n_prep, n_prompt, n_blocks, blocks_per_seq, seq_len):
    tm = hc_ref.shape[0]
    i = pl.program_id(0)
    j = i - n_prep

    width = win_s.shape[2]

    def weight_pieces():
        pieces = [pltpu.make_async_copy(win_ref.at[:, k * width:(k + 1) * width],
                                        win_st.at[:, k * width:(k + 1) * width], w_sem.at[k])
                  for k in range(MIX_SEGMENTS)]
        return pieces + [pltpu.make_async_copy(wout_ref, wout_st, w_sem.at[MIX_SEGMENTS])]

    @pl.when(i == 0)
    def _():
        for k, copy in enumerate(weight_pieces()):
            copy.start(priority=k % 2)
        yab_s[...] = jnp.zeros_like(yab_s)
        carry_ref[...] = jnp.zeros_like(carry_ref)
        row_head = lax.broadcasted_iota(jnp.int32, ones_ref.shape, 0) // HEAD_DIM_B
        col_head = lax.broadcasted_iota(jnp.int32, ones_ref.shape, 1) // HEAD_DIM_B
        ones_ref[...] = jnp.where(row_head == col_head, 1.0 / HEAD_DIM_B, 0.0).astype(BF16)

    assert MIX_FINISH_SLICES <= MIX_SEGMENTS
    slice_rows = tm // MIX_FINISH_SLICES

    def out_proj_prev():
        y = jnp.dot(yab_s[...], wout_s[...], preferred_element_type=F32)
        return dict(y=y, r=lax.rsqrt(jnp.mean(y * y, axis=-1, keepdims=True) + EPS), done=0)

    def finish_slice(prev, anchor=None):
        rows = slice(prev["done"] * slice_rows, (prev["done"] + 1) * slice_rows)
        prev["done"] += 1
        r = prev["r"][rows] if anchor is None else _after(prev["r"][rows], anchor[rows, 0:1])
        o_ref[rows, :] = hv_ref[rows, :] + prev["y"][rows] * r * post_ref[...]

    def seg(k, prev):
        out = jnp.dot(n_s[...], win_s[k], preferred_element_type=F32)
        if prev["done"] < MIX_FINISH_SLICES:
            finish_slice(prev, anchor=out)
        return out

    def conv(z, z1, z2):
        return convw_ref[0:1, :] * z2 + convw_ref[1:2, :] * z1 + convw_ref[2:3, :] * z

    def per_head(piece, rows):
        return jnp.concatenate([jnp.broadcast_to(piece(h), (rows, HEAD_DIM_B)) for h in range(HEADS_B)], axis=1)

    def v_gain():
        return jnp.concatenate([vg_ref[h:h + 1, :] for h in range(HEADS_B)], axis=1)

    def chunk_bias(rows):
        b_t = b_ref[...].T
        return per_head(lambda h: b_t[0:rows, h:h + 1], rows)

    @pl.when(i == 0)
    def _():
        t_idx = lax.broadcasted_iota(jnp.int32, (CHUNK, HEADS_PER_SLAB * CHUNK), 0)
        s_idx = lax.broadcasted_iota(jnp.int32, (CHUNK, HEADS_PER_SLAB * CHUNK), 1) % CHUNK
        for k in range(N_SLABS):
            w_k = jnp.concatenate([ws_ref[k * HEADS_PER_SLAB + hd] for hd in range(HEADS_PER_SLAB)], axis=1)
            wslab_s[k] = jnp.where(t_idx >= s_idx, w_k, 0.0).astype(BF16)
        bias_s[...] = chunk_bias(CHUNK)
        pieces = weight_pieces()
        for k in range(MIX_SEGMENTS):
            pieces[k].wait()
            win_s[k] = win_st[:, k * width:(k + 1) * width].astype(BF16)
        pieces[MIX_SEGMENTS].wait()
        wout_s[...] = wout_st[...].astype(BF16)

    def group_norms(conv_out, mixed, prev):
        yab_s[:, 0:CONV_DIM] = _rms(seg(0, prev) * conv_out, ga_ref[...]).astype(BF16)
        yab_s[:, CONV_DIM:] = _rms(seg(3, prev) * mixed, gb_ref[...]).astype(BF16)

    def prompt_block():
        prev = out_proj_prev()
        n_s[...] = _rms(hc_ref[...], pre_ref[...]).astype(BF16)

        v_n = _head_rms(seg(4, prev), ones_ref[...], v_gain())
        vp_ref[0] = v_n[tm - CHUNK:, :].T
        lane_head = lax.broadcasted_iota(jnp.int32, (CHUNK, V7X_MXU_COLS), 1) // HEAD_DIM_B
        chunks = []
        for c in range(tm // CHUNK):
            slabs = []
            for k in range(N_SLABS):
                cols = slice(k * V7X_MXU_COLS, (k + 1) * V7X_MXU_COLS)
                v_ck = v_n[c * CHUNK:(c + 1) * CHUNK, cols].astype(BF16)
                bd = jnp.concatenate(
                    [jnp.where(lane_head == hd, v_ck, jnp.zeros((), BF16)) for hd in range(HEADS_PER_SLAB)], axis=0)
                slabs.append(jnp.dot(wslab_s[k], bd, preferred_element_type=F32))
            chunks.append(jnp.concatenate(slabs, axis=1) + bias_s[...])
        mixed = jnp.concatenate(chunks, axis=0)

        z = seg(1, prev) * seg(2, prev)
        row = lax.broadcasted_iota(jnp.int32, z.shape, 0)
        carry = jnp.where(j % blocks_per_seq == 0, 0.0, carry_ref[...])
        prev1 = carry[1:2, :]
        prev2 = carry[0:1, :]
        z1 = jnp.where(row == 0, prev1, pltpu.roll(z, 1, axis=0))
        z2 = jnp.where(row == 0, prev2, jnp.where(row == 1, prev1, pltpu.roll(z, 2, axis=0)))
        tail = z[tm - (CONV_K - 1):, :]
        carry_ref[...] = tail
        convp_ref[0] = tail
        group_norms(conv(z, z1, z2), mixed, prev)

    def sample_block():
        prev = out_proj_prev()
        n_s[...] = _rms(hc_ref[...], pre_ref[...]).astype(BF16)

        n_seq = tm // seq_len
        per_seq = lambda a: a.reshape(n_seq, seq_len, a.shape[-1])
        every_row = lambda a: jnp.broadcast_to(a, (n_seq, seq_len, a.shape[-1]))
        v_n = _head_rms(seg(4, prev), ones_ref[...], v_gain())
        first_seq = pl.multiple_of((j - n_prompt) * n_seq, n_seq)
        for c in range(CHUNK_DIM // V7X_LANES):
            lanes = slice(c * V7X_LANES, (c + 1) * V7X_LANES)
            vn_s[c] = v_n[:, lanes]
            for t in range(seq_len):
                vt_s[t, pl.ds(first_seq, n_seq), lanes] = vn_s[c, pl.ds(t, n_seq, stride=seq_len), :]

        @pl.when(j == n_blocks - 1)
        def _():
            for t in range(seq_len):
                vs_ref[t] = vt_s[t].T
        v_seq = per_seq(v_n)
        t_row = lax.broadcasted_iota(jnp.int32, (seq_len, CHUNK_DIM), 0)
        mixed = every_row(chunk_bias(seq_len)[None])
        for s in range(seq_len):
            w_from_s = per_head(lambda h: ws_ref[h, 0:seq_len, s:s + 1], seq_len)
            mixed = mixed + jnp.where(t_row >= s, w_from_s, 0.0)[None] * every_row(v_seq[:, s:s + 1, :])
        mixed = mixed.reshape(tm, CHUNK_DIM)

        z = seg(1, prev) * seg(2, prev)
        t = lax.broadcasted_iota(jnp.int32, z.shape, 0) % seq_len
        st = state_ref[...]
        prev1 = every_row(st[:, 1:2, :]).reshape(tm, CONV_DIM)
        prev2 = every_row(st[:, 0:1, :]).reshape(tm, CONV_DIM)
        z1 = jnp.where(t == 0, prev1, pltpu.roll(z, 1, axis=0))
        z2 = jnp.where(t == 0, prev2, jnp.where(t == 1, prev1, pltpu.roll(z, 2, axis=0)))
        convs_ref[...] = per_seq(z)[:, seq_len - (CONV_K - 1):, :]
        group_norms(conv(z, z1, z2), mixed, prev)

    @pl.when(jnp.logical_and(j >= 0, j < n_prompt))
    def _():
        prompt_block()

    @pl.when(jnp.logical_and(j >= n_prompt, j < n_blocks))
    def _():
        sample_block()

    @pl.when(j == n_blocks)
    def _():
        prev = out_proj_prev()
        for _ in range(MIX_FINISH_SLICES):
            finish_slice(prev)


def _resident(a):
    return pl.BlockSpec(a.shape, lambda i: (0,) * a.ndim, pipeline_mode=pl.Buffered(1))


def _clamped(lo, n):
    return lambda i: jnp.minimum(jnp.maximum(i - lo, 0), n - 1)


def _stream(tm, d, lo, n):
    blk = _clamped(lo, n)
    return pl.BlockSpec((tm, d), lambda i: (blk(i), 0))


def _params():
    return pltpu.CompilerParams(dimension_semantics=("arbitrary",), vmem_limit_bytes=V7X_VMEM_LIMIT_BYTES)


def _mix(h, state, batch_p, seq_p, batch_s, seq_s, mw):
    tm = MIX_TOKEN_BLOCK
    d = h.shape[1]
    n_prep = 1
    assert seq_p % tm == 0 and (batch_s * seq_s) % tm == 0 and tm % seq_s == 0
    n_prompt, n_sample = batch_p * seq_p // tm, batch_s * seq_s // tm
    n_blocks = n_prompt + n_sample
    blocks_per_seq = seq_p // tm
    seqs_per_block = tm // seq_s
    assert mw["w_in"].shape[1] == MIX_SEGMENTS * CONV_DIM
    anywhere = pl.BlockSpec(memory_space=pl.ANY)
    small = [mw["v_g"], mw["w_s"], mw["b_s"], mw["g_a"], mw["g_b"]]
    conv_w_spec = pl.BlockSpec((None,) + mw["conv_w"].shape[1:], lambda i: (0, 0, 0), pipeline_mode=pl.Buffered(1))
    lo_s = n_prep + n_prompt
    prompt_seq = _clamped(0, batch_p)
    per_prompt_seq = lambda i: (prompt_seq((i - n_prep) // blocks_per_seq), 0, 0)
    sample_blk = _clamped(lo_s, n_sample)
    per_sample_blk = pl.BlockSpec((seqs_per_block, CONV_K - 1, CONV_DIM), lambda i: (sample_blk(i), 0, 0))
    return pl.pallas_call(
        functools.partial(_mix_kernel, n_prep=n_prep, n_prompt=n_prompt, n_blocks=n_blocks,
                          blocks_per_seq=blocks_per_seq, seq_len=seq_s),
        grid=(n_prep + n_blocks + 1,),
        in_specs=[_stream(tm, d, n_prep, n_blocks), _stream(tm, d, n_prep + 1, n_blocks),
                  per_sample_blk, _resident(mw["pre"]), _resident(mw["post"]),
                  anywhere, anywhere, conv_w_spec]
                 + [_resident(a) for a in small],
        out_specs=[_stream(tm, d, n_prep + 1, n_blocks),
                   pl.BlockSpec((1, CONV_K - 1, CONV_DIM), per_prompt_seq),
                   pl.BlockSpec((1, CHUNK_DIM, CHUNK), per_prompt_seq),
                   per_sample_blk,
                   pl.BlockSpec((seq_s, CHUNK_DIM, batch_s), lambda i: (0, 0, 0))],
        out_shape=[jax.ShapeDtypeStruct(h.shape, F32),
                   jax.ShapeDtypeStruct((batch_p, CONV_K - 1, CONV_DIM), F32),
                   jax.ShapeDtypeStruct((batch_p, CHUNK_DIM, CHUNK), F32),
                   jax.ShapeDtypeStruct((batch_s, CONV_K - 1, CONV_DIM), F32),
                   jax.ShapeDtypeStruct((seq_s, CHUNK_DIM, batch_s), F32)],
        scratch_shapes=[pltpu.VMEM((MIX_SEGMENTS, d, CONV_DIM), BF16), pltpu.VMEM(mw["w_out"].shape, BF16),
                        pltpu.VMEM((tm, d), BF16), pltpu.VMEM((tm, CONV_DIM + CHUNK_DIM), BF16),
                        pltpu.VMEM((CONV_K - 1, CONV_DIM), F32),
                        pltpu.VMEM((CHUNK_DIM // V7X_LANES, tm, V7X_LANES), F32),
                        pltpu.VMEM((seq_s, batch_s, CHUNK_DIM), F32),
                        pltpu.VMEM((V7X_MXU_COLS, V7X_MXU_COLS), BF16),
                        pltpu.VMEM((N_SLABS, CHUNK, HEADS_PER_SLAB * CHUNK), BF16),
                        pltpu.VMEM((CHUNK, CHUNK_DIM), F32),
                        pltpu.VMEM(mw["w_in"].shape, F32), pltpu.VMEM(mw["w_out"].shape, F32),
                        pltpu.SemaphoreType.DMA((MIX_SEGMENTS + 1,))],
        compiler_params=_params(),
        name="mix",
    )(h, h, state, mw["pre"], mw["post"], mw["w_in"], mw["w_out"], mw["conv_w"], *small)


def _row(g):
    return g.reshape(1, -1).astype(F32)


def _mix_weights(mix_pre_g, mix_post_g, w_in, conv_w, v_norm_g, w_s, b_s, out_g_a, out_g_b, w_out):
    return dict(pre=_row(mix_pre_g), post=_row(mix_post_g), w_in=w_in, w_out=w_out,
                conv_w=conv_w.astype(F32), v_g=v_norm_g.astype(F32), w_s=w_s.astype(F32),
                b_s=b_s.astype(F32), g_a=_row(out_g_a), g_b=_row(out_g_b))


def kernel(x_prompt, x_sample, p_prompt, p_sample, state_conv, ffn1_pre_g, ffn1_post_g, ffn1_w_gate, ffn1_w_up, ffn1_w_down, mix_pre_g, mix_post_g, w_in, conv_w, v_norm_g, w_s, b_s, out_g_a, out_g_b, w_out, ffn2_pre_g, ffn2_post_g, ffn2_w_gate, ffn2_w_up, ffn2_w_down, ple_w_gate, ple_w_proj, ple_post_g):
    depth = ffn1_pre_g.shape[0]
    bp, lp, d = x_prompt.shape
    bs, ls, _ = x_sample.shape
    tm = TOKEN_BLOCK
    assert lp % tm == 0 and (bs * ls) % tm == 0 and tm % ls == 0
    assert CONV_K - 1 <= ls <= V7X_SUBLANES and (lp - 1) % CHUNK + 1 == CHUNK
    n_prompt, n_sample = bp * lp // tm, bs * ls // tm
    x = (x_prompt.reshape(bp * lp, d), x_sample.reshape(bs * ls, d))
    conv_p, conv_s, v_p, v_s = [], [], [], []
    for i in range(depth):
        mw = _mix_weights(mix_pre_g[i], mix_post_g[i], w_in[i], conv_w[i:i + 1], v_norm_g[i], w_s[i], b_s[i],
                          out_g_a[i], out_g_b[i], w_out[i])
        ple = (p_prompt[i].reshape(bp * lp, -1), p_sample[i].reshape(bs * ls, -1),
               ple_w_gate[i], ple_w_proj[i], _row(ple_post_g[i]))

        blocks = dict(n_prompt=n_prompt, n_sample=n_sample)
        h = _ffn_loop(x, _row(ffn1_pre_g[i]), _row(ffn1_post_g[i]),
                      ffn1_w_gate[i], ffn1_w_up[i], ffn1_w_down[i], name="ffn1", **blocks)
        h, cp, vp, cs, vs = _mix(h, state_conv[i].astype(F32), bp, lp, bs, ls, mw)
        x = _ffn_loop(h, _row(ffn2_pre_g[i]), _row(ffn2_post_g[i]),
                      ffn2_w_gate[i], ffn2_w_up[i], ffn2_w_down[i], ple, name="ffn2_ple", **blocks)

        conv_p.append(cp)
        conv_s.append(cs)
        v_p.append(vp.transpose(0, 2, 1).reshape(bp, CHUNK, HEADS_B, HEAD_DIM_B))
        v_s.append(vs.transpose(2, 0, 1).reshape(bs, ls, HEADS_B, HEAD_DIM_B))
    layers = lambda parts: parts[0][None] if depth == 1 else jnp.stack(parts)
    return (x[0].reshape(bp, lp, d), x[1].reshape(bs, ls, d),
            layers(conv_p), layers(conv_s), layers(v_p), layers(v_s))
```
